```python
import math
import jax, jax.numpy as jnp
from jax import lax
import numpy as np

D_MODEL = 1024
BATCH = 4
SEQ = 4096
DEPTH = 1

MIX_WIDTH = D_MODEL
D_CONV = MIX_WIDTH // 2
D_SSM = MIX_WIDTH - D_CONV
CONV_HEAD_DIM = 64
CONV_HEADS = D_CONV // CONV_HEAD_DIM
CONV_WIDTH = 3
SSM_GROUP_CH = 16
SSM_GROUPS = D_SSM // SSM_GROUP_CH
SSM_STATE = 64
D_FF = 256 * ((8 * D_MODEL // 3 + 255) // 256)
N_MOD = 9
EPS = 1e-6
DT_MIN = 1e-3
DT_MAX = 1e-1

kernel_name = "hymba_conv_s5_macaron_adaln_layer"


def rms_norm(x, gain):
    xf = x.astype(jnp.float32)
    y = xf * lax.rsqrt(jnp.mean(xf * xf, axis=-1, keepdims=True) + EPS)
    return (y * gain.astype(jnp.float32)).astype(x.dtype)


def modulate(h, shift, scale):
    return h * (1 + scale[:, None, :]) + shift[:, None, :]


def swiglu(h, w_gate, w_up, w_down):
    return (jax.nn.silu(h @ w_gate) * (h @ w_up)) @ w_down


def short_conv_mixer(b_gate, c_gate, v, conv_w):
    z = c_gate * v
    z = lax.conv_general_dilated(
        z, conv_w[:, None, :].astype(z.dtype),
        window_strides=(1,), padding=[(CONV_WIDTH - 1, 0)],
        dimension_numbers=('NWC', 'WIO', 'NWC'),
        feature_group_count=D_CONV)
    return b_gate * z


def _ssm_combine(left, right):
    a_l, b_l = left
    a_r, b_r = right
    return a_r * a_l, a_r * b_l + b_r


def s5_mixer(u, lambda_re, lambda_im, log_dt, b_re, b_im, c_re, c_im, d_skip, glu_w, glu_b):
    bsz, seq, _ = u.shape
    uf = u.astype(jnp.float32).reshape(bsz, seq, SSM_GROUPS, SSM_GROUP_CH)
    lam = lax.complex(lambda_re.astype(jnp.float32), lambda_im.astype(jnp.float32))
    dt = jnp.exp(log_dt.astype(jnp.float32))[:, None]
    lam_bar = jnp.exp(lam * dt)
    b = lax.complex(b_re.astype(jnp.float32), b_im.astype(jnp.float32))
    b_bar = ((lam_bar - 1) / lam)[..., None] * b
    bu = jnp.einsum('gpc,bsgc->bsgp', b_bar, uf.astype(jnp.complex64))
    a = jnp.broadcast_to(lam_bar, bu.shape)
    _, states = lax.associative_scan(_ssm_combine, (a, bu), axis=1)
    c = lax.complex(c_re.astype(jnp.float32), c_im.astype(jnp.float32))
    y = jnp.einsum('gcp,bsgp->bsgc', c, states).real \
        + d_skip.astype(jnp.float32).reshape(SSM_GROUPS, SSM_GROUP_CH) * uf
    y = jax.nn.gelu(y.reshape(bsz, seq, D_SSM))
    y = y * jax.nn.sigmoid(y @ glu_w.astype(jnp.float32) + glu_b.astype(jnp.float32))
    return y.astype(u.dtype)


def setup_inputs(seed: int = 0) -> dict:
    key = jax.random.key(seed)
    ks = jax.random.split(key, 32)
    f32 = jnp.float32

    def nrm(k, shape, std):
        return jax.random.normal(k, shape, f32) * std

    def gain(k, shape):
        return 1.0 + 0.05 * jax.random.normal(k, shape, f32)

    L = DEPTH
    n_idx = jnp.arange(SSM_STATE, dtype=f32)
    return {
        "x": nrm(ks[0], (BATCH, SEQ, D_MODEL), 1.0),
        "cond": nrm(ks[1], (BATCH, D_MODEL), 1.0),
        "w_mod": nrm(ks[2], (L, D_MODEL, N_MOD * D_MODEL), 0.5 * D_MODEL ** -0.5),
        "b_mod": nrm(ks[3], (L, N_MOD * D_MODEL), 0.02),
        "ffn1_norm": gain(ks[4], (L, D_MODEL)),
        "ffn1_w_gate": nrm(ks[5], (L, D_MODEL, D_FF), D_MODEL ** -0.5),
        "ffn1_w_up": nrm(ks[6], (L, D_MODEL, D_FF), D_MODEL ** -0.5),
        "ffn1_w_down": nrm(ks[7], (L, D_FF, D_MODEL), D_FF ** -0.5),
        "mix_norm": gain(ks[8], (L, D_MODEL)),
        "w_in": nrm(ks[9], (L, D_MODEL, 3 * D_CONV + D_SSM), D_MODEL ** -0.5),
        "conv_w": nrm(ks[10], (L, CONV_WIDTH, D_CONV), CONV_WIDTH ** -0.5),
        "lambda_re": -0.5 + 0.01 * jax.random.normal(ks[11], (L, SSM_GROUPS, SSM_STATE), f32),
        "lambda_im": math.pi * n_idx + 0.01 * jax.random.normal(ks[12], (L, SSM_GROUPS, SSM_STATE), f32),
        "log_dt": jax.random.uniform(ks[13], (L, SSM_GROUPS), f32, math.log(DT_MIN), math.log(DT_MAX)),
        "ssm_b_re": nrm(ks[14], (L, SSM_GROUPS, SSM_STATE, SSM_GROUP_CH), (2 * SSM_GROUP_CH) ** -0.5),
        "ssm_b_im": nrm(ks[15], (L, SSM_GROUPS, SSM_STATE, SSM_GROUP_CH), (2 * SSM_GROUP_CH) ** -0.5),
        "ssm_c_re": nrm(ks[16], (L, SSM_GROUPS, SSM_GROUP_CH, SSM_STATE), SSM_STATE ** -0.5),
        "ssm_c_im": nrm(ks[17], (L, SSM_GROUPS, SSM_GROUP_CH, SSM_STATE), SSM_STATE ** -0.5),
        "ssm_d": nrm(ks[18], (L, D_SSM), 1.0),
        "glu_w": nrm(ks[19], (L, D_SSM, D_SSM), D_SSM ** -0.5),
        "glu_b": nrm(ks[20], (L, D_SSM), 0.02),
        "out_norm_conv": gain(ks[21], (L, D_CONV)),
        "out_norm_ssm": gain(ks[22], (L, D_SSM)),
        "w_out": nrm(ks[23], (L, MIX_WIDTH, D_MODEL), MIX_WIDTH ** -0.5),
        "ffn2_norm": gain(ks[24], (L, D_MODEL)),
        "ffn2_w_gate": nrm(ks[25], (L, D_MODEL, D_FF), D_MODEL ** -0.5),
        "ffn2_w_up": nrm(ks[26], (L, D_MODEL, D_FF), D_MODEL ** -0.5),
        "ffn2_w_down": nrm(ks[27], (L, D_FF, D_MODEL), D_FF ** -0.5),
        "final_norm": gain(ks[28], (D_MODEL,)),
    }


def reference(x, cond, w_mod, b_mod, ffn1_norm, ffn1_w_gate, ffn1_w_up, ffn1_w_down,
              mix_norm, w_in, conv_w, lambda_re, lambda_im, log_dt,
              ssm_b_re, ssm_b_im, ssm_c_re, ssm_c_im, ssm_d, glu_w, glu_b,
              out_norm_conv, out_norm_ssm, w_out,
              ffn2_norm, ffn2_w_gate, ffn2_w_up, ffn2_w_down, final_norm):
    c_act = jax.nn.silu(cond)
    for l in range(DEPTH):
        mod = c_act @ w_mod[l] + b_mod[l]
        (sh1, sc1, g1, sh2, sc2, g2, sh3, sc3, g3) = jnp.split(mod, N_MOD, axis=-1)

        h = modulate(rms_norm(x, ffn1_norm[l]), sh1, sc1)
        x = x + 0.5 * g1[:, None, :] * swiglu(h, ffn1_w_gate[l], ffn1_w_up[l], ffn1_w_down[l])

        h = modulate(rms_norm(x, mix_norm[l]), sh2, sc2)
        proj = h @ w_in[l]
        b_gate, c_gate, v, u = jnp.split(proj, [D_CONV, 2 * D_CONV, 3 * D_CONV], axis=-1)
        y_conv = short_conv_mixer(b_gate, c_gate, v, conv_w[l])
        y_ssm = s5_mixer(u, lambda_re[l], lambda_im[l], log_dt[l], ssm_b_re[l], ssm_b_im[l],
                         ssm_c_re[l], ssm_c_im[l], ssm_d[l], glu_w[l], glu_b[l])
        y_mix = jnp.concatenate([rms_norm(y_conv, out_norm_conv[l]),
                                 rms_norm(y_ssm, out_norm_ssm[l])], axis=-1)
        x = x + g2[:, None, :] * (y_mix @ w_out[l])

        h = modulate(rms_norm(x, ffn2_norm[l]), sh3, sc3)
        x = x + 0.5 * g3[:, None, :] * swiglu(h, ffn2_w_gate[l], ffn2_w_up[l], ffn2_w_down[l])
    return rms_norm(x, final_norm)
```

```python
import functools
import math

import jax
import jax.numpy as jnp
from jax import lax
from jax.experimental import pallas as pl
from jax.experimental.pallas import tpu as pltpu

D_MODEL = 1024
D_CONV = 512
D_SSM = 512
CONV_WIDTH = 3
SSM_GROUP_CH = 16
SSM_GROUPS = 32
SSM_STATE = 64
D_FF = 2816
N_MOD = 9
EPS = 1e-6

SUBLANES = 8
LANES = 128
VMEM_LIMIT_BYTES = 56 * 1024 * 1024

FFN_ROWS = 512
FFN_COLS = 256
MIX_ROWS = 512
SSM_CHUNK = 4
GROUPS_PER_SLAB = LANES // SSM_GROUP_CH
N_SLABS = D_SSM // LANES
SLAB_STATE = GROUPS_PER_SLAB * SSM_STATE
SCAN_STEPS = (1, 2, 4)

BF16 = jnp.bfloat16
F32 = jnp.float32


def _dot(a, b):
    return jnp.dot(a, b, preferred_element_type=F32)


def _sigmoid(x):
    return 1.0 / (1.0 + jnp.exp(-x))


def _rms_norm(x, gain):
    ms = jnp.mean(x * x, axis=-1, keepdims=True)
    return x * lax.rsqrt(ms + EPS) * gain


def _gelu_tanh(x):
    c = math.sqrt(2.0 / math.pi)
    return x * (0.5 * (1.0 + jnp.tanh(c * (x + 0.044715 * (x * x * x)))))


def _mod_kernel(c_ref, w_ref, b_ref, o_ref):
    c = c_ref[...]
    c = c * _sigmoid(c)
    o_ref[...] = _dot(c.astype(BF16), w_ref[...].astype(BF16)) + b_ref[...]


def _modulation(cond, w_mod, b_mod):
    bsz = cond.shape[0]
    rows = -(-bsz // SUBLANES) * SUBLANES
    cond_p = jnp.pad(cond, ((0, rows - bsz), (0, 0)))
    n = w_mod.shape[1]
    tn = D_MODEL
    out = pl.pallas_call(
        _mod_kernel,
        grid=(n // tn,),
        in_specs=[
            pl.BlockSpec((rows, D_MODEL), lambda j: (0, 0)),
            pl.BlockSpec((D_MODEL, tn), lambda j: (0, j)),
            pl.BlockSpec((1, tn), lambda j: (0, j)),
        ],
        out_specs=pl.BlockSpec((rows, tn), lambda j: (0, j)),
        out_shape=jax.ShapeDtypeStruct((rows, n), F32),
        compiler_params=pltpu.CompilerParams(
            dimension_semantics=("arbitrary",), vmem_limit_bytes=VMEM_LIMIT_BYTES),
        name="adaln_mod",
    )(cond_p, w_mod, b_mod.reshape(1, n))
    return out[:bsz].reshape(bsz, N_MOD, D_MODEL)


def _ffn_kernel(x_ref, mod_ref, gain_ref, wg_ref, wu_ref, wd_ref, fgain_ref, o_ref,
                act_ref, *, mod_base, final_norm):
    x = x_ref[...]
    shift = mod_ref[mod_base:mod_base + 1, :]
    scale = mod_ref[mod_base + 1:mod_base + 2, :]
    gate = mod_ref[mod_base + 2:mod_base + 3, :]
    h = (_rms_norm(x, gain_ref[...]) * (1.0 + scale) + shift).astype(BF16)
    for f in range(D_FF // FFN_COLS):
        cols = slice(f * FFN_COLS, (f + 1) * FFN_COLS)
        g = _dot(h, wg_ref[:, cols])
        u = _dot(h, wu_ref[:, cols])
        act_ref[:, cols] = (g * _sigmoid(g) * u).astype(BF16)
    y = _dot(act_ref[...], wd_ref[...])
    out = x + (0.5 * gate) * y
    if final_norm:
        out = _rms_norm(out, fgain_ref[...])
    o_ref[...] = out


def _const_spec(shape):
    zeros = (0,) * len(shape)
    return pl.BlockSpec(shape, lambda b, t: zeros, pipeline_mode=pl.Buffered(1))


def _ffn(x, mod, gain, w_gate, w_up, w_down, final_gain, *, mod_base, final_norm):
    bsz, seq, _ = x.shape
    tm = FFN_ROWS
    kernel = functools.partial(_ffn_kernel, mod_base=mod_base, final_norm=final_norm)
    return pl.pallas_call(
        kernel,
        grid=(bsz, seq // tm),
        in_specs=[
            pl.BlockSpec((None, tm, D_MODEL), lambda b, t: (b, t, 0)),
            pl.BlockSpec((None, N_MOD, D_MODEL), lambda b, t: (b, 0, 0)),
            _const_spec((1, D_MODEL)),
            _const_spec((D_MODEL, D_FF)),
            _const_spec((D_MODEL, D_FF)),
            _const_spec((D_FF, D_MODEL)),
            _const_spec((1, D_MODEL)),
        ],
        out_specs=pl.BlockSpec((None, tm, D_MODEL), lambda b, t: (b, t, 0)),
        out_shape=jax.ShapeDtypeStruct(x.shape, F32),
        scratch_shapes=[pltpu.VMEM((tm, D_FF), BF16)],
        compiler_params=pltpu.CompilerParams(
            dimension_semantics=("arbitrary", "arbitrary"),
            vmem_limit_bytes=VMEM_LIMIT_BYTES),
        name="ffn_final" if final_norm else "ffn",
    )(x, mod, gain.reshape(1, D_MODEL), w_gate, w_up, w_down,
      final_gain.reshape(1, D_MODEL))


def _mixer_kernel(x_ref, mod_ref, gain_ref, win_ref, convw_ref, toep_ref, wb_ref, wc_ref,
                  tab_ref, dskip_ref, gluw_ref, glub_ref, gain_c_ref, gain_s_ref, wout_ref,
                  o_ref, zbuf, ubuf, ybuf, hbuf, carry, ymix):
    tm = MIX_ROWS
    chunk = SSM_CHUNK
    m_rows = tm // chunk
    halo = SUBLANES

    @pl.when(pl.program_id(1) == 0)
    def _():
        zbuf[0:halo, :] = jnp.zeros((halo, D_CONV), F32)
        ubuf[:, 0:halo, :] = jnp.zeros((N_SLABS, halo, LANES), F32)
        carry[...] = jnp.zeros(carry.shape, F32)

    x = x_ref[...]
    shift = mod_ref[3:4, :]
    scale = mod_ref[4:5, :]
    gate = mod_ref[5:6, :]
    h = (_rms_norm(x, gain_ref[...]) * (1.0 + scale) + shift).astype(BF16)

    b_gate = _dot(h, win_ref[:, 0:D_CONV])
    c_gate = _dot(h, win_ref[:, D_CONV:2 * D_CONV])
    v = _dot(h, win_ref[:, 2 * D_CONV:3 * D_CONV])
    z = c_gate * v
    zbuf[halo:halo + tm, :] = z
    z1 = zbuf[halo - 1:halo - 1 + tm, :]
    z2 = zbuf[halo - 2:halo - 2 + tm, :]
    y_conv = b_gate * (convw_ref[0:1, :] * z2 + convw_ref[1:2, :] * z1 + convw_ref[2:3, :] * z)
    zbuf[0:halo, :] = zbuf[tm:tm + halo, :]
    ymix[:, 0:D_CONV] = _rms_norm(y_conv, gain_c_ref[...]).astype(BF16)

    u = _dot(h, win_ref[:, 3 * D_CONV:])
    for q in range(N_SLABS):
        ubuf[q, halo:halo + tm, :] = u[:, q * LANES:(q + 1) * LANES]
    for q in range(N_SLABS):
        cur = [ubuf[q, pl.ds(halo + s, m_rows, stride=chunk), :] for s in range(chunk)]
        prv = [ubuf[q, pl.ds(halo - chunk + s, m_rows, stride=chunk), :] for s in range(chunk)]
        u_cur = jnp.concatenate(cur, axis=1).astype(BF16)
        u_prv = jnp.concatenate(prv, axis=1).astype(BF16)
        ubuf[q, 0:halo, :] = ubuf[q, tm:tm + halo, :]
        inj = _dot(u_prv, wb_ref[q])
        for i in range(SLAB_STATE // LANES):
            lanes_re = slice(i * LANES, (i + 1) * LANES)
            lanes_im = slice(SLAB_STATE + i * LANES, SLAB_STATE + (i + 1) * LANES)
            re = inj[:, lanes_re].reshape(m_rows // SUBLANES, SUBLANES, LANES)
            im = inj[:, lanes_im].reshape(m_rows // SUBLANES, SUBLANES, LANES)
            for k, step in enumerate(SCAN_STEPS):
                a_re = tab_ref[q, 2 * k, :, lanes_re]
                a_im = tab_ref[q, 2 * k + 1, :, lanes_re]
                s_re = pltpu.roll(re, step, axis=1)
                s_im = pltpu.roll(im, step, axis=1)
                re, im = re + a_re * s_re - a_im * s_im, im + a_re * s_im + a_im * s_re
            p_re = tab_ref[q, 2 * len(SCAN_STEPS), :, lanes_re]
            p_im = tab_ref[q, 2 * len(SCAN_STEPS) + 1, :, lanes_re]
            c_re = carry[q:q + 1, lanes_re]
            c_im = carry[q:q + 1, lanes_im]
            for r in range(m_rows // SUBLANES):
                rows = slice(r * SUBLANES, (r + 1) * SUBLANES)
                b_re = jnp.broadcast_to(c_re, (SUBLANES, LANES))
                b_im = jnp.broadcast_to(c_im, (SUBLANES, LANES))
                f_re = re[r] + p_re * b_re - p_im * b_im
                f_im = im[r] + p_re * b_im + p_im * b_re
                hbuf[rows, lanes_re] = f_re
                hbuf[rows, lanes_im] = f_im
                c_re = f_re[SUBLANES - 1:SUBLANES, :]
                c_im = f_im[SUBLANES - 1:SUBLANES, :]
            carry[q:q + 1, lanes_re] = c_re
            carry[q:q + 1, lanes_im] = c_im
        y2 = _dot(u_cur, toep_ref[q]) + _dot(hbuf[...].astype(BF16), wc_ref[q])
        for s in range(chunk):
            ybuf[q, pl.ds(s, m_rows, stride=chunk), :] = y2[:, s * LANES:(s + 1) * LANES]
    y = jnp.concatenate([ybuf[q] for q in range(N_SLABS)], axis=1) + dskip_ref[...] * u
    y = _gelu_tanh(y)
    y = y * _sigmoid(_dot(y.astype(BF16), gluw_ref[...]) + glub_ref[...])
    ymix[:, D_CONV:] = _rms_norm(y, gain_s_ref[...]).astype(BF16)

    o_ref[...] = x + gate * _dot(ymix[...], wout_ref[...])


def _mixer(x, mod, gain, w_in, conv_w, toep, wb, wc, tab, d_skip, glu_w, glu_b,
           gain_c, gain_s, w_out):
    bsz, seq, _ = x.shape
    tm = MIX_ROWS
    m_rows = tm // SSM_CHUNK
    return pl.pallas_call(
        _mixer_kernel,
        grid=(bsz, seq // tm),
        in_specs=[
            pl.BlockSpec((None, tm, D_MODEL), lambda b, t: (b, t, 0)),
            pl.BlockSpec((None, N_MOD, D_MODEL), lambda b, t: (b, 0, 0)),
            _const_spec((1, D_MODEL)),
            _const_spec(w_in.shape),
            _const_spec(conv_w.shape),
            _const_spec(toep.shape),
            _const_spec(wb.shape),
            _const_spec(wc.shape),
            _const_spec(tab.shape),
            _const_spec((1, D_SSM)),
            _const_spec(glu_w.shape),
            _const_spec((1, D_SSM)),
            _const_spec((1, D_CONV)),
            _const_spec((1, D_SSM)),
            _const_spec(w_out.shape),
        ],
        out_specs=pl.BlockSpec((None, tm, D_MODEL), lambda b, t: (b, t, 0)),
        out_shape=jax.ShapeDtypeStruct(x.shape, F32),
        scratch_shapes=[
            pltpu.VMEM((SUBLANES + tm, D_CONV), F32),
            pltpu.VMEM((N_SLABS, SUBLANES + tm, LANES), F32),
            pltpu.VMEM((N_SLABS, tm, LANES), F32),
            pltpu.VMEM((m_rows, 2 * SLAB_STATE), F32),
            pltpu.VMEM((N_SLABS, 2 * SLAB_STATE), F32),
            pltpu.VMEM((tm, D_MODEL), BF16),
        ],
        compiler_params=pltpu.CompilerParams(
            dimension_semantics=("arbitrary", "arbitrary"),
            vmem_limit_bytes=VMEM_LIMIT_BYTES),
        name="mixer",
    )(x, mod, gain.reshape(1, D_MODEL), w_in, conv_w, toep, wb, wc, tab,
      d_skip.reshape(1, D_SSM), glu_w, glu_b.reshape(1, D_SSM),
      gain_c.reshape(1, D_CONV), gain_s.reshape(1, D_SSM), w_out)


def _cmul(ar, ai, br, bi):
    return ar * br - ai * bi, ar * bi + ai * br


def _ssm_operators(lambda_re, lambda_im, log_dt, b_re, b_im, c_re, c_im):
    L = SSM_CHUNK
    hi = lax.Precision.HIGHEST
    lre, lim = lambda_re.astype(F32), lambda_im.astype(F32)
    dt = jnp.exp(log_dt.astype(F32))[:, None]
    mag = jnp.exp(lre * dt)
    bar_re, bar_im = mag * jnp.cos(lim * dt), mag * jnp.sin(lim * dt)
    den = lre * lre + lim * lim
    coef_re = ((bar_re - 1.0) * lre + bar_im * lim) / den
    coef_im = (bar_im * lre - (bar_re - 1.0) * lim) / den
    bb_re, bb_im = _cmul(coef_re[..., None], coef_im[..., None],
                         b_re.astype(F32), b_im.astype(F32))
    n_pow = SUBLANES * L
    pows = [(jnp.ones_like(bar_re), jnp.zeros_like(bar_im))]
    for _ in range(n_pow):
        pows.append(_cmul(pows[-1][0], pows[-1][1], bar_re, bar_im))
    pw_re = jnp.stack([p[0] for p in pows])
    pw_im = jnp.stack([p[1] for p in pows])
    cre, cim = c_re.astype(F32), c_im.astype(F32)

    eye = jnp.eye(GROUPS_PER_SLAB, dtype=F32)

    cp_re, cp_im = _cmul(cre[None], cim[None], pw_re[:L, :, None, :], pw_im[:L, :, None, :])
    taps = (jnp.einsum('mgcp,gpj->mgcj', cp_re, bb_re, precision=hi)
            - jnp.einsum('mgcp,gpj->mgcj', cp_im, bb_im, precision=hi))
    lag = jnp.arange(L)[None, :] - jnp.arange(L)[:, None]
    kst = jnp.where((lag >= 0)[:, :, None, None, None], taps[jnp.clip(lag, 0, L - 1)], 0.0)
    kst = kst.reshape(L, L, N_SLABS, GROUPS_PER_SLAB, SSM_GROUP_CH, SSM_GROUP_CH)
    toep = jnp.einsum('stqgcj,gh->qsgjthc', kst, eye)
    toep = toep.reshape(N_SLABS, L * LANES, L * LANES)

    rev_re = jnp.stack([pows[L - 1 - s][0] for s in range(L)])
    rev_im = jnp.stack([pows[L - 1 - s][1] for s in range(L)])
    wre, wim = _cmul(rev_re[..., None], rev_im[..., None], bb_re[None], bb_im[None])
    wbs = jnp.stack([wre, wim], axis=1)
    wbs = wbs.reshape(L, 2, N_SLABS, GROUPS_PER_SLAB, SSM_STATE, SSM_GROUP_CH)
    wb = jnp.einsum('srqgpj,gh->qsgjrhp', wbs, eye)
    wb = wb.reshape(N_SLABS, L * LANES, 2 * SLAB_STATE)

    ore, oim = _cmul(cre[None], cim[None], pw_re[1:L + 1, :, None, :], pw_im[1:L + 1, :, None, :])
    wcs = jnp.stack([ore, -oim], axis=1)
    wcs = wcs.reshape(L, 2, N_SLABS, GROUPS_PER_SLAB, SSM_GROUP_CH, SSM_STATE)
    wc = jnp.einsum('srqgcp,gh->qrgpshc', wcs, eye)
    wc = wc.reshape(N_SLABS, 2 * SLAB_STATE, L * LANES)

    row = jnp.arange(SUBLANES)
    planes = []
    for step in SCAN_STEPS:
        keep = (row >= step)[:, None, None]
        planes.append(jnp.where(keep, pw_re[L * step][None], 0.0))
        planes.append(jnp.where(keep, pw_im[L * step][None], 0.0))
    planes.append(pw_re[L * (row + 1)])
    planes.append(pw_im[L * (row + 1)])
    tab = jnp.stack(planes)
    tab = tab.reshape(len(planes), SUBLANES, N_SLABS, SLAB_STATE).transpose(2, 0, 1, 3)
    return toep.astype(BF16), wb.astype(BF16), wc.astype(BF16), tab


def kernel(x, cond, w_mod, b_mod, ffn1_norm, ffn1_w_gate, ffn1_w_up, ffn1_w_down, mix_norm, w_in, conv_w, lambda_re, lambda_im, log_dt, ssm_b_re, ssm_b_im, ssm_c_re, ssm_c_im, ssm_d, glu_w, glu_b, out_norm_conv, out_norm_ssm, w_out, ffn2_norm, ffn2_w_gate, ffn2_w_up, ffn2_w_down, final_norm):
    depth = w_mod.shape[0]
    for l in range(depth):
        mod = _modulation(cond, w_mod[l], b_mod[l])
        x = _ffn(x, mod, ffn1_norm[l], ffn1_w_gate[l].astype(BF16), ffn1_w_up[l].astype(BF16),
                 ffn1_w_down[l].astype(BF16), final_norm, mod_base=0, final_norm=False)
        toep, wb, wc, tab = _ssm_operators(lambda_re[l], lambda_im[l], log_dt[l], ssm_b_re[l],
                                           ssm_b_im[l], ssm_c_re[l], ssm_c_im[l])
        x = _mixer(x, mod, mix_norm[l], w_in[l].astype(BF16), conv_w[l], toep, wb, wc, tab,
                   ssm_d[l], glu_w[l].astype(BF16), glu_b[l], out_norm_conv[l], out_norm_ssm[l],
                   w_out[l].astype(BF16))
        x = _ffn(x, mod, ffn2_norm[l], ffn2_w_gate[l].astype(BF16), ffn2_w_up[l].astype(BF16),
                 ffn2_w_down[l].astype(BF16), final_norm, mod_base=6,
                 final_norm=(l == depth - 1))
    return x
```

```python
import functools
import math

import jax
import jax.numpy as jnp
from jax import lax
from jax.experimental import pallas as pl
from jax.experimental.pallas import tpu as pltpu

D_MODEL = 1024
D_CONV = 512
D_SSM = 512
CONV_WIDTH = 3
SSM_GROUP_CH = 16
SSM_GROUPS = 32
SSM_STATE = 64
D_FF = 2816
N_MOD = 9
EPS = 1e-6

SUBLANES = 8
LANES = 128
VMEM_LIMIT_BYTES = 56 * 1024 * 1024

FFN_ROWS = 512
FFN_COLS = 256
MIX_ROWS = 512
SSM_CHUNK = 4
GROUPS_PER_SLAB = LANES // SSM_GROUP_CH
N_SLABS = D_SSM // LANES
SLAB_STATE = GROUPS_PER_SLAB * SSM_STATE
SCAN_STEPS = (1, 2, 4)

BF16 = jnp.bfloat16
F32 = jnp.float32


def _dot(a, b):
    return jnp.dot(a, b, preferred_element_type=F32)


def _sigmoid(x):
    return 1.0 / (1.0 + jnp.exp(-x))


def _rms_norm(x, gain):
    ms = jnp.mean(x * x, axis=-1, keepdims=True)
    return x * lax.rsqrt(ms + EPS) * gain


def _gelu_tanh(x):
    c = math.sqrt(2.0 / math.pi)
    return x * (0.5 * (1.0 + jnp.tanh(c * (x + 0.044715 * (x * x * x)))))


def _mod_kernel(c_ref, w_ref, b_ref, o_ref):
    c = c_ref[...]
    c = c * _sigmoid(c)
    o_ref[...] = _dot(c.astype(BF16), w_ref[...].astype(BF16)) + b_ref[...]


def _modulation(cond, w_mod, b_mod):
    bsz = cond.shape[0]
    rows = -(-bsz // SUBLANES) * SUBLANES
    cond_p = jnp.pad(cond, ((0, rows - bsz), (0, 0)))
    n = w_mod.shape[1]
    tn = D_MODEL
    out = pl.pallas_call(
        _mod_kernel,
        grid=(n // tn,),
        in_specs=[
            pl.BlockSpec((rows, D_MODEL), lambda j: (0, 0)),
            pl.BlockSpec((D_MODEL, tn), lambda j: (0, j)),
            pl.BlockSpec((1, tn), lambda j: (0, j)),
        ],
        out_specs=pl.BlockSpec((rows, tn), lambda j: (0, j)),
        out_shape=jax.ShapeDtypeStruct((rows, n), F32),
        compiler_params=pltpu.CompilerParams(
            dimension_semantics=("arbitrary",), vmem_limit_bytes=VMEM_LIMIT_BYTES),
        name="adaln_mod",
    )(cond_p, w_mod, b_mod.reshape(1, n))
    return out[:bsz].reshape(bsz, N_MOD, D_MODEL)


def _ffn_kernel(x_ref, mod_ref, gain_ref, wg_ref, wu_ref, wd_ref, fgain_ref, o_ref,
                act_ref, *, mod_base, final_norm):
    x = x_ref[...]
    shift = mod_ref[mod_base:mod_base + 1, :]
    scale = mod_ref[mod_base + 1:mod_base + 2, :]
    gate = mod_ref[mod_base + 2:mod_base + 3, :]
    h = (_rms_norm(x, gain_ref[...]) * (1.0 + scale) + shift).astype(BF16)
    for f in range(D_FF // FFN_COLS):
        cols = slice(f * FFN_COLS, (f + 1) * FFN_COLS)
        g = _dot(h, wg_ref[:, cols])
        u = _dot(h, wu_ref[:, cols])
        act_ref[:, cols] = (g * _sigmoid(g) * u).astype(BF16)
    y = _dot(act_ref[...], wd_ref[...])
    out = x + (0.5 * gate) * y
    if final_norm:
        out = _rms_norm(out, fgain_ref[...])
    o_ref[...] = out


def _const_spec(shape):
    zeros = (0,) * len(shape)
    return pl.BlockSpec(shape, lambda b, t: zeros, pipeline_mode=pl.Buffered(1))


def _ffn(x, mod, gain, w_gate, w_up, w_down, final_gain, *, mod_base, final_norm):
    bsz, seq, _ = x.shape
    tm = FFN_ROWS
    kernel = functools.partial(_ffn_kernel, mod_base=mod_base, final_norm=final_norm)
    return pl.pallas_call(
        kernel,
        grid=(bsz, seq // tm),
        in_specs=[
            pl.BlockSpec((None, tm, D_MODEL), lambda b, t: (b, t, 0)),
            pl.BlockSpec((None, N_MOD, D_MODEL), lambda b, t: (b, 0, 0)),
            _const_spec((1, D_MODEL)),
            _const_spec((D_MODEL, D_FF)),
            _const_spec((D_MODEL, D_FF)),
            _const_spec((D_FF, D_MODEL)),
            _const_spec((1, D_MODEL)),
        ],
        out_specs=pl.BlockSpec((None, tm, D_MODEL), lambda b, t: (b, t, 0)),
        out_shape=jax.ShapeDtypeStruct(x.shape, F32),
        scratch_shapes=[pltpu.VMEM((tm, D_FF), BF16)],
        compiler_params=pltpu.CompilerParams(
            dimension_semantics=("arbitrary", "arbitrary"),
            vmem_limit_bytes=VMEM_LIMIT_BYTES),
        name="ffn_final" if final_norm else "ffn",
    )(x, mod, gain.reshape(1, D_MODEL), w_gate, w_up, w_down,
      final_gain.reshape(1, D_MODEL))


def _mixer_kernel(x_ref, mod_ref, gain_ref, win_ref, convw_ref, toep_ref, wb_ref, wc_ref,
                  tab_ref, dskip_ref, gluw_ref, glub_ref, gain_c_ref, gain_s_ref, wout_ref,
                  o_ref, zbuf, ubuf, ybuf, hbuf, carry, ymix):
    tm = MIX_ROWS
    chunk = SSM_CHUNK
    m_rows = tm // chunk
    halo = SUBLANES

    @pl.when(pl.program_id(1) == 0)
    def _():
        zbuf[0:halo, :] = jnp.zeros((halo, D_CONV), F32)
        ubuf[:, 0:halo, :] = jnp.zeros((N_SLABS, halo, LANES), F32)
        carry[...] = jnp.zeros(carry.shape, F32)

    x = x_ref[...]
    shift = mod_ref[3:4, :]
    scale = mod_ref[4:5, :]
    gate = mod_ref[5:6, :]
    h = (_rms_norm(x, gain_ref[...]) * (1.0 + scale) + shift).astype(BF16)

    b_gate = _dot(h, win_ref[:, 0:D_CONV])
    c_gate = _dot(h, win_ref[:, D_CONV:2 * D_CONV])
    v = _dot(h, win_ref[:, 2 * D_CONV:3 * D_CONV])
    z = c_gate * v
    zbuf[halo:halo + tm, :] = z
    z1 = zbuf[halo - 1:halo - 1 + tm, :]
    z2 = zbuf[halo - 2:halo - 2 + tm, :]
    y_conv = b_gate * (convw_ref[0:1, :] * z2 + convw_ref[1:2, :] * z1 + convw_ref[2:3, :] * z)
    zbuf[0:halo, :] = zbuf[tm:tm + halo, :]
    ymix[:, 0:D_CONV] = _rms_norm(y_conv, gain_c_ref[...]).astype(BF16)

    u = _dot(h, win_ref[:, 3 * D_CONV:])
    for q in range(N_SLABS):
        ubuf[q, halo:halo + tm, :] = u[:, q * LANES:(q + 1) * LANES]
    for q in range(N_SLABS):
        cur = [ubuf[q, pl.ds(halo + s, m_rows, stride=chunk), :] for s in range(chunk)]
        prv = [ubuf[q, pl.ds(halo - chunk + s, m_rows, stride=chunk), :] for s in range(chunk)]
        u_cur = jnp.concatenate(cur, axis=1).astype(BF16)
        u_prv = jnp.concatenate(prv, axis=1).astype(BF16)
        ubuf[q, 0:halo, :] = ubuf[q, tm:tm + halo, :]
        inj = _dot(u_prv, wb_ref[q])
        for i in range(SLAB_STATE // LANES):
            lanes_re = slice(i * LANES, (i + 1) * LANES)
            lanes_im = slice(SLAB_STATE + i * LANES, SLAB_STATE + (i + 1) * LANES)
            re = inj[:, lanes_re].reshape(m_rows // SUBLANES, SUBLANES, LANES)
            im = inj[:, lanes_im].reshape(m_rows // SUBLANES, SUBLANES, LANES)
            for k, step in enumerate(SCAN_STEPS):
                a_re = tab_ref[q, 2 * k, :, lanes_re]
                a_im = tab_ref[q, 2 * k + 1, :, lanes_re]
                s_re = pltpu.roll(re, step, axis=1)
                s_im = pltpu.roll(im, step, axis=1)
                re, im = re + a_re * s_re - a_im * s_im, im + a_re * s_im + a_im * s_re
            p_re = tab_ref[q, 2 * len(SCAN_STEPS), :, lanes_re]
            p_im = tab_ref[q, 2 * len(SCAN_STEPS) + 1, :, lanes_re]
            c_re = carry[q:q + 1, lanes_re]
            c_im = carry[q:q + 1, lanes_im]
            for r in range(m_rows // SUBLANES):
                rows = slice(r * SUBLANES, (r + 1) * SUBLANES)
                b_re = jnp.broadcast_to(c_re, (SUBLANES, LANES))
                b_im = jnp.broadcast_to(c_im, (SUBLANES, LANES))
                f_re = re[r] + p_re * b_re - p_im * b_im
                f_im = im[r] + p_re * b_im + p_im * b_re
                hbuf[rows, lanes_re] = f_re
                hbuf[rows, lanes_im] = f_im
                c_re = f_re[SUBLANES - 1:SUBLANES, :]
                c_im = f_im[SUBLANES - 1:SUBLANES, :]
            carry[q:q + 1, lanes_re] = c_re
            carry[q:q + 1, lanes_im] = c_im
        y2 = _dot(u_cur, toep_ref[q]) + _dot(hbuf[...].astype(BF16), wc_ref[q])
        for s in range(chunk):
            ybuf[q, pl.ds(s, m_rows, stride=chunk), :] = y2[:, s * LANES:(s + 1) * LANES]
    y = jnp.concatenate([ybuf[q] for q in range(N_SLABS)], axis=1) + dskip_ref[...] * u
    y = _gelu_tanh(y)
    y = y * _sigmoid(_dot(y.astype(BF16), gluw_ref[...]) + glub_ref[...])
    ymix[:, D_CONV:] = _rms_norm(y, gain_s_ref[...]).astype(BF16)

    o_ref[...] = x + gate * _dot(ymix[...], wout_ref[...])


def _mixer(x, mod, gain, w_in, conv_w, toep, wb, wc, tab, d_skip, glu_w, glu_b,
           gain_c, gain_s, w_out):
    bsz, seq, _ = x.shape
    tm = MIX_ROWS
    m_rows = tm // SSM_CHUNK
    return pl.pallas_call(
        _mixer_kernel,
        grid=(bsz, seq // tm),
        in_specs=[
            pl.BlockSpec((None, tm, D_MODEL), lambda b, t: (b, t, 0)),
            pl.BlockSpec((None, N_MOD, D_MODEL), lambda b, t: (b, 0, 0)),
            _const_spec((1, D_MODEL)),
            _const_spec(w_in.shape),
            _const_spec(conv_w.shape),
            _const_spec(toep.shape),
            _const_spec(wb.shape),
            _const_spec(wc.shape),
            _const_spec(tab.shape),
            _const_spec((1, D_SSM)),
            _const_spec(glu_w.shape),
            _const_spec((1, D_SSM)),
            _const_spec((1, D_CONV)),
            _const_spec((1, D_SSM)),
            _const_spec(w_out.shape),
        ],
        out_specs=pl.BlockSpec((None, tm, D_MODEL), lambda b, t: (b, t, 0)),
        out_shape=jax.ShapeDtypeStruct(x.shape, F32),
        scratch_shapes=[
            pltpu.VMEM((SUBLANES + tm, D_CONV), F32),
            pltpu.VMEM((N_SLABS, SUBLANES + tm, LANES), F32),
            pltpu.VMEM((N_SLABS, tm, LANES), F32),
            pltpu.VMEM((m_rows, 2 * SLAB_STATE), F32),
            pltpu.VMEM((N_SLABS, 2 * SLAB_STATE), F32),
            pltpu.VMEM((tm, D_MODEL), BF16),
        ],
        compiler_params=pltpu.CompilerParams(
            dimension_semantics=("arbitrary", "arbitrary"),
            vmem_limit_bytes=VMEM_LIMIT_BYTES),
        name="mixer",
    )(x, mod, gain.reshape(1, D_MODEL), w_in, conv_w, toep, wb, wc, tab,
      d_skip.reshape(1, D_SSM), glu_w, glu_b.reshape(1, D_SSM),
      gain_c.reshape(1, D_CONV), gain_s.reshape(1, D_SSM), w_out)


def _cmul(ar, ai, br, bi):
    return ar * br - ai * bi, ar * bi + ai * br


def _ssm_ops_kernel(bt_ref, ct_ref, lam_ref, inj_ref, toep_ref, wb_ref, wc_ref):
    L = SSM_CHUNK
    hi = lax.Precision.HIGHEST
    shape = (LANES, SLAB_STATE)
    row_group = lax.broadcasted_iota(jnp.int32, shape, 0) >> (SSM_GROUP_CH.bit_length() - 1)
    col_group = lax.broadcasted_iota(jnp.int32, shape, 1) >> (SSM_STATE.bit_length() - 1)
    same_group = row_group == col_group
    tile_p = (lax.broadcasted_iota(jnp.int32, (SSM_STATE, SLAB_STATE), 0)
              == (lax.broadcasted_iota(jnp.int32, (SSM_STATE, SLAB_STATE), 1) & (SSM_STATE - 1))
              ).astype(F32)

    def block_diag(a):
        tiled = jnp.dot(a, tile_p, precision=hi, preferred_element_type=F32)
        return jnp.where(same_group, tiled, 0.0)

    bt_re, bt_im = block_diag(bt_ref[0]), block_diag(bt_ref[1])
    ct_re, ct_im = block_diag(ct_ref[0]), block_diag(ct_ref[1])

    def dot_nt(a, b):
        return lax.dot_general(a, b, (((1,), (1,)), ((), ())), precision=hi,
                               preferred_element_type=F32)

    taps = []
    for m in range(L):
        s = L - 1 - m
        w_re, w_im = _cmul(bt_re, bt_im, inj_ref[0, m:m + 1, :], inj_ref[1, m:m + 1, :])
        wb_ref[s * LANES:(s + 1) * LANES, 0:SLAB_STATE] = w_re.astype(BF16)
        wb_ref[s * LANES:(s + 1) * LANES, SLAB_STATE:] = w_im.astype(BF16)
        taps.append(dot_nt(w_re, ct_re) - dot_nt(w_im, ct_im))
    for s in range(L):
        for t in range(L):
            blk = taps[t - s] if t >= s else jnp.zeros((LANES, LANES), F32)
            toep_ref[s * LANES:(s + 1) * LANES, t * LANES:(t + 1) * LANES] = blk.astype(BF16)
    for t in range(L):
        o_re, o_im = _cmul(ct_re, ct_im, lam_ref[0, t:t + 1, :], lam_ref[1, t:t + 1, :])
        wc_ref[0:SLAB_STATE, t * LANES:(t + 1) * LANES] = o_re.T.astype(BF16)
        wc_ref[SLAB_STATE:, t * LANES:(t + 1) * LANES] = (-o_im).T.astype(BF16)


def _ssm_operators(lambda_re, lambda_im, log_dt, b_re, b_im, c_re, c_im):
    L = SSM_CHUNK
    lre, lim = lambda_re.astype(F32), lambda_im.astype(F32)
    dt = jnp.exp(log_dt.astype(F32))[:, None]
    mag = jnp.exp(lre * dt)
    bar_re, bar_im = mag * jnp.cos(lim * dt), mag * jnp.sin(lim * dt)
    den = lre * lre + lim * lim
    coef_re = ((bar_re - 1.0) * lre + bar_im * lim) / den
    coef_im = (bar_im * lre - (bar_re - 1.0) * lim) / den
    n_pow = SUBLANES * L
    pows = [(jnp.ones_like(bar_re), jnp.zeros_like(bar_im))]
    for _ in range(n_pow):
        pows.append(_cmul(pows[-1][0], pows[-1][1], bar_re, bar_im))

    def slab(a):
        return a.reshape(N_SLABS, SLAB_STATE)

    def rows(vals):
        pad = [(jnp.zeros_like(bar_re),) * 2] * (SUBLANES - len(vals))
        vals = list(vals) + pad
        return jnp.stack([jnp.stack([slab(v[r]) for v in vals], axis=1) for r in range(2)], axis=1)

    lam_rows = rows([pows[m + 1] for m in range(L)])
    inj_rows = rows([_cmul(coef_re, coef_im, *pows[m]) for m in range(L)])

    def compact(re, im):
        return jnp.stack([re, im], axis=1).astype(F32).reshape(
            N_SLABS, GROUPS_PER_SLAB, 2, SSM_GROUP_CH, SSM_STATE).transpose(0, 2, 1, 3, 4).reshape(
            N_SLABS, 2, LANES, SSM_STATE)

    bt = compact(b_re.transpose(0, 2, 1), b_im.transpose(0, 2, 1))
    ct = compact(c_re, c_im)

    toep, wb, wc = pl.pallas_call(
        _ssm_ops_kernel,
        grid=(N_SLABS,),
        in_specs=[
            pl.BlockSpec((None, 2, LANES, SSM_STATE), lambda q: (q, 0, 0, 0)),
            pl.BlockSpec((None, 2, LANES, SSM_STATE), lambda q: (q, 0, 0, 0)),
            pl.BlockSpec((None, 2, SUBLANES, SLAB_STATE), lambda q: (q, 0, 0, 0)),
            pl.BlockSpec((None, 2, SUBLANES, SLAB_STATE), lambda q: (q, 0, 0, 0)),
        ],
        out_specs=[
            pl.BlockSpec((None, L * LANES, L * LANES), lambda q: (q, 0, 0)),
            pl.BlockSpec((None, L * LANES, 2 * SLAB_STATE), lambda q: (q, 0, 0)),
            pl.BlockSpec((None, 2 * SLAB_STATE, L * LANES), lambda q: (q, 0, 0)),
        ],
        out_shape=[
            jax.ShapeDtypeStruct((N_SLABS, L * LANES, L * LANES), BF16),
            jax.ShapeDtypeStruct((N_SLABS, L * LANES, 2 * SLAB_STATE), BF16),
            jax.ShapeDtypeStruct((N_SLABS, 2 * SLAB_STATE, L * LANES), BF16),
        ],
        compiler_params=pltpu.CompilerParams(
            dimension_semantics=("arbitrary",), vmem_limit_bytes=VMEM_LIMIT_BYTES),
        name="ssm_ops",
    )(bt, ct, lam_rows, inj_rows)

    row = jnp.arange(SUBLANES)
    planes = []
    for step in SCAN_STEPS:
        keep = (row >= step)[None, :, None]
        planes.append(jnp.where(keep, slab(pows[L * step][0])[:, None, :], 0.0))
        planes.append(jnp.where(keep, slab(pows[L * step][1])[:, None, :], 0.0))
    planes.append(jnp.stack([slab(pows[L * (t + 1)][0]) for t in range(SUBLANES)], axis=1))
    planes.append(jnp.stack([slab(pows[L * (t + 1)][1]) for t in range(SUBLANES)], axis=1))
    tab = jnp.stack(planes, axis=1)
    return toep, wb, wc, tab


def kernel(x, cond, w_mod, b_mod, ffn1_norm, ffn1_w_gate, ffn1_w_up, ffn1_w_down, mix_norm, w_in, conv_w, lambda_re, lambda_im, log_dt, ssm_b_re, ssm_b_im, ssm_c_re, ssm_c_im, ssm_d, glu_w, glu_b, out_norm_conv, out_norm_ssm, w_out, ffn2_norm, ffn2_w_gate, ffn2_w_up, ffn2_w_down, final_norm):
    depth = w_mod.shape[0]
    for l in range(depth):
        mod = _modulation(cond, w_mod[l], b_mod[l])
        x = _ffn(x, mod, ffn1_norm[l], ffn1_w_gate[l].astype(BF16), ffn1_w_up[l].astype(BF16),
                 ffn1_w_down[l].astype(BF16), final_norm, mod_base=0, final_norm=False)
        toep, wb, wc, tab = _ssm_operators(lambda_re[l], lambda_im[l], log_dt[l], ssm_b_re[l],
                                           ssm_b_im[l], ssm_c_re[l], ssm_c_im[l])
        x = _mixer(x, mod, mix_norm[l], w_in[l].astype(BF16), conv_w[l], toep, wb, wc, tab,
                   ssm_d[l], glu_w[l].astype(BF16), glu_b[l], out_norm_conv[l], out_norm_ssm[l],
                   w_out[l].astype(BF16))
        x = _ffn(x, mod, ffn2_norm[l], ffn2_w_gate[l].astype(BF16), ffn2_w_up[l].astype(BF16),
                 ffn2_w_down[l].astype(BF16), final_norm, mod_base=6,
                 final_norm=(l == depth - 1))
    return x
```

```python
import functools
import math

import jax
import jax.numpy as jnp
from jax import lax
from jax.experimental import pallas as pl
from jax.experimental.pallas import tpu as pltpu

D_MODEL = 1024
D_CONV = 512
D_SSM = 512
CONV_WIDTH = 3
SSM_GROUP_CH = 16
SSM_GROUPS = 32
SSM_STATE = 64
D_FF = 2816
N_MOD = 9
EPS = 1e-6

SUBLANES = 8
LANES = 128
VMEM_LIMIT_BYTES = 56 * 1024 * 1024

FFN_ROWS = 1024
FFN_SUB = 256
FFN_COLS = 256
MIX_ROWS = 512
SSM_CHUNK = 4
GROUPS_PER_SLAB = LANES // SSM_GROUP_CH
N_SLABS = D_SSM // LANES
SLAB_STATE = GROUPS_PER_SLAB * SSM_STATE
SCAN_STEPS = (1, 2, 4)

BF16 = jnp.bfloat16
F32 = jnp.float32


def _dot(a, b):
    return jnp.dot(a, b, preferred_element_type=F32)


def _sigmoid(x):
    return 1.0 / (1.0 + jnp.exp(-x))


def _rms_norm(x, gain):
    ms = jnp.mean(x * x, axis=-1, keepdims=True)
    return x * lax.rsqrt(ms + EPS) * gain


def _gelu_tanh(x):
    c = math.sqrt(2.0 / math.pi)
    return x * (0.5 * (1.0 + jnp.tanh(c * (x + 0.044715 * (x * x * x)))))


def _mod_kernel(c_ref, w_ref, b_ref, o_ref):
    c = c_ref[...]
    c = c * _sigmoid(c)
    o_ref[...] = _dot(c.astype(BF16), w_ref[...].astype(BF16)) + b_ref[...]


def _modulation(cond, w_mod, b_mod):
    bsz = cond.shape[0]
    rows = -(-bsz // SUBLANES) * SUBLANES
    cond_p = jnp.pad(cond, ((0, rows - bsz), (0, 0)))
    n = w_mod.shape[1]
    tn = D_MODEL
    out = pl.pallas_call(
        _mod_kernel,
        grid=(n // tn,),
        in_specs=[
            pl.BlockSpec((rows, D_MODEL), lambda j: (0, 0)),
            pl.BlockSpec((D_MODEL, tn), lambda j: (0, j)),
            pl.BlockSpec((1, tn), lambda j: (0, j)),
        ],
        out_specs=pl.BlockSpec((rows, tn), lambda j: (0, j)),
        out_shape=jax.ShapeDtypeStruct((rows, n), F32),
        compiler_params=pltpu.CompilerParams(
            dimension_semantics=("arbitrary",), vmem_limit_bytes=VMEM_LIMIT_BYTES),
        name="adaln_mod",
    )(cond_p, w_mod, b_mod.reshape(1, n))
    return out[:bsz].reshape(bsz, N_MOD, D_MODEL)


def _ffn_kernel(*refs, mod_base, final_norm, n_cast):
    x_ref, mod_ref, gain_ref, wg_ref, wu_ref, wd_ref, fgain_ref = refs[:7]
    cast_in = refs[7:7 + n_cast]
    o_ref = refs[7 + n_cast]
    cast_out = refs[8 + n_cast:8 + 2 * n_cast]
    act_ref = refs[8 + 2 * n_cast]
    for src, dst in zip(cast_in, cast_out):
        dst[...] = src[...].astype(BF16)
    shift = mod_ref[mod_base:mod_base + 1, :]
    scale = mod_ref[mod_base + 1:mod_base + 2, :]
    gate = mod_ref[mod_base + 2:mod_base + 3, :]
    for r in range(FFN_ROWS // FFN_SUB):
        rows = slice(r * FFN_SUB, (r + 1) * FFN_SUB)
        x = x_ref[rows, :]
        h = (_rms_norm(x, gain_ref[...]) * (1.0 + scale) + shift).astype(BF16)
        for f in range(D_FF // FFN_COLS):
            cols = slice(f * FFN_COLS, (f + 1) * FFN_COLS)
            g = _dot(h, wg_ref[:, cols])
            u = _dot(h, wu_ref[:, cols])
            act_ref[rows, cols] = (g * _sigmoid(g) * u).astype(BF16)
        y = _dot(act_ref[rows, :], wd_ref[...])
        out = x + (0.5 * gate) * y
        if final_norm:
            out = _rms_norm(out, fgain_ref[...])
        o_ref[rows, :] = out


def _const_spec(shape):
    zeros = (0,) * len(shape)
    return pl.BlockSpec(shape, lambda b, t: zeros, pipeline_mode=pl.Buffered(1))


def _ffn(x, mod, gain, w_gate, w_up, w_down, final_gain, *, mod_base, final_norm, cast=()):
    bsz, seq, _ = x.shape
    tm = FFN_ROWS
    n_t = seq // tm
    steps = bsz * n_t
    kernel = functools.partial(_ffn_kernel, mod_base=mod_base, final_norm=final_norm,
                               n_cast=len(cast))
    band_specs = [pl.BlockSpec((w.shape[0] // steps, w.shape[1]), lambda b, t: (b * n_t + t, 0))
                  for w in cast]
    outs = pl.pallas_call(
        kernel,
        grid=(bsz, n_t),
        in_specs=[
            pl.BlockSpec((None, tm, D_MODEL), lambda b, t: (b, t, 0)),
            pl.BlockSpec((None, N_MOD, D_MODEL), lambda b, t: (b, 0, 0)),
            _const_spec((1, D_MODEL)),
            _const_spec((D_MODEL, D_FF)),
            _const_spec((D_MODEL, D_FF)),
            _const_spec((D_FF, D_MODEL)),
            _const_spec((1, D_MODEL)),
        ] + band_specs,
        out_specs=[pl.BlockSpec((None, tm, D_MODEL), lambda b, t: (b, t, 0))] + band_specs,
        out_shape=[jax.ShapeDtypeStruct(x.shape, F32)]
                  + [jax.ShapeDtypeStruct(w.shape, BF16) for w in cast],
        scratch_shapes=[pltpu.VMEM((tm, D_FF), BF16)],
        compiler_params=pltpu.CompilerParams(
            dimension_semantics=("arbitrary", "arbitrary"),
            vmem_limit_bytes=VMEM_LIMIT_BYTES),
        name="ffn_final" if final_norm else "ffn",
    )(x, mod, gain.reshape(1, D_MODEL), w_gate, w_up, w_down,
      final_gain.reshape(1, D_MODEL), *cast)
    return outs[0], tuple(outs[1:])


def _mixer_kernel(x_ref, mod_ref, gain_ref, win_ref, convw_ref, toep_ref, wb_ref, wc_ref,
                  tab_ref, dskip_ref, gluw_ref, glub_ref, gain_c_ref, gain_s_ref, wout_ref,
                  o_ref, zbuf, ubuf, ybuf, hbuf, carry, ymix):
    tm = MIX_ROWS
    chunk = SSM_CHUNK
    m_rows = tm // chunk
    halo = SUBLANES

    @pl.when(pl.program_id(1) == 0)
    def _():
        zbuf[0:halo, :] = jnp.zeros((halo, D_CONV), F32)
        ubuf[:, 0:halo, :] = jnp.zeros((N_SLABS, halo, LANES), F32)
        carry[...] = jnp.zeros(carry.shape, F32)

    x = x_ref[...]
    shift = mod_ref[3:4, :]
    scale = mod_ref[4:5, :]
    gate = mod_ref[5:6, :]
    h = (_rms_norm(x, gain_ref[...]) * (1.0 + scale) + shift).astype(BF16)

    b_gate = _dot(h, win_ref[:, 0:D_CONV])
    c_gate = _dot(h, win_ref[:, D_CONV:2 * D_CONV])
    v = _dot(h, win_ref[:, 2 * D_CONV:3 * D_CONV])
    z = c_gate * v
    zbuf[halo:halo + tm, :] = z
    z1 = zbuf[halo - 1:halo - 1 + tm, :]
    z2 = zbuf[halo - 2:halo - 2 + tm, :]
    y_conv = b_gate * (convw_ref[0:1, :] * z2 + convw_ref[1:2, :] * z1 + convw_ref[2:3, :] * z)
    zbuf[0:halo, :] = zbuf[tm:tm + halo, :]
    ymix[:, 0:D_CONV] = _rms_norm(y_conv, gain_c_ref[...]).astype(BF16)

    u = _dot(h, win_ref[:, 3 * D_CONV:])
    for q in range(N_SLABS):
        ubuf[q, halo:halo + tm, :] = u[:, q * LANES:(q + 1) * LANES]
    for q in range(N_SLABS):
        cur = [ubuf[q, pl.ds(halo + s, m_rows, stride=chunk), :] for s in range(chunk)]
        prv = [ubuf[q, pl.ds(halo - chunk + s, m_rows, stride=chunk), :] for s in range(chunk)]
        u_cur = jnp.concatenate(cur, axis=1).astype(BF16)
        u_prv = jnp.concatenate(prv, axis=1).astype(BF16)
        ubuf[q, 0:halo, :] = ubuf[q, tm:tm + halo, :]
        inj = _dot(u_prv, wb_ref[q])
        for i in range(SLAB_STATE // LANES):
            lanes_re = slice(i * LANES, (i + 1) * LANES)
            lanes_im = slice(SLAB_STATE + i * LANES, SLAB_STATE + (i + 1) * LANES)
            re = inj[:, lanes_re].reshape(m_rows // SUBLANES, SUBLANES, LANES)
            im = inj[:, lanes_im].reshape(m_rows // SUBLANES, SUBLANES, LANES)
            for k, step in enumerate(SCAN_STEPS):
                a_re = tab_ref[q, 2 * k, :, lanes_re]
                a_im = tab_ref[q, 2 * k + 1, :, lanes_re]
                s_re = pltpu.roll(re, step, axis=1)
                s_im = pltpu.roll(im, step, axis=1)
                re, im = re + a_re * s_re - a_im * s_im, im + a_re * s_im + a_im * s_re
            p_re = tab_ref[q, 2 * len(SCAN_STEPS), :, lanes_re]
            p_im = tab_ref[q, 2 * len(SCAN_STEPS) + 1, :, lanes_re]
            c_re = carry[q:q + 1, lanes_re]
            c_im = carry[q:q + 1, lanes_im]
            for r in range(m_rows // SUBLANES):
                rows = slice(r * SUBLANES, (r + 1) * SUBLANES)
                b_re = jnp.broadcast_to(c_re, (SUBLANES, LANES))
                b_im = jnp.broadcast_to(c_im, (SUBLANES, LANES))
                f_re = re[r] + p_re * b_re - p_im * b_im
                f_im = im[r] + p_re * b_im + p_im * b_re
                hbuf[rows, lanes_re] = f_re
                hbuf[rows, lanes_im] = f_im
                c_re = f_re[SUBLANES - 1:SUBLANES, :]
                c_im = f_im[SUBLANES - 1:SUBLANES, :]
            carry[q:q + 1, lanes_re] = c_re
            carry[q:q + 1, lanes_im] = c_im
        y2 = _dot(u_cur, toep_ref[q]) + _dot(hbuf[...].astype(BF16), wc_ref[q])
        for s in range(chunk):
            ybuf[q, pl.ds(s, m_rows, stride=chunk), :] = y2[:, s * LANES:(s + 1) * LANES]
    y = jnp.concatenate([ybuf[q] for q in range(N_SLABS)], axis=1) + dskip_ref[...] * u
    y = _gelu_tanh(y)
    y = y * _sigmoid(_dot(y.astype(BF16), gluw_ref[...]) + glub_ref[...])
    ymix[:, D_CONV:] = _rms_norm(y, gain_s_ref[...]).astype(BF16)

    o_ref[...] = x + gate * _dot(ymix[...], wout_ref[...])


def _mixer(x, mod, gain, w_in, conv_w, toep, wb, wc, tab, d_skip, glu_w, glu_b,
           gain_c, gain_s, w_out):
    bsz, seq, _ = x.shape
    tm = MIX_ROWS
    m_rows = tm // SSM_CHUNK
    return pl.pallas_call(
        _mixer_kernel,
        grid=(bsz, seq // tm),
        in_specs=[
            pl.BlockSpec((None, tm, D_MODEL), lambda b, t: (b, t, 0)),
            pl.BlockSpec((None, N_MOD, D_MODEL), lambda b, t: (b, 0, 0)),
            _const_spec((1, D_MODEL)),
            _const_spec(w_in.shape),
            _const_spec(conv_w.shape),
            _const_spec(toep.shape),
            _const_spec(wb.shape),
            _const_spec(wc.shape),
            _const_spec(tab.shape),
            _const_spec((1, D_SSM)),
            _const_spec(glu_w.shape),
            _const_spec((1, D_SSM)),
            _const_spec((1, D_CONV)),
            _const_spec((1, D_SSM)),
            _const_spec(w_out.shape),
        ],
        out_specs=pl.BlockSpec((None, tm, D_MODEL), lambda b, t: (b, t, 0)),
        out_shape=jax.ShapeDtypeStruct(x.shape, F32),
        scratch_shapes=[
            pltpu.VMEM((SUBLANES + tm, D_CONV), F32),
            pltpu.VMEM((N_SLABS, SUBLANES + tm, LANES), F32),
            pltpu.VMEM((N_SLABS, tm, LANES), F32),
            pltpu.VMEM((m_rows, 2 * SLAB_STATE), F32),
            pltpu.VMEM((N_SLABS, 2 * SLAB_STATE), F32),
            pltpu.VMEM((tm, D_MODEL), BF16),
        ],
        compiler_params=pltpu.CompilerParams(
            dimension_semantics=("arbitrary", "arbitrary"),
            vmem_limit_bytes=VMEM_LIMIT_BYTES),
        name="mixer",
    )(x, mod, gain.reshape(1, D_MODEL), w_in, conv_w, toep, wb, wc, tab,
      d_skip.reshape(1, D_SSM), glu_w, glu_b.reshape(1, D_SSM),
      gain_c.reshape(1, D_CONV), gain_s.reshape(1, D_SSM), w_out)


def _cmul(ar, ai, br, bi):
    return ar * br - ai * bi, ar * bi + ai * br


def _ssm_ops_kernel(bt_ref, ct_ref, lam_ref, inj_ref, toep_ref, wb_ref, wc_ref):
    L = SSM_CHUNK
    hi = lax.Precision.HIGHEST
    shape = (LANES, SLAB_STATE)
    row_group = lax.broadcasted_iota(jnp.int32, shape, 0) >> (SSM_GROUP_CH.bit_length() - 1)
    col_group = lax.broadcasted_iota(jnp.int32, shape, 1) >> (SSM_STATE.bit_length() - 1)
    same_group = row_group == col_group
    tile_p = (lax.broadcasted_iota(jnp.int32, (SSM_STATE, SLAB_STATE), 0)
              == (lax.broadcasted_iota(jnp.int32, (SSM_STATE, SLAB_STATE), 1) & (SSM_STATE - 1))
              ).astype(F32)

    def block_diag(a):
        tiled = jnp.dot(a, tile_p, precision=hi, preferred_element_type=F32)
        return jnp.where(same_group, tiled, 0.0)

    bt_re, bt_im = block_diag(bt_ref[0]), block_diag(bt_ref[1])
    ct_re, ct_im = block_diag(ct_ref[0]), block_diag(ct_ref[1])

    def dot_nt(a, b):
        return lax.dot_general(a, b, (((1,), (1,)), ((), ())), precision=hi,
                               preferred_element_type=F32)

    taps = []
    for m in range(L):
        s = L - 1 - m
        w_re, w_im = _cmul(bt_re, bt_im, inj_ref[0, m:m + 1, :], inj_ref[1, m:m + 1, :])
        wb_ref[s * LANES:(s + 1) * LANES, 0:SLAB_STATE] = w_re.astype(BF16)
        wb_ref[s * LANES:(s + 1) * LANES, SLAB_STATE:] = w_im.astype(BF16)
        taps.append(dot_nt(w_re, ct_re) - dot_nt(w_im, ct_im))
    for s in range(L):
        for t in range(L):
            blk = taps[t - s] if t >= s else jnp.zeros((LANES, LANES), F32)
            toep_ref[s * LANES:(s + 1) * LANES, t * LANES:(t + 1) * LANES] = blk.astype(BF16)
    for t in range(L):
        o_re, o_im = _cmul(ct_re, ct_im, lam_ref[0, t:t + 1, :], lam_ref[1, t:t + 1, :])
        wc_ref[0:SLAB_STATE, t * LANES:(t + 1) * LANES] = o_re.T.astype(BF16)
        wc_ref[SLAB_STATE:, t * LANES:(t + 1) * LANES] = (-o_im).T.astype(BF16)


def _ssm_operators(lambda_re, lambda_im, log_dt, b_re, b_im, c_re, c_im):
    L = SSM_CHUNK
    lre, lim = lambda_re.astype(F32), lambda_im.astype(F32)
    dt = jnp.exp(log_dt.astype(F32))[:, None]
    mag = jnp.exp(lre * dt)
    bar_re, bar_im = mag * jnp.cos(lim * dt), mag * jnp.sin(lim * dt)
    den = lre * lre + lim * lim
    coef_re = ((bar_re - 1.0) * lre + bar_im * lim) / den
    coef_im = (bar_im * lre - (bar_re - 1.0) * lim) / den
    n_pow = SUBLANES * L
    pows = [(jnp.ones_like(bar_re), jnp.zeros_like(bar_im))]
    for _ in range(n_pow):
        pows.append(_cmul(pows[-1][0], pows[-1][1], bar_re, bar_im))

    def slab(a):
        return a.reshape(N_SLABS, SLAB_STATE)

    def rows(vals):
        pad = [(jnp.zeros_like(bar_re),) * 2] * (SUBLANES - len(vals))
        vals = list(vals) + pad
        return jnp.stack([jnp.stack([slab(v[r]) for v in vals], axis=1) for r in range(2)], axis=1)

    lam_rows = rows([pows[m + 1] for m in range(L)])
    inj_rows = rows([_cmul(coef_re, coef_im, *pows[m]) for m in range(L)])

    def compact(re, im):
        return jnp.stack([re, im], axis=1).astype(F32).reshape(
            N_SLABS, GROUPS_PER_SLAB, 2, SSM_GROUP_CH, SSM_STATE).transpose(0, 2, 1, 3, 4).reshape(
            N_SLABS, 2, LANES, SSM_STATE)

    bt = compact(b_re.transpose(0, 2, 1), b_im.transpose(0, 2, 1))
    ct = compact(c_re, c_im)

    toep, wb, wc = pl.pallas_call(
        _ssm_ops_kernel,
        grid=(N_SLABS,),
        in_specs=[
            pl.BlockSpec((None, 2, LANES, SSM_STATE), lambda q: (q, 0, 0, 0)),
            pl.BlockSpec((None, 2, LANES, SSM_STATE), lambda q: (q, 0, 0, 0)),
            pl.BlockSpec((None, 2, SUBLANES, SLAB_STATE), lambda q: (q, 0, 0, 0)),
            pl.BlockSpec((None, 2, SUBLANES, SLAB_STATE), lambda q: (q, 0, 0, 0)),
        ],
        out_specs=[
            pl.BlockSpec((None, L * LANES, L * LANES), lambda q: (q, 0, 0)),
            pl.BlockSpec((None, L * LANES, 2 * SLAB_STATE), lambda q: (q, 0, 0)),
            pl.BlockSpec((None, 2 * SLAB_STATE, L * LANES), lambda q: (q, 0, 0)),
        ],
        out_shape=[
            jax.ShapeDtypeStruct((N_SLABS, L * LANES, L * LANES), BF16),
            jax.ShapeDtypeStruct((N_SLABS, L * LANES, 2 * SLAB_STATE), BF16),
            jax.ShapeDtypeStruct((N_SLABS, 2 * SLAB_STATE, L * LANES), BF16),
        ],
        compiler_params=pltpu.CompilerParams(
            dimension_semantics=("arbitrary",), vmem_limit_bytes=VMEM_LIMIT_BYTES),
        name="ssm_ops",
    )(bt, ct, lam_rows, inj_rows)

    row = jnp.arange(SUBLANES)
    planes = []
    for step in SCAN_STEPS:
        keep = (row >= step)[None, :, None]
        planes.append(jnp.where(keep, slab(pows[L * step][0])[:, None, :], 0.0))
        planes.append(jnp.where(keep, slab(pows[L * step][1])[:, None, :], 0.0))
    planes.append(jnp.stack([slab(pows[L * (t + 1)][0]) for t in range(SUBLANES)], axis=1))
    planes.append(jnp.stack([slab(pows[L * (t + 1)][1]) for t in range(SUBLANES)], axis=1))
    tab = jnp.stack(planes, axis=1)
    return toep, wb, wc, tab


def kernel(x, cond, w_mod, b_mod, ffn1_norm, ffn1_w_gate, ffn1_w_up, ffn1_w_down, mix_norm, w_in, conv_w, lambda_re, lambda_im, log_dt, ssm_b_re, ssm_b_im, ssm_c_re, ssm_c_im, ssm_d, glu_w, glu_b, out_norm_conv, out_norm_ssm, w_out, ffn2_norm, ffn2_w_gate, ffn2_w_up, ffn2_w_down, final_norm):
    depth = w_mod.shape[0]
    for l in range(depth):
        mod = _modulation(cond, w_mod[l], b_mod[l])
        later = (w_in[l], glu_w[l], w_out[l], ffn2_w_gate[l], ffn2_w_up[l], ffn2_w_down[l])
        x, (w_in_b, glu_w_b, w_out_b, wg2, wu2, wd2) = _ffn(
            x, mod, ffn1_norm[l], ffn1_w_gate[l].astype(BF16), ffn1_w_up[l].astype(BF16),
            ffn1_w_down[l].astype(BF16), final_norm, mod_base=0, final_norm=False, cast=later)
        toep, wb, wc, tab = _ssm_operators(lambda_re[l], lambda_im[l], log_dt[l], ssm_b_re[l],
                                           ssm_b_im[l], ssm_c_re[l], ssm_c_im[l])
        x = _mixer(x, mod, mix_norm[l], w_in_b, conv_w[l], toep, wb, wc, tab,
                   ssm_d[l], glu_w_b, glu_b[l], out_norm_conv[l], out_norm_ssm[l], w_out_b)
        x, _ = _ffn(x, mod, ffn2_norm[l], wg2, wu2, wd2, final_norm, mod_base=6,
                    final_norm=(l == depth - 1))
    return x
```

```python
import functools
import math

import jax
import jax.numpy as jnp
from jax import lax
from jax.experimental import pallas as pl
from jax.experimental.pallas import tpu as pltpu

D_MODEL = 1024
D_CONV = 512
D_SSM = 512
CONV_WIDTH = 3
SSM_GROUP_CH = 16
SSM_GROUPS = 32
SSM_STATE = 64
D_FF = 2816
N_MOD = 9
EPS = 1e-6

SUBLANES = 8
LANES = 128
VMEM_LIMIT_BYTES = 56 * 1024 * 1024

FFN_ROWS = 1024
FFN_SUB = 256
FFN_COLS = 256
MIX_ROWS = 512
MIX_SUB = 256
SSM_CHUNK = 4
GROUPS_PER_SLAB = LANES // SSM_GROUP_CH
N_SLABS = D_SSM // LANES
SLAB_STATE = GROUPS_PER_SLAB * SSM_STATE
SCAN_STEPS = (1, 2, 4)

BF16 = jnp.bfloat16
F32 = jnp.float32


def _dot(a, b):
    return jnp.dot(a, b, preferred_element_type=F32)


def _sigmoid(x):
    return 1.0 / (1.0 + jnp.exp(-x))


def _rms_norm(x, gain):
    ms = jnp.mean(x * x, axis=-1, keepdims=True)
    return x * lax.rsqrt(ms + EPS) * gain


def _gelu_tanh(x):
    c = math.sqrt(2.0 / math.pi)
    return x * (0.5 * (1.0 + jnp.tanh(c * (x + 0.044715 * (x * x * x)))))


def _mod_kernel(c_ref, w_ref, b_ref, o_ref):
    c = c_ref[...]
    c = c * _sigmoid(c)
    o_ref[...] = _dot(c.astype(BF16), w_ref[...].astype(BF16)) + b_ref[...]


def _modulation(cond, w_mod, b_mod):
    bsz = cond.shape[0]
    rows = -(-bsz // SUBLANES) * SUBLANES
    cond_p = jnp.pad(cond, ((0, rows - bsz), (0, 0)))
    n = w_mod.shape[1]
    tn = D_MODEL
    out = pl.pallas_call(
        _mod_kernel,
        grid=(n // tn,),
        in_specs=[
            pl.BlockSpec((rows, D_MODEL), lambda j: (0, 0)),
            pl.BlockSpec((D_MODEL, tn), lambda j: (0, j)),
            pl.BlockSpec((1, tn), lambda j: (0, j)),
        ],
        out_specs=pl.BlockSpec((rows, tn), lambda j: (0, j)),
        out_shape=jax.ShapeDtypeStruct((rows, n), F32),
        compiler_params=pltpu.CompilerParams(
            dimension_semantics=("arbitrary",), vmem_limit_bytes=VMEM_LIMIT_BYTES),
        name="adaln_mod",
    )(cond_p, w_mod, b_mod.reshape(1, n))
    return out[:bsz].reshape(bsz, N_MOD, D_MODEL)


def _ffn_kernel(*refs, mod_base, final_norm, n_cast):
    x_ref, mod_ref, gain_ref, wg_ref, wu_ref, wd_ref, fgain_ref = refs[:7]
    cast_in = refs[7:7 + n_cast]
    o_ref = refs[7 + n_cast]
    cast_out = refs[8 + n_cast:8 + 2 * n_cast]
    act_ref = refs[8 + 2 * n_cast]
    for src, dst in zip(cast_in, cast_out):
        dst[...] = src[...].astype(BF16)
    shift = mod_ref[mod_base:mod_base + 1, :]
    scale = mod_ref[mod_base + 1:mod_base + 2, :]
    gate = mod_ref[mod_base + 2:mod_base + 3, :]
    for r in range(FFN_ROWS // FFN_SUB):
        rows = slice(r * FFN_SUB, (r + 1) * FFN_SUB)
        x = x_ref[rows, :]
        h = (_rms_norm(x, gain_ref[...]) * (1.0 + scale) + shift).astype(BF16)
        for f in range(D_FF // FFN_COLS):
            cols = slice(f * FFN_COLS, (f + 1) * FFN_COLS)
            g = _dot(h, wg_ref[:, cols])
            u = _dot(h, wu_ref[:, cols])
            act_ref[rows, cols] = (g * _sigmoid(g) * u).astype(BF16)
        y = _dot(act_ref[rows, :], wd_ref[...])
        out = x + (0.5 * gate) * y
        if final_norm:
            out = _rms_norm(out, fgain_ref[...])
        o_ref[rows, :] = out


def _const_spec(shape):
    zeros = (0,) * len(shape)
    return pl.BlockSpec(shape, lambda b, t: zeros, pipeline_mode=pl.Buffered(1))


def _ffn(x, mod, gain, w_gate, w_up, w_down, final_gain, *, mod_base, final_norm, cast=()):
    bsz, seq, _ = x.shape
    tm = FFN_ROWS
    n_t = seq // tm
    steps = bsz * n_t
    kernel = functools.partial(_ffn_kernel, mod_base=mod_base, final_norm=final_norm,
                               n_cast=len(cast))
    band_specs = [pl.BlockSpec((w.shape[0] // steps, w.shape[1]), lambda b, t: (b * n_t + t, 0))
                  for w in cast]
    outs = pl.pallas_call(
        kernel,
        grid=(bsz, n_t),
        in_specs=[
            pl.BlockSpec((None, tm, D_MODEL), lambda b, t: (b, t, 0)),
            pl.BlockSpec((None, N_MOD, D_MODEL), lambda b, t: (b, 0, 0)),
            _const_spec((1, D_MODEL)),
            _const_spec((D_MODEL, D_FF)),
            _const_spec((D_MODEL, D_FF)),
            _const_spec((D_FF, D_MODEL)),
            _const_spec((1, D_MODEL)),
        ] + band_specs,
        out_specs=[pl.BlockSpec((None, tm, D_MODEL), lambda b, t: (b, t, 0))] + band_specs,
        out_shape=[jax.ShapeDtypeStruct(x.shape, F32)]
                  + [jax.ShapeDtypeStruct(w.shape, BF16) for w in cast],
        scratch_shapes=[pltpu.VMEM((tm, D_FF), BF16)],
        compiler_params=pltpu.CompilerParams(
            dimension_semantics=("arbitrary", "arbitrary"),
            vmem_limit_bytes=VMEM_LIMIT_BYTES),
        name="ffn_final" if final_norm else "ffn",
    )(x, mod, gain.reshape(1, D_MODEL), w_gate, w_up, w_down,
      final_gain.reshape(1, D_MODEL), *cast)
    return outs[0], tuple(outs[1:])


def _mixer_kernel(x_ref, mod_ref, gain_ref, win_ref, convw_ref, toep_ref, wb_ref, wc_ref,
                  tab_ref, dskip_ref, gluw_ref, glub_ref, gain_c_ref, gain_s_ref, wout_ref,
                  o_ref, zbuf, ubuf, ybuf, hbuf, carry, ymix):
    tm = MIX_ROWS
    chunk = SSM_CHUNK
    m_rows = tm // chunk
    halo = SUBLANES

    @pl.when(pl.program_id(1) == 0)
    def _():
        zbuf[0:halo, :] = jnp.zeros((halo, D_CONV), F32)
        ubuf[:, 0:halo, :] = jnp.zeros((N_SLABS, halo, LANES), F32)
        carry[...] = jnp.zeros(carry.shape, F32)

    x = x_ref[...]
    shift = mod_ref[3:4, :]
    scale = mod_ref[4:5, :]
    gate = mod_ref[5:6, :]
    h = (_rms_norm(x, gain_ref[...]) * (1.0 + scale) + shift).astype(BF16)


    u = _dot(h, win_ref[:, 3 * D_CONV:])
    for q in range(N_SLABS):
        ubuf[q, halo:halo + tm, :] = u[:, q * LANES:(q + 1) * LANES]
    u_curs, injs = [], []
    for q in range(N_SLABS):
        cur = [ubuf[q, pl.ds(halo + s, m_rows, stride=chunk), :] for s in range(chunk)]
        prv = [ubuf[q, pl.ds(halo - chunk + s, m_rows, stride=chunk), :] for s in range(chunk)]
        u_curs.append(jnp.concatenate(cur, axis=1).astype(BF16))
        u_prv = jnp.concatenate(prv, axis=1).astype(BF16)
        ubuf[q, 0:halo, :] = ubuf[q, tm:tm + halo, :]
        injs.append(_dot(u_prv, wb_ref[q]))

    conv_proj = []
    for q in range(N_SLABS):
        if q < 3:
            conv_proj.append(_dot(h, win_ref[:, q * D_CONV:(q + 1) * D_CONV]))
        inj = injs[q]
        for i in range(SLAB_STATE // LANES):
            lanes_re = slice(i * LANES, (i + 1) * LANES)
            lanes_im = slice(SLAB_STATE + i * LANES, SLAB_STATE + (i + 1) * LANES)
            re = inj[:, lanes_re].reshape(m_rows // SUBLANES, SUBLANES, LANES)
            im = inj[:, lanes_im].reshape(m_rows // SUBLANES, SUBLANES, LANES)
            for k, step in enumerate(SCAN_STEPS):
                a_re = tab_ref[q, 2 * k, :, lanes_re]
                a_im = tab_ref[q, 2 * k + 1, :, lanes_re]
                s_re = pltpu.roll(re, step, axis=1)
                s_im = pltpu.roll(im, step, axis=1)
                re, im = re + a_re * s_re - a_im * s_im, im + a_re * s_im + a_im * s_re
            p_re = tab_ref[q, 2 * len(SCAN_STEPS), :, lanes_re]
            p_im = tab_ref[q, 2 * len(SCAN_STEPS) + 1, :, lanes_re]
            c_re = carry[q:q + 1, lanes_re]
            c_im = carry[q:q + 1, lanes_im]
            for r in range(m_rows // SUBLANES):
                rows = slice(r * SUBLANES, (r + 1) * SUBLANES)
                b_re = jnp.broadcast_to(c_re, (SUBLANES, LANES))
                b_im = jnp.broadcast_to(c_im, (SUBLANES, LANES))
                f_re = re[r] + p_re * b_re - p_im * b_im
                f_im = im[r] + p_re * b_im + p_im * b_re
                hbuf[q, rows, lanes_re] = f_re
                hbuf[q, rows, lanes_im] = f_im
                c_re = f_re[SUBLANES - 1:SUBLANES, :]
                c_im = f_im[SUBLANES - 1:SUBLANES, :]
            carry[q:q + 1, lanes_re] = c_re
            carry[q:q + 1, lanes_im] = c_im

    for q in range(N_SLABS):
        y2 = _dot(u_curs[q], toep_ref[q]) + _dot(hbuf[q].astype(BF16), wc_ref[q])
        for s in range(chunk):
            ybuf[q, pl.ds(s, m_rows, stride=chunk), :] = y2[:, s * LANES:(s + 1) * LANES]
    b_gate, c_gate, v = conv_proj
    z = c_gate * v
    zbuf[halo:halo + tm, :] = z
    z1 = zbuf[halo - 1:halo - 1 + tm, :]
    z2 = zbuf[halo - 2:halo - 2 + tm, :]
    y_conv = b_gate * (convw_ref[0:1, :] * z2 + convw_ref[1:2, :] * z1 + convw_ref[2:3, :] * z)
    zbuf[0:halo, :] = zbuf[tm:tm + halo, :]
    ymix[:, 0:D_CONV] = _rms_norm(y_conv, gain_c_ref[...]).astype(BF16)

    for r in range(MIX_ROWS // MIX_SUB):
        rows = slice(r * MIX_SUB, (r + 1) * MIX_SUB)
        u_rows = slice(halo + r * MIX_SUB, halo + (r + 1) * MIX_SUB)
        y = jnp.concatenate([ybuf[q, rows, :] for q in range(N_SLABS)], axis=1)
        y = y + dskip_ref[...] * jnp.concatenate([ubuf[q, u_rows, :] for q in range(N_SLABS)], axis=1)
        y = _gelu_tanh(y)
        y = y * _sigmoid(_dot(y.astype(BF16), gluw_ref[...]) + glub_ref[...])
        ymix[rows, D_CONV:] = _rms_norm(y, gain_s_ref[...]).astype(BF16)
        o_ref[rows, :] = x_ref[rows, :] + gate * _dot(ymix[rows, :], wout_ref[...])


def _mixer(x, mod, gain, w_in, conv_w, toep, wb, wc, tab, d_skip, glu_w, glu_b,
           gain_c, gain_s, w_out):
    bsz, seq, _ = x.shape
    tm = MIX_ROWS
    m_rows = tm // SSM_CHUNK
    return pl.pallas_call(
        _mixer_kernel,
        grid=(bsz, seq // tm),
        in_specs=[
            pl.BlockSpec((None, tm, D_MODEL), lambda b, t: (b, t, 0)),
            pl.BlockSpec((None, N_MOD, D_MODEL), lambda b, t: (b, 0, 0)),
            _const_spec((1, D_MODEL)),
            _const_spec(w_in.shape),
            _const_spec(conv_w.shape),
            _const_spec(toep.shape),
            _const_spec(wb.shape),
            _const_spec(wc.shape),
            _const_spec(tab.shape),
            _const_spec((1, D_SSM)),
            _const_spec(glu_w.shape),
            _const_spec((1, D_SSM)),
            _const_spec((1, D_CONV)),
            _const_spec((1, D_SSM)),
            _const_spec(w_out.shape),
        ],
        out_specs=pl.BlockSpec((None, tm, D_MODEL), lambda b, t: (b, t, 0)),
        out_shape=jax.ShapeDtypeStruct(x.shape, F32),
        scratch_shapes=[
            pltpu.VMEM((SUBLANES + tm, D_CONV), F32),
            pltpu.VMEM((N_SLABS, SUBLANES + tm, LANES), F32),
            pltpu.VMEM((N_SLABS, tm, LANES), F32),
            pltpu.VMEM((N_SLABS, m_rows, 2 * SLAB_STATE), F32),
            pltpu.VMEM((N_SLABS, 2 * SLAB_STATE), F32),
            pltpu.VMEM((tm, D_MODEL), BF16),
        ],
        compiler_params=pltpu.CompilerParams(
            dimension_semantics=("arbitrary", "arbitrary"),
            vmem_limit_bytes=VMEM_LIMIT_BYTES),
        name="mixer",
    )(x, mod, gain.reshape(1, D_MODEL), w_in, conv_w, toep, wb, wc, tab,
      d_skip.reshape(1, D_SSM), glu_w, glu_b.reshape(1, D_SSM),
      gain_c.reshape(1, D_CONV), gain_s.reshape(1, D_SSM), w_out)


def _cmul(ar, ai, br, bi):
    return ar * br - ai * bi, ar * bi + ai * br


def _ssm_ops_kernel(bt_ref, ct_ref, lam_ref, inj_ref, toep_ref, wb_ref, wc_ref):
    L = SSM_CHUNK
    shape = (LANES, SLAB_STATE)
    row_group = lax.broadcasted_iota(jnp.int32, shape, 0) >> (SSM_GROUP_CH.bit_length() - 1)
    col_group = lax.broadcasted_iota(jnp.int32, shape, 1) >> (SSM_STATE.bit_length() - 1)
    same_group = row_group == col_group
    tile_p = (lax.broadcasted_iota(jnp.int32, (SSM_STATE, SLAB_STATE), 0)
              == (lax.broadcasted_iota(jnp.int32, (SSM_STATE, SLAB_STATE), 1) & (SSM_STATE - 1))
              ).astype(BF16)

    def split(a):
        head = a.astype(BF16)
        return head, (a - head.astype(F32)).astype(BF16)

    def block_diag(a):
        head, rest = split(a)
        return jnp.where(same_group, _dot(head, tile_p) + _dot(rest, tile_p), 0.0)

    bt_re, bt_im = block_diag(bt_ref[0]), block_diag(bt_ref[1])
    ct_re, ct_im = block_diag(ct_ref[0]), block_diag(ct_ref[1])
    ct_re_parts, ct_im_parts = split(ct_re), split(ct_im)

    def dot_nt(a, b_parts):
        nt = (((1,), (1,)), ((), ()))
        a_head, a_rest = split(a)
        b_head, b_rest = b_parts
        return (lax.dot_general(a_head, b_head, nt, preferred_element_type=F32)
                + lax.dot_general(a_rest, b_head, nt, preferred_element_type=F32)
                + lax.dot_general(a_head, b_rest, nt, preferred_element_type=F32))

    taps = []
    for m in range(L):
        s = L - 1 - m
        w_re, w_im = _cmul(bt_re, bt_im, inj_ref[0, m:m + 1, :], inj_ref[1, m:m + 1, :])
        wb_ref[s * LANES:(s + 1) * LANES, 0:SLAB_STATE] = w_re.astype(BF16)
        wb_ref[s * LANES:(s + 1) * LANES, SLAB_STATE:] = w_im.astype(BF16)
        taps.append(dot_nt(w_re, ct_re_parts) - dot_nt(w_im, ct_im_parts))
    for s in range(L):
        for t in range(L):
            blk = taps[t - s] if t >= s else jnp.zeros((LANES, LANES), F32)
            toep_ref[s * LANES:(s + 1) * LANES, t * LANES:(t + 1) * LANES] = blk.astype(BF16)
    for t in range(L):
        o_re, o_im = _cmul(ct_re, ct_im, lam_ref[0, t:t + 1, :], lam_ref[1, t:t + 1, :])
        wc_ref[0:SLAB_STATE, t * LANES:(t + 1) * LANES] = o_re.T.astype(BF16)
        wc_ref[SLAB_STATE:, t * LANES:(t + 1) * LANES] = (-o_im).T.astype(BF16)


def _ssm_operators(lambda_re, lambda_im, log_dt, b_re, b_im, c_re, c_im):
    L = SSM_CHUNK
    lre, lim = lambda_re.astype(F32), lambda_im.astype(F32)
    dt = jnp.exp(log_dt.astype(F32))[:, None]
    mag = jnp.exp(lre * dt)
    bar_re, bar_im = mag * jnp.cos(lim * dt), mag * jnp.sin(lim * dt)
    den = lre * lre + lim * lim
    coef_re = ((bar_re - 1.0) * lre + bar_im * lim) / den
    coef_im = (bar_im * lre - (bar_re - 1.0) * lim) / den
    n_pow = SUBLANES * L
    pows = [(jnp.ones_like(bar_re), jnp.zeros_like(bar_im))]
    for _ in range(n_pow):
        pows.append(_cmul(pows[-1][0], pows[-1][1], bar_re, bar_im))

    def slab(a):
        return a.reshape(N_SLABS, SLAB_STATE)

    def rows(vals):
        pad = [(jnp.zeros_like(bar_re),) * 2] * (SUBLANES - len(vals))
        vals = list(vals) + pad
        return jnp.stack([jnp.stack([slab(v[r]) for v in vals], axis=1) for r in range(2)], axis=1)

    lam_rows = rows([pows[m + 1] for m in range(L)])
    inj_rows = rows([_cmul(coef_re, coef_im, *pows[m]) for m in range(L)])

    def compact(re, im):
        return jnp.stack([re, im], axis=1).astype(F32).reshape(
            N_SLABS, GROUPS_PER_SLAB, 2, SSM_GROUP_CH, SSM_STATE).transpose(0, 2, 1, 3, 4).reshape(
            N_SLABS, 2, LANES, SSM_STATE)

    bt = compact(b_re.transpose(0, 2, 1), b_im.transpose(0, 2, 1))
    ct = compact(c_re, c_im)

    toep, wb, wc = pl.pallas_call(
        _ssm_ops_kernel,
        grid=(N_SLABS,),
        in_specs=[
            pl.BlockSpec((None, 2, LANES, SSM_STATE), lambda q: (q, 0, 0, 0)),
            pl.BlockSpec((None, 2, LANES, SSM_STATE), lambda q: (q, 0, 0, 0)),
            pl.BlockSpec((None, 2, SUBLANES, SLAB_STATE), lambda q: (q, 0, 0, 0)),
            pl.BlockSpec((None, 2, SUBLANES, SLAB_STATE), lambda q: (q, 0, 0, 0)),
        ],
        out_specs=[
            pl.BlockSpec((None, L * LANES, L * LANES), lambda q: (q, 0, 0)),
            pl.BlockSpec((None, L * LANES, 2 * SLAB_STATE), lambda q: (q, 0, 0)),
            pl.BlockSpec((None, 2 * SLAB_STATE, L * LANES), lambda q: (q, 0, 0)),
        ],
        out_shape=[
            jax.ShapeDtypeStruct((N_SLABS, L * LANES, L * LANES), BF16),
            jax.ShapeDtypeStruct((N_SLABS, L * LANES, 2 * SLAB_STATE), BF16),
            jax.ShapeDtypeStruct((N_SLABS, 2 * SLAB_STATE, L * LANES), BF16),
        ],
        compiler_params=pltpu.CompilerParams(
            dimension_semantics=("arbitrary",), vmem_limit_bytes=VMEM_LIMIT_BYTES),
        name="ssm_ops",
    )(bt, ct, lam_rows, inj_rows)

    row = jnp.arange(SUBLANES)
    planes = []
    for step in SCAN_STEPS:
        keep = (row >= step)[None, :, None]
        planes.append(jnp.where(keep, slab(pows[L * step][0])[:, None, :], 0.0))
        planes.append(jnp.where(keep, slab(pows[L * step][1])[:, None, :], 0.0))
    planes.append(jnp.stack([slab(pows[L * (t + 1)][0]) for t in range(SUBLANES)], axis=1))
    planes.append(jnp.stack([slab(pows[L * (t + 1)][1]) for t in range(SUBLANES)], axis=1))
    tab = jnp.stack(planes, axis=1)
    return toep, wb, wc, tab


def kernel(x, cond, w_mod, b_mod, ffn1_norm, ffn1_w_gate, ffn1_w_up, ffn1_w_down, mix_norm, w_in, conv_w, lambda_re, lambda_im, log_dt, ssm_b_re, ssm_b_im, ssm_c_re, ssm_c_im, ssm_d, glu_w, glu_b, out_norm_conv, out_norm_ssm, w_out, ffn2_norm, ffn2_w_gate, ffn2_w_up, ffn2_w_down, final_norm):
    depth = w_mod.shape[0]
    for l in range(depth):
        mod = _modulation(cond, w_mod[l], b_mod[l])
        later = (w_in[l], glu_w[l], w_out[l], ffn2_w_gate[l], ffn2_w_up[l], ffn2_w_down[l])
        x, (w_in_b, glu_w_b, w_out_b, wg2, wu2, wd2) = _ffn(
            x, mod, ffn1_norm[l], ffn1_w_gate[l].astype(BF16), ffn1_w_up[l].astype(BF16),
            ffn1_w_down[l].astype(BF16), final_norm, mod_base=0, final_norm=False, cast=later)
        toep, wb, wc, tab = _ssm_operators(lambda_re[l], lambda_im[l], log_dt[l], ssm_b_re[l],
                                           ssm_b_im[l], ssm_c_re[l], ssm_c_im[l])
        x = _mixer(x, mod, mix_norm[l], w_in_b, conv_w[l], toep, wb, wc, tab,
                   ssm_d[l], glu_w_b, glu_b[l], out_norm_conv[l], out_norm_ssm[l], w_out_b)
        x, _ = _ffn(x, mod, ffn2_norm[l], wg2, wu2, wd2, final_norm, mod_base=6,
                    final_norm=(l == depth - 1))
    return x
```

```python
import functools
import math

import jax
import jax.numpy as jnp
from jax import lax
from jax.experimental import pallas as pl
from jax.experimental.pallas import tpu as pltpu

D_MODEL = 1024
D_CONV = 512
D_SSM = 512
CONV_WIDTH = 3
SSM_GROUP_CH = 16
SSM_GROUPS = 32
SSM_STATE = 64
D_FF = 2816
N_MOD = 9
EPS = 1e-6

SUBLANES = 8
LANES = 128
VMEM_LIMIT_BYTES = 56 * 1024 * 1024

MOD_COLS = 2304
FFN_ROWS = 1024
FFN_SUB = 256
FFN_COLS = 256
MIX_ROWS = 1024
MIX_SUB = 256
SSM_CHUNK = 4
GROUPS_PER_SLAB = LANES // SSM_GROUP_CH
N_SLABS = D_SSM // LANES
SLAB_STATE = GROUPS_PER_SLAB * SSM_STATE
SCAN_STEPS = (1, 2, 4)

BF16 = jnp.bfloat16
F32 = jnp.float32


def _dot(a, b):
    return jnp.dot(a, b, preferred_element_type=F32)


def _sigmoid(x):
    return 1.0 / (1.0 + jnp.exp(-x))


def _rms_norm(x, gain):
    ms = jnp.mean(x * x, axis=-1, keepdims=True)
    return x * lax.rsqrt(ms + EPS) * gain


def _gelu_tanh(x):
    c = math.sqrt(2.0 / math.pi)
    return x * (0.5 * (1.0 + jnp.tanh(c * (x + 0.044715 * (x * x * x)))))


def _mod_kernel(c_ref, w_ref, b_ref, o_ref):
    c = c_ref[...]
    c = c * _sigmoid(c)
    o_ref[...] = _dot(c.astype(BF16), w_ref[...].astype(BF16)) + b_ref[...]


def _modulation(cond, w_mod, b_mod):
    bsz = cond.shape[0]
    rows = -(-bsz // SUBLANES) * SUBLANES
    cond_p = jnp.pad(cond, ((0, rows - bsz), (0, 0)))
    n = w_mod.shape[1]
    tn = MOD_COLS
    out = pl.pallas_call(
        _mod_kernel,
        grid=(n // tn,),
        in_specs=[
            pl.BlockSpec((rows, D_MODEL), lambda j: (0, 0)),
            pl.BlockSpec((D_MODEL, tn), lambda j: (0, j)),
            pl.BlockSpec((1, tn), lambda j: (0, j)),
        ],
        out_specs=pl.BlockSpec((rows, tn), lambda j: (0, j)),
        out_shape=jax.ShapeDtypeStruct((rows, n), F32),
        compiler_params=pltpu.CompilerParams(
            dimension_semantics=("arbitrary",), vmem_limit_bytes=VMEM_LIMIT_BYTES),
        name="adaln_mod",
    )(cond_p, w_mod, b_mod.reshape(1, n))
    return out[:bsz].reshape(bsz, N_MOD, D_MODEL)


def _ffn_kernel(*refs, mod_base, final_norm, n_cast):
    x_ref, mod_ref, gain_ref, wg_ref, wu_ref, wd_ref, fgain_ref = refs[:7]
    cast_in = refs[7:7 + n_cast]
    o_ref = refs[7 + n_cast]
    cast_out = refs[8 + n_cast:8 + 2 * n_cast]
    act_ref = refs[8 + 2 * n_cast]
    for src, dst in zip(cast_in, cast_out):
        dst[...] = src[...].astype(BF16)
    shift = mod_ref[mod_base:mod_base + 1, :]
    scale = mod_ref[mod_base + 1:mod_base + 2, :]
    gate = mod_ref[mod_base + 2:mod_base + 3, :]
    for r in range(FFN_ROWS // FFN_SUB):
        rows = slice(r * FFN_SUB, (r + 1) * FFN_SUB)
        x = x_ref[rows, :]
        h = (_rms_norm(x, gain_ref[...]) * (1.0 + scale) + shift).astype(BF16)
        for f in range(D_FF // FFN_COLS):
            cols = slice(f * FFN_COLS, (f + 1) * FFN_COLS)
            g = _dot(h, wg_ref[:, cols])
            u = _dot(h, wu_ref[:, cols])
            act_ref[rows, cols] = (g * _sigmoid(g) * u).astype(BF16)
        y = _dot(act_ref[rows, :], wd_ref[...])
        out = x + (0.5 * gate) * y
        if final_norm:
            out = _rms_norm(out, fgain_ref[...])
        o_ref[rows, :] = out


def _const_spec(shape):
    zeros = (0,) * len(shape)
    return pl.BlockSpec(shape, lambda *_: zeros, pipeline_mode=pl.Buffered(1))


def _ffn(x, mod, gain, w_gate, w_up, w_down, final_gain, *, mod_base, final_norm, cast=()):
    bsz, seq, _ = x.shape
    tm = FFN_ROWS
    n_t = seq // tm
    steps = bsz * n_t
    kernel = functools.partial(_ffn_kernel, mod_base=mod_base, final_norm=final_norm,
                               n_cast=len(cast))
    band_specs = [pl.BlockSpec((w.shape[0] // steps, w.shape[1]), lambda b, t: (b * n_t + t, 0))
                  for w in cast]
    outs = pl.pallas_call(
        kernel,
        grid=(bsz, n_t),
        in_specs=[
            pl.BlockSpec((None, tm, D_MODEL), lambda b, t: (b, t, 0)),
            pl.BlockSpec((None, N_MOD, D_MODEL), lambda b, t: (b, 0, 0)),
            _const_spec((1, D_MODEL)),
            _const_spec((D_MODEL, D_FF)),
            _const_spec((D_MODEL, D_FF)),
            _const_spec((D_FF, D_MODEL)),
            _const_spec((1, D_MODEL)),
        ] + band_specs,
        out_specs=[pl.BlockSpec((None, tm, D_MODEL), lambda b, t: (b, t, 0))] + band_specs,
        out_shape=[jax.ShapeDtypeStruct(x.shape, F32)]
                  + [jax.ShapeDtypeStruct(w.shape, BF16) for w in cast],
        scratch_shapes=[pltpu.VMEM((tm, D_FF), BF16)],
        compiler_params=pltpu.CompilerParams(
            dimension_semantics=("arbitrary", "arbitrary"),
            vmem_limit_bytes=VMEM_LIMIT_BYTES),
        name="ffn_final" if final_norm else "ffn",
    )(x, mod, gain.reshape(1, D_MODEL), w_gate, w_up, w_down,
      final_gain.reshape(1, D_MODEL), *cast)
    return outs[0], tuple(outs[1:])


def _mixer_kernel(x_ref, mod_ref, gain_ref, win_ref, convw_ref, toep_ref, wb_ref, wc_ref,
                  tab_ref, dskip_ref, gluw_ref, glub_ref, gain_c_ref, gain_s_ref, wout_ref,
                  o_ref, zbuf, ubuf, ybuf, hbuf, carry, ymix):
    tm = MIX_ROWS
    chunk = SSM_CHUNK
    m_rows = tm // chunk
    halo = SUBLANES

    @pl.when(pl.program_id(1) == 0)
    def _():
        zbuf[0:halo, :] = jnp.zeros((halo, D_CONV), F32)
        ubuf[:, 0:halo, :] = jnp.zeros((N_SLABS, halo, LANES), F32)
        carry[...] = jnp.zeros(carry.shape, F32)

    x = x_ref[...]
    shift = mod_ref[3:4, :]
    scale = mod_ref[4:5, :]
    gate = mod_ref[5:6, :]
    h = (_rms_norm(x, gain_ref[...]) * (1.0 + scale) + shift).astype(BF16)


    u = _dot(h, win_ref[:, 3 * D_CONV:])
    for q in range(N_SLABS):
        ubuf[q, halo:halo + tm, :] = u[:, q * LANES:(q + 1) * LANES]
    u_curs, injs = [], []
    for q in range(N_SLABS):
        cur = [ubuf[q, pl.ds(halo + s, m_rows, stride=chunk), :] for s in range(chunk)]
        prv = [ubuf[q, pl.ds(halo - chunk + s, m_rows, stride=chunk), :] for s in range(chunk)]
        u_curs.append(jnp.concatenate(cur, axis=1).astype(BF16))
        u_prv = jnp.concatenate(prv, axis=1).astype(BF16)
        ubuf[q, 0:halo, :] = ubuf[q, tm:tm + halo, :]
        injs.append(_dot(u_prv, wb_ref[q]))

    conv_proj = []
    for q in range(N_SLABS):
        if q < 3:
            conv_proj.append(_dot(h, win_ref[:, q * D_CONV:(q + 1) * D_CONV]))
        inj = injs[q]
        for i in range(SLAB_STATE // LANES):
            lanes_re = slice(i * LANES, (i + 1) * LANES)
            lanes_im = slice(SLAB_STATE + i * LANES, SLAB_STATE + (i + 1) * LANES)
            re = inj[:, lanes_re].reshape(m_rows // SUBLANES, SUBLANES, LANES)
            im = inj[:, lanes_im].reshape(m_rows // SUBLANES, SUBLANES, LANES)
            for k, step in enumerate(SCAN_STEPS):
                a_re = tab_ref[q, 2 * k, :, lanes_re]
                a_im = tab_ref[q, 2 * k + 1, :, lanes_re]
                s_re = pltpu.roll(re, step, axis=1)
                s_im = pltpu.roll(im, step, axis=1)
                re, im = re + a_re * s_re - a_im * s_im, im + a_re * s_im + a_im * s_re
            p_re = tab_ref[q, 2 * len(SCAN_STEPS), :, lanes_re]
            p_im = tab_ref[q, 2 * len(SCAN_STEPS) + 1, :, lanes_re]
            c_re = carry[q:q + 1, lanes_re]
            c_im = carry[q:q + 1, lanes_im]
            for r in range(m_rows // SUBLANES):
                rows = slice(r * SUBLANES, (r + 1) * SUBLANES)
                b_re = jnp.broadcast_to(c_re, (SUBLANES, LANES))
                b_im = jnp.broadcast_to(c_im, (SUBLANES, LANES))
                f_re = re[r] + p_re * b_re - p_im * b_im
                f_im = im[r] + p_re * b_im + p_im * b_re
                hbuf[q, rows, lanes_re] = f_re
                hbuf[q, rows, lanes_im] = f_im
                c_re = f_re[SUBLANES - 1:SUBLANES, :]
                c_im = f_im[SUBLANES - 1:SUBLANES, :]
            carry[q:q + 1, lanes_re] = c_re
            carry[q:q + 1, lanes_im] = c_im

    for q in range(N_SLABS):
        y2 = _dot(u_curs[q], toep_ref[q]) + _dot(hbuf[q].astype(BF16), wc_ref[q])
        for s in range(chunk):
            ybuf[q, pl.ds(s, m_rows, stride=chunk), :] = y2[:, s * LANES:(s + 1) * LANES]
    b_gate, c_gate, v = conv_proj
    z = c_gate * v
    zbuf[halo:halo + tm, :] = z
    z1 = zbuf[halo - 1:halo - 1 + tm, :]
    z2 = zbuf[halo - 2:halo - 2 + tm, :]
    y_conv = b_gate * (convw_ref[0:1, :] * z2 + convw_ref[1:2, :] * z1 + convw_ref[2:3, :] * z)
    zbuf[0:halo, :] = zbuf[tm:tm + halo, :]
    ymix[:, 0:D_CONV] = _rms_norm(y_conv, gain_c_ref[...]).astype(BF16)

    for r in range(MIX_ROWS // MIX_SUB):
        rows = slice(r * MIX_SUB, (r + 1) * MIX_SUB)
        u_rows = slice(halo + r * MIX_SUB, halo + (r + 1) * MIX_SUB)
        y = jnp.concatenate([ybuf[q, rows, :] for q in range(N_SLABS)], axis=1)
        y = y + dskip_ref[...] * jnp.concatenate([ubuf[q, u_rows, :] for q in range(N_SLABS)], axis=1)
        y = _gelu_tanh(y)
        y = y * _sigmoid(_dot(y.astype(BF16), gluw_ref[...]) + glub_ref[...])
        ymix[rows, D_CONV:] = _rms_norm(y, gain_s_ref[...]).astype(BF16)
        o_ref[rows, :] = x_ref[rows, :] + gate * _dot(ymix[rows, :], wout_ref[...])


def _mixer(x, mod, gain, w_in, conv_w, toep, wb, wc, tab, d_skip, glu_w, glu_b,
           gain_c, gain_s, w_out):
    bsz, seq, _ = x.shape
    tm = MIX_ROWS
    m_rows = tm // SSM_CHUNK
    return pl.pallas_call(
        _mixer_kernel,
        grid=(bsz, seq // tm),
        in_specs=[
            pl.BlockSpec((None, tm, D_MODEL), lambda b, t: (b, t, 0)),
            pl.BlockSpec((None, N_MOD, D_MODEL), lambda b, t: (b, 0, 0)),
            _const_spec((1, D_MODEL)),
            _const_spec(w_in.shape),
            _const_spec(conv_w.shape),
            _const_spec(toep.shape),
            _const_spec(wb.shape),
            _const_spec(wc.shape),
            _const_spec(tab.shape),
            _const_spec((1, D_SSM)),
            _const_spec(glu_w.shape),
            _const_spec((1, D_SSM)),
            _const_spec((1, D_CONV)),
            _const_spec((1, D_SSM)),
            _const_spec(w_out.shape),
        ],
        out_specs=pl.BlockSpec((None, tm, D_MODEL), lambda b, t: (b, t, 0)),
        out_shape=jax.ShapeDtypeStruct(x.shape, F32),
        scratch_shapes=[
            pltpu.VMEM((SUBLANES + tm, D_CONV), F32),
            pltpu.VMEM((N_SLABS, SUBLANES + tm, LANES), F32),
            pltpu.VMEM((N_SLABS, tm, LANES), F32),
            pltpu.VMEM((N_SLABS, m_rows, 2 * SLAB_STATE), F32),
            pltpu.VMEM((N_SLABS, 2 * SLAB_STATE), F32),
            pltpu.VMEM((tm, D_MODEL), BF16),
        ],
        compiler_params=pltpu.CompilerParams(
            dimension_semantics=("arbitrary", "arbitrary"),
            vmem_limit_bytes=VMEM_LIMIT_BYTES),
        name="mixer",
    )(x, mod, gain.reshape(1, D_MODEL), w_in, conv_w, toep, wb, wc, tab,
      d_skip.reshape(1, D_SSM), glu_w, glu_b.reshape(1, D_SSM),
      gain_c.reshape(1, D_CONV), gain_s.reshape(1, D_SSM), w_out)


def _cmul(ar, ai, br, bi):
    return ar * br - ai * bi, ar * bi + ai * br


def _ssm_ops_kernel(bt_ref, ct_ref, lam_ref, inj_ref, toep_ref, wb_ref, wc_ref):
    L = SSM_CHUNK
    shape = (LANES, SLAB_STATE)
    row_group = lax.broadcasted_iota(jnp.int32, shape, 0) >> (SSM_GROUP_CH.bit_length() - 1)
    col_group = lax.broadcasted_iota(jnp.int32, shape, 1) >> (SSM_STATE.bit_length() - 1)
    same_group = row_group == col_group
    tile_p = (lax.broadcasted_iota(jnp.int32, (SSM_STATE, SLAB_STATE), 0)
              == (lax.broadcasted_iota(jnp.int32, (SSM_STATE, SLAB_STATE), 1) & (SSM_STATE - 1))
              ).astype(BF16)

    def split(a):
        head = a.astype(BF16)
        return head, (a - head.astype(F32)).astype(BF16)

    def block_diag(a):
        head, rest = split(a)
        return jnp.where(same_group, _dot(head, tile_p) + _dot(rest, tile_p), 0.0)

    bt_re, bt_im = block_diag(bt_ref[0]), block_diag(bt_ref[1])
    ct_re, ct_im = block_diag(ct_ref[0]), block_diag(ct_ref[1])
    ct_re_parts, ct_im_parts = split(ct_re), split(ct_im)

    def dot_nt(a, b_parts):
        nt = (((1,), (1,)), ((), ()))
        a_head, a_rest = split(a)
        b_head, b_rest = b_parts
        return (lax.dot_general(a_head, b_head, nt, preferred_element_type=F32)
                + lax.dot_general(a_rest, b_head, nt, preferred_element_type=F32)
                + lax.dot_general(a_head, b_rest, nt, preferred_element_type=F32))

    taps = []
    for m in range(L):
        s = L - 1 - m
        w_re, w_im = _cmul(bt_re, bt_im, inj_ref[0, m:m + 1, :], inj_ref[1, m:m + 1, :])
        wb_ref[s * LANES:(s + 1) * LANES, 0:SLAB_STATE] = w_re.astype(BF16)
        wb_ref[s * LANES:(s + 1) * LANES, SLAB_STATE:] = w_im.astype(BF16)
        taps.append(dot_nt(w_re, ct_re_parts) - dot_nt(w_im, ct_im_parts))
    for s in range(L):
        for t in range(L):
            blk = taps[t - s] if t >= s else jnp.zeros((LANES, LANES), F32)
            toep_ref[s * LANES:(s + 1) * LANES, t * LANES:(t + 1) * LANES] = blk.astype(BF16)
    for t in range(L):
        o_re, o_im = _cmul(ct_re, ct_im, lam_ref[0, t:t + 1, :], lam_ref[1, t:t + 1, :])
        wc_ref[0:SLAB_STATE, t * LANES:(t + 1) * LANES] = o_re.T.astype(BF16)
        wc_ref[SLAB_STATE:, t * LANES:(t + 1) * LANES] = (-o_im).T.astype(BF16)


def _ssm_operators(lambda_re, lambda_im, log_dt, b_re, b_im, c_re, c_im):
    L = SSM_CHUNK
    lre, lim = lambda_re.astype(F32), lambda_im.astype(F32)
    dt = jnp.exp(log_dt.astype(F32))[:, None]
    mag = jnp.exp(lre * dt)
    bar_re, bar_im = mag * jnp.cos(lim * dt), mag * jnp.sin(lim * dt)
    den = lre * lre + lim * lim
    coef_re = ((bar_re - 1.0) * lre + bar_im * lim) / den
    coef_im = (bar_im * lre - (bar_re - 1.0) * lim) / den
    n_pow = SUBLANES * L
    pows = [(jnp.ones_like(bar_re), jnp.zeros_like(bar_im))]
    for _ in range(n_pow):
        pows.append(_cmul(pows[-1][0], pows[-1][1], bar_re, bar_im))

    def slab(a):
        return a.reshape(N_SLABS, SLAB_STATE)

    def rows(vals):
        pad = [(jnp.zeros_like(bar_re),) * 2] * (SUBLANES - len(vals))
        vals = list(vals) + pad
        return jnp.stack([jnp.stack([slab(v[r]) for v in vals], axis=1) for r in range(2)], axis=1)

    lam_rows = rows([pows[m + 1] for m in range(L)])
    inj_rows = rows([_cmul(coef_re, coef_im, *pows[m]) for m in range(L)])

    def compact(re, im):
        return jnp.stack([re, im], axis=1).astype(F32).reshape(
            N_SLABS, GROUPS_PER_SLAB, 2, SSM_GROUP_CH, SSM_STATE).transpose(0, 2, 1, 3, 4).reshape(
            N_SLABS, 2, LANES, SSM_STATE)

    bt = compact(b_re.transpose(0, 2, 1), b_im.transpose(0, 2, 1))
    ct = compact(c_re, c_im)

    toep, wb, wc = pl.pallas_call(
        _ssm_ops_kernel,
        grid=(N_SLABS,),
        in_specs=[
            pl.BlockSpec((None, 2, LANES, SSM_STATE), lambda q: (q, 0, 0, 0)),
            pl.BlockSpec((None, 2, LANES, SSM_STATE), lambda q: (q, 0, 0, 0)),
            pl.BlockSpec((None, 2, SUBLANES, SLAB_STATE), lambda q: (q, 0, 0, 0)),
            pl.BlockSpec((None, 2, SUBLANES, SLAB_STATE), lambda q: (q, 0, 0, 0)),
        ],
        out_specs=[
            pl.BlockSpec((None, L * LANES, L * LANES), lambda q: (q, 0, 0)),
            pl.BlockSpec((None, L * LANES, 2 * SLAB_STATE), lambda q: (q, 0, 0)),
            pl.BlockSpec((None, 2 * SLAB_STATE, L * LANES), lambda q: (q, 0, 0)),
        ],
        out_shape=[
            jax.ShapeDtypeStruct((N_SLABS, L * LANES, L * LANES), BF16),
            jax.ShapeDtypeStruct((N_SLABS, L * LANES, 2 * SLAB_STATE), BF16),
            jax.ShapeDtypeStruct((N_SLABS, 2 * SLAB_STATE, L * LANES), BF16),
        ],
        compiler_params=pltpu.CompilerParams(
            dimension_semantics=("arbitrary",), vmem_limit_bytes=VMEM_LIMIT_BYTES),
        name="ssm_ops",
    )(bt, ct, lam_rows, inj_rows)

    row = jnp.arange(SUBLANES)
    planes = []
    for step in SCAN_STEPS:
        keep = (row >= step)[None, :, None]
        planes.append(jnp.where(keep, slab(pows[L * step][0])[:, None, :], 0.0))
        planes.append(jnp.where(keep, slab(pows[L * step][1])[:, None, :], 0.0))
    planes.append(jnp.stack([slab(pows[L * (t + 1)][0]) for t in range(SUBLANES)], axis=1))
    planes.append(jnp.stack([slab(pows[L * (t + 1)][1]) for t in range(SUBLANES)], axis=1))
    tab = jnp.stack(planes, axis=1)
    return toep, wb, wc, tab


def kernel(x, cond, w_mod, b_mod, ffn1_norm, ffn1_w_gate, ffn1_w_up, ffn1_w_down, mix_norm, w_in, conv_w, lambda_re, lambda_im, log_dt, ssm_b_re, ssm_b_im, ssm_c_re, ssm_c_im, ssm_d, glu_w, glu_b, out_norm_conv, out_norm_ssm, w_out, ffn2_norm, ffn2_w_gate, ffn2_w_up, ffn2_w_down, final_norm):
    depth = w_mod.shape[0]
    for l in range(depth):
        mod = _modulation(cond, w_mod[l], b_mod[l])
        later = (w_in[l], glu_w[l], w_out[l], ffn2_w_gate[l], ffn2_w_up[l], ffn2_w_down[l])
        x, (w_in_b, glu_w_b, w_out_b, wg2, wu2, wd2) = _ffn(
            x, mod, ffn1_norm[l], ffn1_w_gate[l].astype(BF16), ffn1_w_up[l].astype(BF16),
            ffn1_w_down[l].astype(BF16), final_norm, mod_base=0, final_norm=False, cast=later)
        toep, wb, wc, tab = _ssm_operators(lambda_re[l], lambda_im[l], log_dt[l], ssm_b_re[l],
                                           ssm_b_im[l], ssm_c_re[l], ssm_c_im[l])
        x = _mixer(x, mod, mix_norm[l], w_in_b, conv_w[l], toep, wb, wc, tab,
                   ssm_d[l], glu_w_b, glu_b[l], out_norm_conv[l], out_norm_ssm[l], w_out_b)
        x, _ = _ffn(x, mod, ffn2_norm[l], wg2, wu2, wd2, final_norm, mod_base=6,
                    final_norm=(l == depth - 1))
    return x
```

```python
import functools
import math

import jax
import jax.numpy as jnp
from jax import lax
from jax.experimental import pallas as pl
from jax.experimental.pallas import tpu as pltpu

D_MODEL = 1024
D_CONV = 512
D_SSM = 512
CONV_WIDTH = 3
SSM_GROUP_CH = 16
SSM_GROUPS = 32
SSM_STATE = 64
D_FF = 2816
N_MOD = 9
EPS = 1e-6

SUBLANES = 8
LANES = 128
VMEM_LIMIT_BYTES = 56 * 1024 * 1024

MOD_COLS = 3072
MOD_ROWS = SUBLANES
FFN_ROWS = 1024
FFN_SUB = 256
FFN_COLS = 256
MIX_ROWS = 1024
MIX_SUB = 256
SSM_CHUNK = 4
GROUPS_PER_SLAB = LANES // SSM_GROUP_CH
N_SLABS = D_SSM // LANES
SLAB_STATE = GROUPS_PER_SLAB * SSM_STATE
SCAN_STEPS = (1, 2, 4)

BF16 = jnp.bfloat16
F32 = jnp.float32


def _dot(a, b):
    return jnp.dot(a, b, preferred_element_type=F32)


def _sigmoid(x):
    return 1.0 / (1.0 + jnp.exp(-x))


def _rms_norm(x, gain):
    ms = jnp.mean(x * x, axis=-1, keepdims=True)
    return x * lax.rsqrt(ms + EPS) * gain


def _gelu_tanh(x):
    c = math.sqrt(2.0 / math.pi)
    return x * (0.5 * (1.0 + jnp.tanh(c * (x + 0.044715 * (x * x * x)))))


def _mod_kernel(c_ref, w_ref, b_ref, o_ref):
    c = c_ref[...]
    c = c * _sigmoid(c)
    pad = jnp.zeros((MOD_ROWS - c.shape[0], D_MODEL), F32)
    c = jnp.concatenate([c, pad], axis=0)
    res = _dot(c.astype(BF16), w_ref[...].astype(BF16)) + b_ref[...]
    for i in range(MOD_COLS // D_MODEL):
        o_ref[i] = res[:, i * D_MODEL:(i + 1) * D_MODEL]


def _modulation(cond, w_mod, b_mod):
    bsz = cond.shape[0]
    assert bsz <= MOD_ROWS
    n = w_mod.shape[1]
    tn = MOD_COLS
    return pl.pallas_call(
        _mod_kernel,
        grid=(n // tn,),
        in_specs=[
            pl.BlockSpec((bsz, D_MODEL), lambda j: (0, 0)),
            pl.BlockSpec((D_MODEL, tn), lambda j: (0, j)),
            pl.BlockSpec((1, tn), lambda j: (0, j)),
        ],
        out_specs=pl.BlockSpec((tn // D_MODEL, MOD_ROWS, D_MODEL), lambda j: (j, 0, 0)),
        out_shape=jax.ShapeDtypeStruct((N_MOD, MOD_ROWS, D_MODEL), F32),
        compiler_params=pltpu.CompilerParams(
            dimension_semantics=("arbitrary",), vmem_limit_bytes=VMEM_LIMIT_BYTES),
        name="adaln_mod",
    )(cond, w_mod, b_mod.reshape(1, n))


def _ffn_kernel(*refs, mod_base, final_norm, n_cast):
    x_ref, mod_ref, gain_ref, wg_ref, wu_ref, wd_ref, fgain_ref = refs[:7]
    cast_in = refs[7:7 + n_cast]
    o_ref = refs[7 + n_cast]
    cast_out = refs[8 + n_cast:8 + 2 * n_cast]
    act_ref = refs[8 + 2 * n_cast]
    for src, dst in zip(cast_in, cast_out):
        dst[...] = src[...].astype(BF16)
    batch = pl.ds(pl.program_id(0), 1)
    shift = mod_ref[mod_base, batch, :]
    scale = mod_ref[mod_base + 1, batch, :]
    gate = mod_ref[mod_base + 2, batch, :]
    for r in range(FFN_ROWS // FFN_SUB):
        rows = slice(r * FFN_SUB, (r + 1) * FFN_SUB)
        x = x_ref[rows, :]
        h = (_rms_norm(x, gain_ref[...]) * (1.0 + scale) + shift).astype(BF16)
        for f in range(D_FF // FFN_COLS):
            cols = slice(f * FFN_COLS, (f + 1) * FFN_COLS)
            g = _dot(h, wg_ref[:, cols])
            u = _dot(h, wu_ref[:, cols])
            act_ref[rows, cols] = (g * _sigmoid(g) * u).astype(BF16)
        y = _dot(act_ref[rows, :], wd_ref[...])
        out = x + (0.5 * gate) * y
        if final_norm:
            out = _rms_norm(out, fgain_ref[...])
        o_ref[rows, :] = out


def _const_spec(shape):
    zeros = (0,) * len(shape)
    return pl.BlockSpec(shape, lambda *_: zeros, pipeline_mode=pl.Buffered(1))


def _ffn(x, mod, gain, w_gate, w_up, w_down, final_gain, *, mod_base, final_norm, cast=()):
    bsz, seq, _ = x.shape
    tm = FFN_ROWS
    n_t = seq // tm
    steps = bsz * n_t
    kernel = functools.partial(_ffn_kernel, mod_base=mod_base, final_norm=final_norm,
                               n_cast=len(cast))
    band_specs = [pl.BlockSpec((w.shape[0] // steps, w.shape[1]), lambda b, t: (b * n_t + t, 0))
                  for w in cast]
    outs = pl.pallas_call(
        kernel,
        grid=(bsz, n_t),
        in_specs=[
            pl.BlockSpec((None, tm, D_MODEL), lambda b, t: (b, t, 0)),
            _const_spec((N_MOD, MOD_ROWS, D_MODEL)),
            _const_spec((1, D_MODEL)),
            _const_spec((D_MODEL, D_FF)),
            _const_spec((D_MODEL, D_FF)),
            _const_spec((D_FF, D_MODEL)),
            _const_spec((1, D_MODEL)),
        ] + band_specs,
        out_specs=[pl.BlockSpec((None, tm, D_MODEL), lambda b, t: (b, t, 0))] + band_specs,
        out_shape=[jax.ShapeDtypeStruct(x.shape, F32)]
                  + [jax.ShapeDtypeStruct(w.shape, BF16) for w in cast],
        scratch_shapes=[pltpu.VMEM((tm, D_FF), BF16)],
        compiler_params=pltpu.CompilerParams(
            dimension_semantics=("arbitrary", "arbitrary"),
            vmem_limit_bytes=VMEM_LIMIT_BYTES),
        name="ffn_final" if final_norm else "ffn",
    )(x, mod, gain.reshape(1, D_MODEL), w_gate, w_up, w_down,
      final_gain.reshape(1, D_MODEL), *cast)
    return outs[0], tuple(outs[1:])


def _mixer_kernel(x_ref, mod_ref, gain_ref, win_ref, convw_ref, toep_ref, wb_ref, wc_ref,
                  tab_ref, dskip_ref, gluw_ref, glub_ref, gain_c_ref, gain_s_ref, wout_ref,
                  o_ref, zbuf, ubuf, ybuf, hbuf, carry, ymix):
    tm = MIX_ROWS
    chunk = SSM_CHUNK
    m_rows = tm // chunk
    halo = SUBLANES

    @pl.when(pl.program_id(1) == 0)
    def _():
        zbuf[0:halo, :] = jnp.zeros((halo, D_CONV), F32)
        ubuf[:, 0:halo, :] = jnp.zeros((N_SLABS, halo, LANES), F32)
        carry[...] = jnp.zeros(carry.shape, F32)

    x = x_ref[...]
    batch = pl.ds(pl.program_id(0), 1)
    shift = mod_ref[3, batch, :]
    scale = mod_ref[4, batch, :]
    gate = mod_ref[5, batch, :]
    h = (_rms_norm(x, gain_ref[...]) * (1.0 + scale) + shift).astype(BF16)


    u = _dot(h, win_ref[:, 3 * D_CONV:])
    for q in range(N_SLABS):
        ubuf[q, halo:halo + tm, :] = u[:, q * LANES:(q + 1) * LANES]
    u_curs, injs = [], []
    for q in range(N_SLABS):
        cur = [ubuf[q, pl.ds(halo + s, m_rows, stride=chunk), :] for s in range(chunk)]
        prv = [ubuf[q, pl.ds(halo - chunk + s, m_rows, stride=chunk), :] for s in range(chunk)]
        u_curs.append(jnp.concatenate(cur, axis=1).astype(BF16))
        u_prv = jnp.concatenate(prv, axis=1).astype(BF16)
        ubuf[q, 0:halo, :] = ubuf[q, tm:tm + halo, :]
        injs.append(_dot(u_prv, wb_ref[q]))

    conv_proj = []
    for q in range(N_SLABS):
        if q < 3:
            conv_proj.append(_dot(h, win_ref[:, q * D_CONV:(q + 1) * D_CONV]))
        inj = injs[q]
        for i in range(SLAB_STATE // LANES):
            lanes_re = slice(i * LANES, (i + 1) * LANES)
            lanes_im = slice(SLAB_STATE + i * LANES, SLAB_STATE + (i + 1) * LANES)
            re = inj[:, lanes_re].reshape(m_rows // SUBLANES, SUBLANES, LANES)
            im = inj[:, lanes_im].reshape(m_rows // SUBLANES, SUBLANES, LANES)
            for k, step in enumerate(SCAN_STEPS):
                a_re = tab_ref[q, 2 * k, :, lanes_re]
                a_im = tab_ref[q, 2 * k + 1, :, lanes_re]
                s_re = pltpu.roll(re, step, axis=1)
                s_im = pltpu.roll(im, step, axis=1)
                re, im = re + a_re * s_re - a_im * s_im, im + a_re * s_im + a_im * s_re
            p_re = tab_ref[q, 2 * len(SCAN_STEPS), :, lanes_re]
            p_im = tab_ref[q, 2 * len(SCAN_STEPS) + 1, :, lanes_re]
            c_re = carry[q:q + 1, lanes_re]
            c_im = carry[q:q + 1, lanes_im]
            for r in range(m_rows // SUBLANES):
                rows = slice(r * SUBLANES, (r + 1) * SUBLANES)
                b_re = jnp.broadcast_to(c_re, (SUBLANES, LANES))
                b_im = jnp.broadcast_to(c_im, (SUBLANES, LANES))
                f_re = re[r] + p_re * b_re - p_im * b_im
                f_im = im[r] + p_re * b_im + p_im * b_re
                hbuf[q, rows, lanes_re] = f_re
                hbuf[q, rows, lanes_im] = f_im
                c_re = f_re[SUBLANES - 1:SUBLANES, :]
                c_im = f_im[SUBLANES - 1:SUBLANES, :]
            carry[q:q + 1, lanes_re] = c_re
            carry[q:q + 1, lanes_im] = c_im

    for q in range(N_SLABS):
        y2 = _dot(u_curs[q], toep_ref[q]) + _dot(hbuf[q].astype(BF16), wc_ref[q])
        for s in range(chunk):
            ybuf[q, pl.ds(s, m_rows, stride=chunk), :] = y2[:, s * LANES:(s + 1) * LANES]
    b_gate, c_gate, v = conv_proj
    z = c_gate * v
    zbuf[halo:halo + tm, :] = z
    z1 = zbuf[halo - 1:halo - 1 + tm, :]
    z2 = zbuf[halo - 2:halo - 2 + tm, :]
    y_conv = b_gate * (convw_ref[0:1, :] * z2 + convw_ref[1:2, :] * z1 + convw_ref[2:3, :] * z)
    zbuf[0:halo, :] = zbuf[tm:tm + halo, :]
    ymix[:, 0:D_CONV] = _rms_norm(y_conv, gain_c_ref[...]).astype(BF16)

    for r in range(MIX_ROWS // MIX_SUB):
        rows = slice(r * MIX_SUB, (r + 1) * MIX_SUB)
        u_rows = slice(halo + r * MIX_SUB, halo + (r + 1) * MIX_SUB)
        y = jnp.concatenate([ybuf[q, rows, :] for q in range(N_SLABS)], axis=1)
        y = y + dskip_ref[...] * jnp.concatenate([ubuf[q, u_rows, :] for q in range(N_SLABS)], axis=1)
        y = _gelu_tanh(y)
        y = y * _sigmoid(_dot(y.astype(BF16), gluw_ref[...]) + glub_ref[...])
        ymix[rows, D_CONV:] = _rms_norm(y, gain_s_ref[...]).astype(BF16)
        o_ref[rows, :] = x_ref[rows, :] + gate * _dot(ymix[rows, :], wout_ref[...])


def _mixer(x, mod, gain, w_in, conv_w, toep, wb, wc, tab, d_skip, glu_w, glu_b,
           gain_c, gain_s, w_out):
    bsz, seq, _ = x.shape
    tm = MIX_ROWS
    m_rows = tm // SSM_CHUNK
    return pl.pallas_call(
        _mixer_kernel,
        grid=(bsz, seq // tm),
        in_specs=[
            pl.BlockSpec((None, tm, D_MODEL), lambda b, t: (b, t, 0)),
            _const_spec((N_MOD, MOD_ROWS, D_MODEL)),
            _const_spec((1, D_MODEL)),
            _const_spec(w_in.shape),
            _const_spec(conv_w.shape),
            _const_spec(toep.shape),
            _const_spec(wb.shape),
            _const_spec(wc.shape),
            _const_spec(tab.shape),
            _const_spec((1, D_SSM)),
            _const_spec(glu_w.shape),
            _const_spec((1, D_SSM)),
            _const_spec((1, D_CONV)),
            _const_spec((1, D_SSM)),
            _const_spec(w_out.shape),
        ],
        out_specs=pl.BlockSpec((None, tm, D_MODEL), lambda b, t: (b, t, 0)),
        out_shape=jax.ShapeDtypeStruct(x.shape, F32),
        scratch_shapes=[
            pltpu.VMEM((SUBLANES + tm, D_CONV), F32),
            pltpu.VMEM((N_SLABS, SUBLANES + tm, LANES), F32),
            pltpu.VMEM((N_SLABS, tm, LANES), F32),
            pltpu.VMEM((N_SLABS, m_rows, 2 * SLAB_STATE), F32),
            pltpu.VMEM((N_SLABS, 2 * SLAB_STATE), F32),
            pltpu.VMEM((tm, D_MODEL), BF16),
        ],
        compiler_params=pltpu.CompilerParams(
            dimension_semantics=("arbitrary", "arbitrary"),
            vmem_limit_bytes=VMEM_LIMIT_BYTES),
        name="mixer",
    )(x, mod, gain.reshape(1, D_MODEL), w_in, conv_w, toep, wb, wc, tab,
      d_skip.reshape(1, D_SSM), glu_w, glu_b.reshape(1, D_SSM),
      gain_c.reshape(1, D_CONV), gain_s.reshape(1, D_SSM), w_out)


def _cmul(ar, ai, br, bi):
    return ar * br - ai * bi, ar * bi + ai * br


def _ssm_ops_kernel(bt_ref, ct_ref, base_ref, toep_ref, wb_ref, wc_ref, tab_ref):
    L = SSM_CHUNK
    bar = base_ref[0, 0:1, :], base_ref[1, 0:1, :]
    coef = base_ref[0, 1:2, :], base_ref[1, 1:2, :]
    pows = [(jnp.ones((1, SLAB_STATE), F32), jnp.zeros((1, SLAB_STATE), F32))]
    for _ in range(SUBLANES * L):
        pows.append(_cmul(*pows[-1], *bar))

    row = lax.broadcasted_iota(jnp.int32, (SUBLANES, SLAB_STATE), 0)
    for k, step in enumerate(SCAN_STEPS):
        for part in range(2):
            tab_ref[2 * k + part] = jnp.where(row >= step, pows[L * step][part], 0.0)
    for part in range(2):
        plane = jnp.zeros((SUBLANES, SLAB_STATE), F32)
        for t in range(SUBLANES):
            plane = jnp.where(row == t, pows[L * (t + 1)][part], plane)
        tab_ref[2 * len(SCAN_STEPS) + part] = plane

    shape = (LANES, SLAB_STATE)
    row_group = lax.broadcasted_iota(jnp.int32, shape, 0) >> (SSM_GROUP_CH.bit_length() - 1)
    col_group = lax.broadcasted_iota(jnp.int32, shape, 1) >> (SSM_STATE.bit_length() - 1)
    same_group = row_group == col_group
    tile_p = (lax.broadcasted_iota(jnp.int32, (SSM_STATE, SLAB_STATE), 0)
              == (lax.broadcasted_iota(jnp.int32, (SSM_STATE, SLAB_STATE), 1) & (SSM_STATE - 1))
              ).astype(BF16)

    def split(a):
        head = a.astype(BF16)
        return head, (a - head.astype(F32)).astype(BF16)

    def block_diag(a):
        head, rest = split(a)
        return jnp.where(same_group, _dot(head, tile_p) + _dot(rest, tile_p), 0.0)

    bt_re, bt_im = block_diag(bt_ref[0]), block_diag(bt_ref[1])
    ct_re, ct_im = block_diag(ct_ref[0]), block_diag(ct_ref[1])
    ct_re_parts, ct_im_parts = split(ct_re), split(ct_im)

    def dot_nt(a, b_parts):
        nt = (((1,), (1,)), ((), ()))
        a_head, a_rest = split(a)
        b_head, b_rest = b_parts
        return (lax.dot_general(a_head, b_head, nt, preferred_element_type=F32)
                + lax.dot_general(a_rest, b_head, nt, preferred_element_type=F32)
                + lax.dot_general(a_head, b_rest, nt, preferred_element_type=F32))

    taps = []
    for m in range(L):
        s = L - 1 - m
        w_re, w_im = _cmul(bt_re, bt_im, *_cmul(*coef, *pows[m]))
        wb_ref[s * LANES:(s + 1) * LANES, 0:SLAB_STATE] = w_re.astype(BF16)
        wb_ref[s * LANES:(s + 1) * LANES, SLAB_STATE:] = w_im.astype(BF16)
        taps.append(dot_nt(w_re, ct_re_parts) - dot_nt(w_im, ct_im_parts))
    for s in range(L):
        for t in range(L):
            blk = taps[t - s] if t >= s else jnp.zeros((LANES, LANES), F32)
            toep_ref[s * LANES:(s + 1) * LANES, t * LANES:(t + 1) * LANES] = blk.astype(BF16)
    for t in range(L):
        o_re, o_im = _cmul(ct_re, ct_im, *pows[t + 1])
        wc_ref[0:SLAB_STATE, t * LANES:(t + 1) * LANES] = o_re.T.astype(BF16)
        wc_ref[SLAB_STATE:, t * LANES:(t + 1) * LANES] = (-o_im).T.astype(BF16)


def _ssm_operators(lambda_re, lambda_im, log_dt, b_re, b_im, c_re, c_im):
    L = SSM_CHUNK
    lre, lim = lambda_re.astype(F32), lambda_im.astype(F32)
    dt = jnp.exp(log_dt.astype(F32))[:, None]
    mag = jnp.exp(lre * dt)
    bar_re, bar_im = mag * jnp.cos(lim * dt), mag * jnp.sin(lim * dt)
    den = lre * lre + lim * lim
    coef_re = ((bar_re - 1.0) * lre + bar_im * lim) / den
    coef_im = (bar_im * lre - (bar_re - 1.0) * lim) / den

    def slab(a):
        return a.reshape(N_SLABS, SLAB_STATE)

    zero = jnp.zeros((N_SLABS, SUBLANES - 2, SLAB_STATE), F32)
    base = jnp.stack([
        jnp.concatenate([slab(bar_re)[:, None], slab(coef_re)[:, None], zero], axis=1),
        jnp.concatenate([slab(bar_im)[:, None], slab(coef_im)[:, None], zero], axis=1)], axis=1)

    def compact(re, im):
        return jnp.stack([re, im], axis=1).astype(F32).reshape(
            N_SLABS, GROUPS_PER_SLAB, 2, SSM_GROUP_CH, SSM_STATE).transpose(0, 2, 1, 3, 4).reshape(
            N_SLABS, 2, LANES, SSM_STATE)

    bt = compact(b_re.transpose(0, 2, 1), b_im.transpose(0, 2, 1))
    ct = compact(c_re, c_im)

    n_planes = 2 * (len(SCAN_STEPS) + 1)
    return pl.pallas_call(
        _ssm_ops_kernel,
        grid=(N_SLABS,),
        in_specs=[
            pl.BlockSpec((None, 2, LANES, SSM_STATE), lambda q: (q, 0, 0, 0)),
            pl.BlockSpec((None, 2, LANES, SSM_STATE), lambda q: (q, 0, 0, 0)),
            pl.BlockSpec((None, 2, SUBLANES, SLAB_STATE), lambda q: (q, 0, 0, 0)),
        ],
        out_specs=[
            pl.BlockSpec((None, L * LANES, L * LANES), lambda q: (q, 0, 0)),
            pl.BlockSpec((None, L * LANES, 2 * SLAB_STATE), lambda q: (q, 0, 0)),
            pl.BlockSpec((None, 2 * SLAB_STATE, L * LANES), lambda q: (q, 0, 0)),
            pl.BlockSpec((None, n_planes, SUBLANES, SLAB_STATE), lambda q: (q, 0, 0, 0)),
        ],
        out_shape=[
            jax.ShapeDtypeStruct((N_SLABS, L * LANES, L * LANES), BF16),
            jax.ShapeDtypeStruct((N_SLABS, L * LANES, 2 * SLAB_STATE), BF16),
            jax.ShapeDtypeStruct((N_SLABS, 2 * SLAB_STATE, L * LANES), BF16),
            jax.ShapeDtypeStruct((N_SLABS, n_planes, SUBLANES, SLAB_STATE), F32),
        ],
        compiler_params=pltpu.CompilerParams(
            dimension_semantics=("arbitrary",), vmem_limit_bytes=VMEM_LIMIT_BYTES),
        name="ssm_ops",
    )(bt, ct, base)


def kernel(x, cond, w_mod, b_mod, ffn1_norm, ffn1_w_gate, ffn1_w_up, ffn1_w_down, mix_norm, w_in, conv_w, lambda_re, lambda_im, log_dt, ssm_b_re, ssm_b_im, ssm_c_re, ssm_c_im, ssm_d, glu_w, glu_b, out_norm_conv, out_norm_ssm, w_out, ffn2_norm, ffn2_w_gate, ffn2_w_up, ffn2_w_down, final_norm):
    depth = w_mod.shape[0]
    for l in range(depth):
        mod = _modulation(cond, w_mod[l], b_mod[l])
        later = (w_in[l], glu_w[l], w_out[l], ffn2_w_gate[l], ffn2_w_up[l], ffn2_w_down[l])
        x, (w_in_b, glu_w_b, w_out_b, wg2, wu2, wd2) = _ffn(
            x, mod, ffn1_norm[l], ffn1_w_gate[l].astype(BF16), ffn1_w_up[l].astype(BF16),
            ffn1_w_down[l].astype(BF16), final_norm, mod_base=0, final_norm=False, cast=later)
        toep, wb, wc, tab = _ssm_operators(lambda_re[l], lambda_im[l], log_dt[l], ssm_b_re[l],
                                           ssm_b_im[l], ssm_c_re[l], ssm_c_im[l])
        x = _mixer(x, mod, mix_norm[l], w_in_b, conv_w[l], toep, wb, wc, tab,
                   ssm_d[l], glu_w_b, glu_b[l], out_norm_conv[l], out_norm_ssm[l], w_out_b)
        x, _ = _ffn(x, mod, ffn2_norm[l], wg2, wu2, wd2, final_norm, mod_base=6,
                    final_norm=(l == depth - 1))
    return x
```

```python
import functools
import math

import jax
import jax.numpy as jnp
from jax import lax
from jax.experimental import pallas as pl
from jax.experimental.pallas import tpu as pltpu

D_MODEL = 1024
D_CONV = 512
D_SSM = 512
CONV_WIDTH = 3
SSM_GROUP_CH = 16
SSM_GROUPS = 32
SSM_STATE = 64
D_FF = 2816
N_MOD = 9
EPS = 1e-6

SUBLANES = 8
LANES = 128
VMEM_LIMIT_BYTES = 56 * 1024 * 1024

MOD_COLS = 3072
MOD_ROWS = SUBLANES
FFN_ROWS = 1024
FFN_SUB = 256
FFN_COLS = 256
FFN_LOAD_CHUNKS = 8
MIX_ROWS = 1024
MIX_SUB = 256
SSM_CHUNK = 4
GROUPS_PER_SLAB = LANES // SSM_GROUP_CH
N_SLABS = D_SSM // LANES
SLAB_STATE = GROUPS_PER_SLAB * SSM_STATE
SCAN_STEPS = (1, 2, 4)

BF16 = jnp.bfloat16
F32 = jnp.float32


def _dot(a, b):
    return jnp.dot(a, b, preferred_element_type=F32)


def _sigmoid(x):
    return 1.0 / (1.0 + jnp.exp(-x))


def _rms_norm(x, gain):
    ms = jnp.mean(x * x, axis=-1, keepdims=True)
    return x * lax.rsqrt(ms + EPS) * gain


def _gelu_tanh(x):
    c = math.sqrt(2.0 / math.pi)
    return x * (0.5 * (1.0 + jnp.tanh(c * (x + 0.044715 * (x * x * x)))))


def _mod_kernel(c_ref, w_ref, b_ref, o_ref):
    c = c_ref[...]
    c = c * _sigmoid(c)
    pad = jnp.zeros((MOD_ROWS - c.shape[0], D_MODEL), F32)
    c = jnp.concatenate([c, pad], axis=0)
    res = _dot(c.astype(BF16), w_ref[...].astype(BF16)) + b_ref[...]
    for i in range(MOD_COLS // D_MODEL):
        o_ref[i] = res[:, i * D_MODEL:(i + 1) * D_MODEL]


def _modulation(cond, w_mod, b_mod):
    bsz = cond.shape[0]
    assert bsz <= MOD_ROWS
    n = w_mod.shape[1]
    tn = MOD_COLS
    return pl.pallas_call(
        _mod_kernel,
        grid=(n // tn,),
        in_specs=[
            pl.BlockSpec((bsz, D_MODEL), lambda j: (0, 0)),
            pl.BlockSpec((D_MODEL, tn), lambda j: (0, j)),
            pl.BlockSpec((1, tn), lambda j: (0, j)),
        ],
        out_specs=pl.BlockSpec((tn // D_MODEL, MOD_ROWS, D_MODEL), lambda j: (j, 0, 0)),
        out_shape=jax.ShapeDtypeStruct((N_MOD, MOD_ROWS, D_MODEL), F32),
        compiler_params=pltpu.CompilerParams(
            dimension_semantics=("arbitrary",), vmem_limit_bytes=VMEM_LIMIT_BYTES),
        name="adaln_mod",
    )(cond, w_mod, b_mod.reshape(1, n))


def _load_as_bf16(src_hbm, dst, stage, sem):
    rows = stage.shape[1]
    n = src_hbm.shape[0] // rows

    def copy(c):
        return pltpu.make_async_copy(src_hbm.at[pl.ds(c * rows, rows), :], stage.at[c % 2],
                                     sem.at[c % 2])

    copy(0).start()
    for c in range(n):
        if c + 1 < n:
            copy(c + 1).start()
        copy(c).wait()
        dst[c * rows:(c + 1) * rows, :] = stage[c % 2].astype(BF16)


def _ffn_kernel(*refs, mod_base, final_norm, n_cast, f32_weights):
    x_ref, mod_ref, gain_ref, wg_ref, wu_ref, wd_ref, fgain_ref = refs[:7]
    cast_in = refs[7:7 + n_cast]
    o_ref = refs[7 + n_cast]
    cast_out = refs[8 + n_cast:8 + 2 * n_cast]
    act_ref = refs[8 + 2 * n_cast]
    if f32_weights:
        wg_hbm, wu_hbm, wd_hbm = wg_ref, wu_ref, wd_ref
        wg_ref, wu_ref, wd_ref, stage_up, stage_down, sem_up, sem_down = refs[9 + 2 * n_cast:]

        @pl.when((pl.program_id(0) == 0) & (pl.program_id(1) == 0))
        def _():
            _load_as_bf16(wg_hbm, wg_ref, stage_up, sem_up)
            _load_as_bf16(wu_hbm, wu_ref, stage_up, sem_up)
            _load_as_bf16(wd_hbm, wd_ref, stage_down, sem_down)
    for src, dst in zip(cast_in, cast_out):
        dst[...] = src[...].astype(BF16)
    batch = pl.ds(pl.program_id(0), 1)
    shift = mod_ref[mod_base, batch, :]
    scale = mod_ref[mod_base + 1, batch, :]
    gate = mod_ref[mod_base + 2, batch, :]
    for r in range(FFN_ROWS // FFN_SUB):
        rows = slice(r * FFN_SUB, (r + 1) * FFN_SUB)
        x = x_ref[rows, :]
        h = (_rms_norm(x, gain_ref[...]) * (1.0 + scale) + shift).astype(BF16)
        for f in range(D_FF // FFN_COLS):
            cols = slice(f * FFN_COLS, (f + 1) * FFN_COLS)
            g = _dot(h, wg_ref[:, cols])
            u = _dot(h, wu_ref[:, cols])
            act_ref[rows, cols] = (g * _sigmoid(g) * u).astype(BF16)
        y = _dot(act_ref[rows, :], wd_ref[...])
        out = x + (0.5 * gate) * y
        if final_norm:
            out = _rms_norm(out, fgain_ref[...])
        o_ref[rows, :] = out


def _const_spec(shape):
    zeros = (0,) * len(shape)
    return pl.BlockSpec(shape, lambda *_: zeros, pipeline_mode=pl.Buffered(1))


def _ffn(x, mod, gain, w_gate, w_up, w_down, final_gain, *, mod_base, final_norm, cast=()):
    bsz, seq, _ = x.shape
    tm = FFN_ROWS
    n_t = seq // tm
    steps = bsz * n_t
    f32_weights = w_gate.dtype == F32
    assert w_up.dtype == w_gate.dtype and w_down.dtype == w_gate.dtype
    kernel = functools.partial(_ffn_kernel, mod_base=mod_base, final_norm=final_norm,
                               n_cast=len(cast), f32_weights=f32_weights)
    band_specs = [pl.BlockSpec((w.shape[0] // steps, w.shape[1]), lambda b, t: (b * n_t + t, 0))
                  for w in cast]
    scratch = [pltpu.VMEM((tm, D_FF), BF16)]
    if f32_weights:
        weight_specs = [pl.BlockSpec(memory_space=pl.ANY)] * 3
        scratch += [
            pltpu.VMEM((D_MODEL, D_FF), BF16), pltpu.VMEM((D_MODEL, D_FF), BF16),
            pltpu.VMEM((D_FF, D_MODEL), BF16),
            pltpu.VMEM((2, D_MODEL // FFN_LOAD_CHUNKS, D_FF), F32),
            pltpu.VMEM((2, D_FF // FFN_LOAD_CHUNKS, D_MODEL), F32),
            pltpu.SemaphoreType.DMA((2,)), pltpu.SemaphoreType.DMA((2,)),
        ]
    else:
        weight_specs = [_const_spec((D_MODEL, D_FF)), _const_spec((D_MODEL, D_FF)),
                        _const_spec((D_FF, D_MODEL))]
    outs = pl.pallas_call(
        kernel,
        grid=(bsz, n_t),
        in_specs=[
            pl.BlockSpec((None, tm, D_MODEL), lambda b, t: (b, t, 0)),
            _const_spec((N_MOD, MOD_ROWS, D_MODEL)),
            _const_spec((1, D_MODEL)),
            *weight_specs,
            _const_spec((1, D_MODEL)),
        ] + band_specs,
        out_specs=[pl.BlockSpec((None, tm, D_MODEL), lambda b, t: (b, t, 0))] + band_specs,
        out_shape=[jax.ShapeDtypeStruct(x.shape, F32)]
                  + [jax.ShapeDtypeStruct(w.shape, BF16) for w in cast],
        scratch_shapes=scratch,
        compiler_params=pltpu.CompilerParams(
            dimension_semantics=("arbitrary", "arbitrary"),
            vmem_limit_bytes=VMEM_LIMIT_BYTES),
        name="ffn_final" if final_norm else "ffn",
    )(x, mod, gain.reshape(1, D_MODEL), w_gate, w_up, w_down,
      final_gain.reshape(1, D_MODEL), *cast)
    return outs[0], tuple(outs[1:])


def _mixer_kernel(x_ref, mod_ref, gain_ref, win_ref, convw_ref, toep_ref, wb_ref, wc_ref,
                  tab_ref, dskip_ref, gluw_ref, glub_ref, gain_c_ref, gain_s_ref, wout_ref,
                  o_ref, zbuf, ubuf, ybuf, hbuf, carry, ymix):
    tm = MIX_ROWS
    chunk = SSM_CHUNK
    m_rows = tm // chunk
    halo = SUBLANES

    @pl.when(pl.program_id(1) == 0)
    def _():
        zbuf[0:halo, :] = jnp.zeros((halo, D_CONV), F32)
        ubuf[:, 0:halo, :] = jnp.zeros((N_SLABS, halo, LANES), F32)
        carry[...] = jnp.zeros(carry.shape, F32)

    x = x_ref[...]
    batch = pl.ds(pl.program_id(0), 1)
    shift = mod_ref[3, batch, :]
    scale = mod_ref[4, batch, :]
    gate = mod_ref[5, batch, :]
    h = (_rms_norm(x, gain_ref[...]) * (1.0 + scale) + shift).astype(BF16)


    u = _dot(h, win_ref[:, 3 * D_CONV:])
    for q in range(N_SLABS):
        ubuf[q, halo:halo + tm, :] = u[:, q * LANES:(q + 1) * LANES]
    u_curs, injs = [], []
    for q in range(N_SLABS):
        cur = [ubuf[q, pl.ds(halo + s, m_rows, stride=chunk), :] for s in range(chunk)]
        prv = [ubuf[q, pl.ds(halo - chunk + s, m_rows, stride=chunk), :] for s in range(chunk)]
        u_curs.append(jnp.concatenate(cur, axis=1).astype(BF16))
        u_prv = jnp.concatenate(prv, axis=1).astype(BF16)
        ubuf[q, 0:halo, :] = ubuf[q, tm:tm + halo, :]
        injs.append(_dot(u_prv, wb_ref[q]))

    conv_proj = []
    for q in range(N_SLABS):
        if q < 3:
            conv_proj.append(_dot(h, win_ref[:, q * D_CONV:(q + 1) * D_CONV]))
        inj = injs[q]
        for i in range(SLAB_STATE // LANES):
            lanes_re = slice(i * LANES, (i + 1) * LANES)
            lanes_im = slice(SLAB_STATE + i * LANES, SLAB_STATE + (i + 1) * LANES)
            re = inj[:, lanes_re].reshape(m_rows // SUBLANES, SUBLANES, LANES)
            im = inj[:, lanes_im].reshape(m_rows // SUBLANES, SUBLANES, LANES)
            for k, step in enumerate(SCAN_STEPS):
                a_re = tab_ref[q, 2 * k, :, lanes_re]
                a_im = tab_ref[q, 2 * k + 1, :, lanes_re]
                s_re = pltpu.roll(re, step, axis=1)
                s_im = pltpu.roll(im, step, axis=1)
                re, im = re + a_re * s_re - a_im * s_im, im + a_re * s_im + a_im * s_re
            p_re = tab_ref[q, 2 * len(SCAN_STEPS), :, lanes_re]
            p_im = tab_ref[q, 2 * len(SCAN_STEPS) + 1, :, lanes_re]
            c_re = carry[q:q + 1, lanes_re]
            c_im = carry[q:q + 1, lanes_im]
            for r in range(m_rows // SUBLANES):
                rows = slice(r * SUBLANES, (r + 1) * SUBLANES)
                b_re = jnp.broadcast_to(c_re, (SUBLANES, LANES))
                b_im = jnp.broadcast_to(c_im, (SUBLANES, LANES))
                f_re = re[r] + p_re * b_re - p_im * b_im
                f_im = im[r] + p_re * b_im + p_im * b_re
                hbuf[q, rows, lanes_re] = f_re
                hbuf[q, rows, lanes_im] = f_im
                c_re = f_re[SUBLANES - 1:SUBLANES, :]
                c_im = f_im[SUBLANES - 1:SUBLANES, :]
            carry[q:q + 1, lanes_re] = c_re
            carry[q:q + 1, lanes_im] = c_im

    for q in range(N_SLABS):
        y2 = _dot(u_curs[q], toep_ref[q]) + _dot(hbuf[q].astype(BF16), wc_ref[q])
        for s in range(chunk):
            ybuf[q, pl.ds(s, m_rows, stride=chunk), :] = y2[:, s * LANES:(s + 1) * LANES]
    b_gate, c_gate, v = conv_proj
    z = c_gate * v
    zbuf[halo:halo + tm, :] = z
    z1 = zbuf[halo - 1:halo - 1 + tm, :]
    z2 = zbuf[halo - 2:halo - 2 + tm, :]
    y_conv = b_gate * (convw_ref[0:1, :] * z2 + convw_ref[1:2, :] * z1 + convw_ref[2:3, :] * z)
    zbuf[0:halo, :] = zbuf[tm:tm + halo, :]
    ymix[:, 0:D_CONV] = _rms_norm(y_conv, gain_c_ref[...]).astype(BF16)

    for r in range(MIX_ROWS // MIX_SUB):
        rows = slice(r * MIX_SUB, (r + 1) * MIX_SUB)
        u_rows = slice(halo + r * MIX_SUB, halo + (r + 1) * MIX_SUB)
        y = jnp.concatenate([ybuf[q, rows, :] for q in range(N_SLABS)], axis=1)
        y = y + dskip_ref[...] * jnp.concatenate([ubuf[q, u_rows, :] for q in range(N_SLABS)], axis=1)
        y = _gelu_tanh(y)
        y = y * _sigmoid(_dot(y.astype(BF16), gluw_ref[...]) + glub_ref[...])
        ymix[rows, D_CONV:] = _rms_norm(y, gain_s_ref[...]).astype(BF16)
        o_ref[rows, :] = x_ref[rows, :] + gate * _dot(ymix[rows, :], wout_ref[...])


def _mixer(x, mod, gain, w_in, conv_w, toep, wb, wc, tab, d_skip, glu_w, glu_b,
           gain_c, gain_s, w_out):
    bsz, seq, _ = x.shape
    tm = MIX_ROWS
    m_rows = tm // SSM_CHUNK
    return pl.pallas_call(
        _mixer_kernel,
        grid=(bsz, seq // tm),
        in_specs=[
            pl.BlockSpec((None, tm, D_MODEL), lambda b, t: (b, t, 0)),
            _const_spec((N_MOD, MOD_ROWS, D_MODEL)),
            _const_spec((1, D_MODEL)),
            _const_spec(w_in.shape),
            _const_spec(conv_w.shape),
            _const_spec(toep.shape),
            _const_spec(wb.shape),
            _const_spec(wc.shape),
            _const_spec(tab.shape),
            _const_spec((1, D_SSM)),
            _const_spec(glu_w.shape),
            _const_spec((1, D_SSM)),
            _const_spec((1, D_CONV)),
            _const_spec((1, D_SSM)),
            _const_spec(w_out.shape),
        ],
        out_specs=pl.BlockSpec((None, tm, D_MODEL), lambda b, t: (b, t, 0)),
        out_shape=jax.ShapeDtypeStruct(x.shape, F32),
        scratch_shapes=[
            pltpu.VMEM((SUBLANES + tm, D_CONV), F32),
            pltpu.VMEM((N_SLABS, SUBLANES + tm, LANES), F32),
            pltpu.VMEM((N_SLABS, tm, LANES), F32),
            pltpu.VMEM((N_SLABS, m_rows, 2 * SLAB_STATE), F32),
            pltpu.VMEM((N_SLABS, 2 * SLAB_STATE), F32),
            pltpu.VMEM((tm, D_MODEL), BF16),
        ],
        compiler_params=pltpu.CompilerParams(
            dimension_semantics=("arbitrary", "arbitrary"),
            vmem_limit_bytes=VMEM_LIMIT_BYTES),
        name="mixer",
    )(x, mod, gain.reshape(1, D_MODEL), w_in, conv_w, toep, wb, wc, tab,
      d_skip.reshape(1, D_SSM), glu_w, glu_b.reshape(1, D_SSM),
      gain_c.reshape(1, D_CONV), gain_s.reshape(1, D_SSM), w_out)


def _cmul(ar, ai, br, bi):
    return ar * br - ai * bi, ar * bi + ai * br


def _ssm_ops_kernel(bt_ref, ct_ref, base_ref, toep_ref, wb_ref, wc_ref, tab_ref):
    L = SSM_CHUNK
    bar = base_ref[0, 0:1, :], base_ref[1, 0:1, :]
    coef = base_ref[0, 1:2, :], base_ref[1, 1:2, :]
    pows = [(jnp.ones((1, SLAB_STATE), F32), jnp.zeros((1, SLAB_STATE), F32))]
    for _ in range(SUBLANES * L):
        pows.append(_cmul(*pows[-1], *bar))

    row = lax.broadcasted_iota(jnp.int32, (SUBLANES, SLAB_STATE), 0)
    for k, step in enumerate(SCAN_STEPS):
        for part in range(2):
            tab_ref[2 * k + part] = jnp.where(row >= step, pows[L * step][part], 0.0)
    for part in range(2):
        plane = jnp.zeros((SUBLANES, SLAB_STATE), F32)
        for t in range(SUBLANES):
            plane = jnp.where(row == t, pows[L * (t + 1)][part], plane)
        tab_ref[2 * len(SCAN_STEPS) + part] = plane

    shape = (LANES, SLAB_STATE)
    row_group = lax.broadcasted_iota(jnp.int32, shape, 0) >> (SSM_GROUP_CH.bit_length() - 1)
    col_group = lax.broadcasted_iota(jnp.int32, shape, 1) >> (SSM_STATE.bit_length() - 1)
    same_group = row_group == col_group
    tile_p = (lax.broadcasted_iota(jnp.int32, (SSM_STATE, SLAB_STATE), 0)
              == (lax.broadcasted_iota(jnp.int32, (SSM_STATE, SLAB_STATE), 1) & (SSM_STATE - 1))
              ).astype(BF16)

    def split(a):
        head = a.astype(BF16)
        return head, (a - head.astype(F32)).astype(BF16)

    def block_diag(a):
        head, rest = split(a)
        return jnp.where(same_group, _dot(head, tile_p) + _dot(rest, tile_p), 0.0)

    bt_re, bt_im = block_diag(bt_ref[0]), block_diag(bt_ref[1])
    ct_re, ct_im = block_diag(ct_ref[0]), block_diag(ct_ref[1])
    ct_re_parts, ct_im_parts = split(ct_re), split(ct_im)

    def dot_nt(a, b_parts):
        nt = (((1,), (1,)), ((), ()))
        a_head, a_rest = split(a)
        b_head, b_rest = b_parts
        return (lax.dot_general(a_head, b_head, nt, preferred_element_type=F32)
                + lax.dot_general(a_rest, b_head, nt, preferred_element_type=F32)
                + lax.dot_general(a_head, b_rest, nt, preferred_element_type=F32))

    taps = []
    for m in range(L):
        s = L - 1 - m
        w_re, w_im = _cmul(bt_re, bt_im, *_cmul(*coef, *pows[m]))
        wb_ref[s * LANES:(s + 1) * LANES, 0:SLAB_STATE] = w_re.astype(BF16)
        wb_ref[s * LANES:(s + 1) * LANES, SLAB_STATE:] = w_im.astype(BF16)
        taps.append(dot_nt(w_re, ct_re_parts) - dot_nt(w_im, ct_im_parts))
    for s in range(L):
        for t in range(L):
            blk = taps[t - s] if t >= s else jnp.zeros((LANES, LANES), F32)
            toep_ref[s * LANES:(s + 1) * LANES, t * LANES:(t + 1) * LANES] = blk.astype(BF16)
    for t in range(L):
        o_re, o_im = _cmul(ct_re, ct_im, *pows[t + 1])
        wc_ref[0:SLAB_STATE, t * LANES:(t + 1) * LANES] = o_re.T.astype(BF16)
        wc_ref[SLAB_STATE:, t * LANES:(t + 1) * LANES] = (-o_im).T.astype(BF16)


def _ssm_operators(lambda_re, lambda_im, log_dt, b_re, b_im, c_re, c_im):
    L = SSM_CHUNK
    lre, lim = lambda_re.astype(F32), lambda_im.astype(F32)
    dt = jnp.exp(log_dt.astype(F32))[:, None]
    mag = jnp.exp(lre * dt)
    bar_re, bar_im = mag * jnp.cos(lim * dt), mag * jnp.sin(lim * dt)
    den = lre * lre + lim * lim
    coef_re = ((bar_re - 1.0) * lre + bar_im * lim) / den
    coef_im = (bar_im * lre - (bar_re - 1.0) * lim) / den

    def slab(a):
        return a.reshape(N_SLABS, SLAB_STATE)

    zero = jnp.zeros((N_SLABS, SUBLANES - 2, SLAB_STATE), F32)
    base = jnp.stack([
        jnp.concatenate([slab(bar_re)[:, None], slab(coef_re)[:, None], zero], axis=1),
        jnp.concatenate([slab(bar_im)[:, None], slab(coef_im)[:, None], zero], axis=1)], axis=1)

    def compact(re, im):
        return jnp.stack([re, im], axis=1).astype(F32).reshape(
            N_SLABS, GROUPS_PER_SLAB, 2, SSM_GROUP_CH, SSM_STATE).transpose(0, 2, 1, 3, 4).reshape(
            N_SLABS, 2, LANES, SSM_STATE)

    bt = compact(b_re.transpose(0, 2, 1), b_im.transpose(0, 2, 1))
    ct = compact(c_re, c_im)

    n_planes = 2 * (len(SCAN_STEPS) + 1)
    return pl.pallas_call(
        _ssm_ops_kernel,
        grid=(N_SLABS,),
        in_specs=[
            pl.BlockSpec((None, 2, LANES, SSM_STATE), lambda q: (q, 0, 0, 0)),
            pl.BlockSpec((None, 2, LANES, SSM_STATE), lambda q: (q, 0, 0, 0)),
            pl.BlockSpec((None, 2, SUBLANES, SLAB_STATE), lambda q: (q, 0, 0, 0)),
        ],
        out_specs=[
            pl.BlockSpec((None, L * LANES, L * LANES), lambda q: (q, 0, 0)),
            pl.BlockSpec((None, L * LANES, 2 * SLAB_STATE), lambda q: (q, 0, 0)),
            pl.BlockSpec((None, 2 * SLAB_STATE, L * LANES), lambda q: (q, 0, 0)),
            pl.BlockSpec((None, n_planes, SUBLANES, SLAB_STATE), lambda q: (q, 0, 0, 0)),
        ],
        out_shape=[
            jax.ShapeDtypeStruct((N_SLABS, L * LANES, L * LANES), BF16),
            jax.ShapeDtypeStruct((N_SLABS, L * LANES, 2 * SLAB_STATE), BF16),
            jax.ShapeDtypeStruct((N_SLABS, 2 * SLAB_STATE, L * LANES), BF16),
            jax.ShapeDtypeStruct((N_SLABS, n_planes, SUBLANES, SLAB_STATE), F32),
        ],
        compiler_params=pltpu.CompilerParams(
            dimension_semantics=("arbitrary",), vmem_limit_bytes=VMEM_LIMIT_BYTES),
        name="ssm_ops",
    )(bt, ct, base)


def kernel(x, cond, w_mod, b_mod, ffn1_norm, ffn1_w_gate, ffn1_w_up, ffn1_w_down, mix_norm, w_in, conv_w, lambda_re, lambda_im, log_dt, ssm_b_re, ssm_b_im, ssm_c_re, ssm_c_im, ssm_d, glu_w, glu_b, out_norm_conv, out_norm_ssm, w_out, ffn2_norm, ffn2_w_gate, ffn2_w_up, ffn2_w_down, final_norm):
    depth = w_mod.shape[0]
    for l in range(depth):
        mod = _modulation(cond, w_mod[l], b_mod[l])
        later = (w_in[l], glu_w[l], w_out[l], ffn2_w_gate[l], ffn2_w_up[l], ffn2_w_down[l])
        x, (w_in_b, glu_w_b, w_out_b, wg2, wu2, wd2) = _ffn(
            x, mod, ffn1_norm[l], ffn1_w_gate[l], ffn1_w_up[l], ffn1_w_down[l], final_norm,
            mod_base=0, final_norm=False, cast=later)
        toep, wb, wc, tab = _ssm_operators(lambda_re[l], lambda_im[l], log_dt[l], ssm_b_re[l],
                                           ssm_b_im[l], ssm_c_re[l], ssm_c_im[l])
        x = _mixer(x, mod, mix_norm[l], w_in_b, conv_w[l], toep, wb, wc, tab,
                   ssm_d[l], glu_w_b, glu_b[l], out_norm_conv[l], out_norm_ssm[l], w_out_b)
        x, _ = _ffn(x, mod, ffn2_norm[l], wg2, wu2, wd2, final_norm, mod_base=6,
                    final_norm=(l == depth - 1))
    return x
```

```python
import functools
import math

import jax
import jax.numpy as jnp
from jax import lax
from jax.experimental import pallas as pl
from jax.experimental.pallas import tpu as pltpu

D_MODEL = 1024
D_CONV = 512
D_SSM = 512
CONV_WIDTH = 3
SSM_GROUP_CH = 16
SSM_GROUPS = 32
SSM_STATE = 64
D_FF = 2816
N_MOD = 9
EPS = 1e-6

SUBLANES = 8
LANES = 128
VMEM_LIMIT_BYTES = 56 * 1024 * 1024

MOD_COLS = 3072
MOD_ROWS = SUBLANES
FFN_ROWS = 1024
FFN_SUB = 256
FFN_COLS = 256
FFN_LOAD_CHUNKS = 16
FFN_LOAD_DEPTH = 4
MIX_ROWS = 1024
MIX_SUB = 256
SSM_CHUNK = 4
GROUPS_PER_SLAB = LANES // SSM_GROUP_CH
N_SLABS = D_SSM // LANES
SLAB_STATE = GROUPS_PER_SLAB * SSM_STATE
SCAN_STEPS = (1, 2, 4)

BF16 = jnp.bfloat16
F32 = jnp.float32


def _dot(a, b):
    return jnp.dot(a, b, preferred_element_type=F32)


def _sigmoid(x):
    return 1.0 / (1.0 + jnp.exp(-x))


def _rms_norm(x, gain):
    ms = jnp.mean(x * x, axis=-1, keepdims=True)
    return x * lax.rsqrt(ms + EPS) * gain


def _gelu_tanh(x):
    c = math.sqrt(2.0 / math.pi)
    return x * (0.5 * (1.0 + jnp.tanh(c * (x + 0.044715 * (x * x * x)))))


def _mod_kernel(c_ref, w_ref, b_ref, o_ref):
    c = c_ref[...]
    c = c * _sigmoid(c)
    pad = jnp.zeros((MOD_ROWS - c.shape[0], D_MODEL), F32)
    c = jnp.concatenate([c, pad], axis=0)
    res = _dot(c.astype(BF16), w_ref[...].astype(BF16)) + b_ref[...]
    for i in range(MOD_COLS // D_MODEL):
        o_ref[i] = res[:, i * D_MODEL:(i + 1) * D_MODEL]


def _modulation(cond, w_mod, b_mod):
    bsz = cond.shape[0]
    assert bsz <= MOD_ROWS
    n = w_mod.shape[1]
    tn = MOD_COLS
    return pl.pallas_call(
        _mod_kernel,
        grid=(n // tn,),
        in_specs=[
            pl.BlockSpec((bsz, D_MODEL), lambda j: (0, 0)),
            pl.BlockSpec((D_MODEL, tn), lambda j: (0, j)),
            pl.BlockSpec((1, tn), lambda j: (0, j)),
        ],
        out_specs=pl.BlockSpec((tn // D_MODEL, MOD_ROWS, D_MODEL), lambda j: (j, 0, 0)),
        out_shape=jax.ShapeDtypeStruct((N_MOD, MOD_ROWS, D_MODEL), F32),
        compiler_params=pltpu.CompilerParams(
            dimension_semantics=("arbitrary",), vmem_limit_bytes=VMEM_LIMIT_BYTES),
        name="adaln_mod",
    )(cond, w_mod, b_mod.reshape(1, n))


def _load_as_bf16(pairs, stage, sem):
    depth, rows = stage.shape[0], stage.shape[1]
    chunks = [(src, dst, r) for src, dst in pairs for r in range(0, src.shape[0], rows)]

    def copy(c):
        src, _, r = chunks[c]
        return pltpu.make_async_copy(src.at[pl.ds(r, rows), :], stage.at[c % depth],
                                     sem.at[c % depth])

    for c in range(min(depth - 1, len(chunks))):
        copy(c).start()
    for c in range(len(chunks)):
        if c + depth - 1 < len(chunks):
            copy(c + depth - 1).start()
        copy(c).wait()
        _, dst, r = chunks[c]
        dst[r:r + rows, :] = stage[c % depth].astype(BF16)


def _ffn_kernel(*refs, mod_base, final_norm, n_cast, f32_weights):
    x_ref, mod_ref, gain_ref, wg_ref, wu_ref, wd_ref, fgain_ref = refs[:7]
    cast_in = refs[7:7 + n_cast]
    o_ref = refs[7 + n_cast]
    cast_out = refs[8 + n_cast:8 + 2 * n_cast]
    act_ref = refs[8 + 2 * n_cast]
    if f32_weights:
        wg_hbm, wu_hbm, wd_hbm = wg_ref, wu_ref, wd_ref
        wg_ref, wu_ref, wd_ref, stage_up, stage_down, sem_up, sem_down = refs[9 + 2 * n_cast:]

        @pl.when((pl.program_id(0) == 0) & (pl.program_id(1) == 0))
        def _():
            _load_as_bf16([(wg_hbm, wg_ref), (wu_hbm, wu_ref)], stage_up, sem_up)
            _load_as_bf16([(wd_hbm, wd_ref)], stage_down, sem_down)
    for src, dst in zip(cast_in, cast_out):
        dst[...] = src[...].astype(BF16)
    batch = pl.ds(pl.program_id(0), 1)
    shift = mod_ref[mod_base, batch, :]
    scale = mod_ref[mod_base + 1, batch, :]
    gate = mod_ref[mod_base + 2, batch, :]
    for r in range(FFN_ROWS // FFN_SUB):
        rows = slice(r * FFN_SUB, (r + 1) * FFN_SUB)
        x = x_ref[rows, :]
        h = (_rms_norm(x, gain_ref[...]) * (1.0 + scale) + shift).astype(BF16)
        for f in range(D_FF // FFN_COLS):
            cols = slice(f * FFN_COLS, (f + 1) * FFN_COLS)
            g = _dot(h, wg_ref[:, cols])
            u = _dot(h, wu_ref[:, cols])
            act_ref[rows, cols] = (g * _sigmoid(g) * u).astype(BF16)
        y = _dot(act_ref[rows, :], wd_ref[...])
        out = x + (0.5 * gate) * y
        if final_norm:
            out = _rms_norm(out, fgain_ref[...])
        o_ref[rows, :] = out


def _const_spec(shape):
    zeros = (0,) * len(shape)
    return pl.BlockSpec(shape, lambda *_: zeros, pipeline_mode=pl.Buffered(1))


def _ffn(x, mod, gain, w_gate, w_up, w_down, final_gain, *, mod_base, final_norm, cast=()):
    bsz, seq, _ = x.shape
    tm = FFN_ROWS
    n_t = seq // tm
    steps = bsz * n_t
    f32_weights = w_gate.dtype == F32
    assert w_up.dtype == w_gate.dtype and w_down.dtype == w_gate.dtype
    kernel = functools.partial(_ffn_kernel, mod_base=mod_base, final_norm=final_norm,
                               n_cast=len(cast), f32_weights=f32_weights)
    band_specs = [pl.BlockSpec((w.shape[0] // steps, w.shape[1]), lambda b, t: (b * n_t + t, 0))
                  for w in cast]
    scratch = [pltpu.VMEM((tm, D_FF), BF16)]
    if f32_weights:
        weight_specs = [pl.BlockSpec(memory_space=pl.ANY)] * 3
        scratch += [
            pltpu.VMEM((D_MODEL, D_FF), BF16), pltpu.VMEM((D_MODEL, D_FF), BF16),
            pltpu.VMEM((D_FF, D_MODEL), BF16),
            pltpu.VMEM((FFN_LOAD_DEPTH, D_MODEL // FFN_LOAD_CHUNKS, D_FF), F32),
            pltpu.VMEM((FFN_LOAD_DEPTH, D_FF // FFN_LOAD_CHUNKS, D_MODEL), F32),
            pltpu.SemaphoreType.DMA((FFN_LOAD_DEPTH,)), pltpu.SemaphoreType.DMA((FFN_LOAD_DEPTH,)),
        ]
    else:
        weight_specs = [_const_spec((D_MODEL, D_FF)), _const_spec((D_MODEL, D_FF)),
                        _const_spec((D_FF, D_MODEL))]
    outs = pl.pallas_call(
        kernel,
        grid=(bsz, n_t),
        in_specs=[
            pl.BlockSpec((None, tm, D_MODEL), lambda b, t: (b, t, 0)),
            _const_spec((N_MOD, MOD_ROWS, D_MODEL)),
            _const_spec((1, D_MODEL)),
            *weight_specs,
            _const_spec((1, D_MODEL)),
        ] + band_specs,
        out_specs=[pl.BlockSpec((None, tm, D_MODEL), lambda b, t: (b, t, 0))] + band_specs,
        out_shape=[jax.ShapeDtypeStruct(x.shape, F32)]
                  + [jax.ShapeDtypeStruct(w.shape, BF16) for w in cast],
        scratch_shapes=scratch,
        compiler_params=pltpu.CompilerParams(
            dimension_semantics=("arbitrary", "arbitrary"),
            vmem_limit_bytes=VMEM_LIMIT_BYTES),
        name="ffn_final" if final_norm else "ffn",
    )(x, mod, gain.reshape(1, D_MODEL), w_gate, w_up, w_down,
      final_gain.reshape(1, D_MODEL), *cast)
    return outs[0], tuple(outs[1:])


def _mixer_kernel(x_ref, mod_ref, gain_ref, win_ref, convw_ref, toep_ref, wb_ref, wc_ref,
                  tab_ref, dskip_ref, gluw_ref, glub_ref, gain_c_ref, gain_s_ref, wout_ref,
                  o_ref, zbuf, ubuf, ybuf, hbuf, carry, ymix):
    tm = MIX_ROWS
    chunk = SSM_CHUNK
    m_rows = tm // chunk
    halo = SUBLANES

    @pl.when(pl.program_id(1) == 0)
    def _():
        zbuf[0:halo, :] = jnp.zeros((halo, D_CONV), F32)
        ubuf[:, 0:halo, :] = jnp.zeros((N_SLABS, halo, LANES), F32)
        carry[...] = jnp.zeros(carry.shape, F32)

    x = x_ref[...]
    batch = pl.ds(pl.program_id(0), 1)
    shift = mod_ref[3, batch, :]
    scale = mod_ref[4, batch, :]
    gate = mod_ref[5, batch, :]
    h = (_rms_norm(x, gain_ref[...]) * (1.0 + scale) + shift).astype(BF16)


    u = _dot(h, win_ref[:, 3 * D_CONV:])
    for q in range(N_SLABS):
        ubuf[q, halo:halo + tm, :] = u[:, q * LANES:(q + 1) * LANES]
    u_curs, injs = [], []
    for q in range(N_SLABS):
        cur = [ubuf[q, pl.ds(halo + s, m_rows, stride=chunk), :] for s in range(chunk)]
        prv = [ubuf[q, pl.ds(halo - chunk + s, m_rows, stride=chunk), :] for s in range(chunk)]
        u_curs.append(jnp.concatenate(cur, axis=1).astype(BF16))
        u_prv = jnp.concatenate(prv, axis=1).astype(BF16)
        ubuf[q, 0:halo, :] = ubuf[q, tm:tm + halo, :]
        injs.append(_dot(u_prv, wb_ref[q]))

    conv_proj = []
    for q in range(N_SLABS):
        if q < 3:
            conv_proj.append(_dot(h, win_ref[:, q * D_CONV:(q + 1) * D_CONV]))
        inj = injs[q]
        for i in range(SLAB_STATE // LANES):
            lanes_re = slice(i * LANES, (i + 1) * LANES)
            lanes_im = slice(SLAB_STATE + i * LANES, SLAB_STATE + (i + 1) * LANES)
            re = inj[:, lanes_re].reshape(m_rows // SUBLANES, SUBLANES, LANES)
            im = inj[:, lanes_im].reshape(m_rows // SUBLANES, SUBLANES, LANES)
            for k, step in enumerate(SCAN_STEPS):
                a_re = tab_ref[q, 2 * k, :, lanes_re]
                a_im = tab_ref[q, 2 * k + 1, :, lanes_re]
                s_re = pltpu.roll(re, step, axis=1)
                s_im = pltpu.roll(im, step, axis=1)
                re, im = re + a_re * s_re - a_im * s_im, im + a_re * s_im + a_im * s_re
            p_re = tab_ref[q, 2 * len(SCAN_STEPS), :, lanes_re]
            p_im = tab_ref[q, 2 * len(SCAN_STEPS) + 1, :, lanes_re]
            c_re = carry[q:q + 1, lanes_re]
            c_im = carry[q:q + 1, lanes_im]
            for r in range(m_rows // SUBLANES):
                rows = slice(r * SUBLANES, (r + 1) * SUBLANES)
                b_re = jnp.broadcast_to(c_re, (SUBLANES, LANES))
                b_im = jnp.broadcast_to(c_im, (SUBLANES, LANES))
                f_re = re[r] + p_re * b_re - p_im * b_im
                f_im = im[r] + p_re * b_im + p_im * b_re
                hbuf[q, rows, lanes_re] = f_re
                hbuf[q, rows, lanes_im] = f_im
                c_re = f_re[SUBLANES - 1:SUBLANES, :]
                c_im = f_im[SUBLANES - 1:SUBLANES, :]
            carry[q:q + 1, lanes_re] = c_re
            carry[q:q + 1, lanes_im] = c_im

    for q in range(N_SLABS):
        y2 = _dot(u_curs[q], toep_ref[q]) + _dot(hbuf[q].astype(BF16), wc_ref[q])
        for s in range(chunk):
            ybuf[q, pl.ds(s, m_rows, stride=chunk), :] = y2[:, s * LANES:(s + 1) * LANES]
    b_gate, c_gate, v = conv_proj
    z = c_gate * v
    zbuf[halo:halo + tm, :] = z
    z1 = zbuf[halo - 1:halo - 1 + tm, :]
    z2 = zbuf[halo - 2:halo - 2 + tm, :]
    y_conv = b_gate * (convw_ref[0:1, :] * z2 + convw_ref[1:2, :] * z1 + convw_ref[2:3, :] * z)
    zbuf[0:halo, :] = zbuf[tm:tm + halo, :]
    ymix[:, 0:D_CONV] = _rms_norm(y_conv, gain_c_ref[...]).astype(BF16)

    for r in range(MIX_ROWS // MIX_SUB):
        rows = slice(r * MIX_SUB, (r + 1) * MIX_SUB)
        u_rows = slice(halo + r * MIX_SUB, halo + (r + 1) * MIX_SUB)
        y = jnp.concatenate([ybuf[q, rows, :] for q in range(N_SLABS)], axis=1)
        y = y + dskip_ref[...] * jnp.concatenate([ubuf[q, u_rows, :] for q in range(N_SLABS)], axis=1)
        y = _gelu_tanh(y)
        y = y * _sigmoid(_dot(y.astype(BF16), gluw_ref[...]) + glub_ref[...])
        ymix[rows, D_CONV:] = _rms_norm(y, gain_s_ref[...]).astype(BF16)
        o_ref[rows, :] = x_ref[rows, :] + gate * _dot(ymix[rows, :], wout_ref[...])


def _mixer(x, mod, gain, w_in, conv_w, toep, wb, wc, tab, d_skip, glu_w, glu_b,
           gain_c, gain_s, w_out):
    bsz, seq, _ = x.shape
    tm = MIX_ROWS
    m_rows = tm // SSM_CHUNK
    return pl.pallas_call(
        _mixer_kernel,
        grid=(bsz, seq // tm),
        in_specs=[
            pl.BlockSpec((None, tm, D_MODEL), lambda b, t: (b, t, 0)),
            _const_spec((N_MOD, MOD_ROWS, D_MODEL)),
            _const_spec((1, D_MODEL)),
            _const_spec(w_in.shape),
            _const_spec(conv_w.shape),
            _const_spec(toep.shape),
            _const_spec(wb.shape),
            _const_spec(wc.shape),
            _const_spec(tab.shape),
            _const_spec((1, D_SSM)),
            _const_spec(glu_w.shape),
            _const_spec((1, D_SSM)),
            _const_spec((1, D_CONV)),
            _const_spec((1, D_SSM)),
            _const_spec(w_out.shape),
        ],
        out_specs=pl.BlockSpec((None, tm, D_MODEL), lambda b, t: (b, t, 0)),
        out_shape=jax.ShapeDtypeStruct(x.shape, F32),
        scratch_shapes=[
            pltpu.VMEM((SUBLANES + tm, D_CONV), F32),
            pltpu.VMEM((N_SLABS, SUBLANES + tm, LANES), F32),
            pltpu.VMEM((N_SLABS, tm, LANES), F32),
            pltpu.VMEM((N_SLABS, m_rows, 2 * SLAB_STATE), F32),
            pltpu.VMEM((N_SLABS, 2 * SLAB_STATE), F32),
            pltpu.VMEM((tm, D_MODEL), BF16),
        ],
        compiler_params=pltpu.CompilerParams(
            dimension_semantics=("arbitrary", "arbitrary"),
            vmem_limit_bytes=VMEM_LIMIT_BYTES),
        name="mixer",
    )(x, mod, gain.reshape(1, D_MODEL), w_in, conv_w, toep, wb, wc, tab,
      d_skip.reshape(1, D_SSM), glu_w, glu_b.reshape(1, D_SSM),
      gain_c.reshape(1, D_CONV), gain_s.reshape(1, D_SSM), w_out)


def _cmul(ar, ai, br, bi):
    return ar * br - ai * bi, ar * bi + ai * br


def _ssm_ops_kernel(bt_ref, ct_ref, base_ref, toep_ref, wb_ref, wc_ref, tab_ref):
    L = SSM_CHUNK
    bar = base_ref[0, 0:1, :], base_ref[1, 0:1, :]
    coef = base_ref[0, 1:2, :], base_ref[1, 1:2, :]
    pows = [(jnp.ones((1, SLAB_STATE), F32), jnp.zeros((1, SLAB_STATE), F32))]
    for _ in range(SUBLANES * L):
        pows.append(_cmul(*pows[-1], *bar))

    row = lax.broadcasted_iota(jnp.int32, (SUBLANES, SLAB_STATE), 0)
    for k, step in enumerate(SCAN_STEPS):
        for part in range(2):
            tab_ref[2 * k + part] = jnp.where(row >= step, pows[L * step][part], 0.0)
    for part in range(2):
        plane = jnp.zeros((SUBLANES, SLAB_STATE), F32)
        for t in range(SUBLANES):
            plane = jnp.where(row == t, pows[L * (t + 1)][part], plane)
        tab_ref[2 * len(SCAN_STEPS) + part] = plane

    shape = (LANES, SLAB_STATE)
    row_group = lax.broadcasted_iota(jnp.int32, shape, 0) >> (SSM_GROUP_CH.bit_length() - 1)
    col_group = lax.broadcasted_iota(jnp.int32, shape, 1) >> (SSM_STATE.bit_length() - 1)
    same_group = row_group == col_group
    tile_p = (lax.broadcasted_iota(jnp.int32, (SSM_STATE, SLAB_STATE), 0)
              == (lax.broadcasted_iota(jnp.int32, (SSM_STATE, SLAB_STATE), 1) & (SSM_STATE - 1))
              ).astype(BF16)

    def split(a):
        head = a.astype(BF16)
        return head, (a - head.astype(F32)).astype(BF16)

    def block_diag(a):
        head, rest = split(a)
        return jnp.where(same_group, _dot(head, tile_p) + _dot(rest, tile_p), 0.0)

    bt_re, bt_im = block_diag(bt_ref[0]), block_diag(bt_ref[1])
    ct_re, ct_im = block_diag(ct_ref[0]), block_diag(ct_ref[1])
    ct_re_parts, ct_im_parts = split(ct_re), split(ct_im)

    def dot_nt(a, b_parts):
        nt = (((1,), (1,)), ((), ()))
        a_head, a_rest = split(a)
        b_head, b_rest = b_parts
        return (lax.dot_general(a_head, b_head, nt, preferred_element_type=F32)
                + lax.dot_general(a_rest, b_head, nt, preferred_element_type=F32)
                + lax.dot_general(a_head, b_rest, nt, preferred_element_type=F32))

    taps = []
    for m in range(L):
        s = L - 1 - m
        w_re, w_im = _cmul(bt_re, bt_im, *_cmul(*coef, *pows[m]))
        wb_ref[s * LANES:(s + 1) * LANES, 0:SLAB_STATE] = w_re.astype(BF16)
        wb_ref[s * LANES:(s + 1) * LANES, SLAB_STATE:] = w_im.astype(BF16)
        taps.append(dot_nt(w_re, ct_re_parts) - dot_nt(w_im, ct_im_parts))
    for s in range(L):
        for t in range(L):
            blk = taps[t - s] if t >= s else jnp.zeros((LANES, LANES), F32)
            toep_ref[s * LANES:(s + 1) * LANES, t * LANES:(t + 1) * LANES] = blk.astype(BF16)
    for t in range(L):
        o_re, o_im = _cmul(ct_re, ct_im, *pows[t + 1])
        wc_ref[0:SLAB_STATE, t * LANES:(t + 1) * LANES] = o_re.T.astype(BF16)
        wc_ref[SLAB_STATE:, t * LANES:(t + 1) * LANES] = (-o_im).T.astype(BF16)


def _ssm_operators(lambda_re, lambda_im, log_dt, b_re, b_im, c_re, c_im):
    L = SSM_CHUNK
    lre, lim = lambda_re.astype(F32), lambda_im.astype(F32)
    dt = jnp.exp(log_dt.astype(F32))[:, None]
    mag = jnp.exp(lre * dt)
    bar_re, bar_im = mag * jnp.cos(lim * dt), mag * jnp.sin(lim * dt)
    den = lre * lre + lim * lim
    coef_re = ((bar_re - 1.0) * lre + bar_im * lim) / den
    coef_im = (bar_im * lre - (bar_re - 1.0) * lim) / den

    def slab(a):
        return a.reshape(N_SLABS, SLAB_STATE)

    zero = jnp.zeros((N_SLABS, SUBLANES - 2, SLAB_STATE), F32)
    base = jnp.stack([
        jnp.concatenate([slab(bar_re)[:, None], slab(coef_re)[:, None], zero], axis=1),
        jnp.concatenate([slab(bar_im)[:, None], slab(coef_im)[:, None], zero], axis=1)], axis=1)

    def compact(re, im):
        return jnp.stack([re, im], axis=1).astype(F32).reshape(
            N_SLABS, GROUPS_PER_SLAB, 2, SSM_GROUP_CH, SSM_STATE).transpose(0, 2, 1, 3, 4).reshape(
            N_SLABS, 2, LANES, SSM_STATE)

    bt = compact(b_re.transpose(0, 2, 1), b_im.transpose(0, 2, 1))
    ct = compact(c_re, c_im)

    n_planes = 2 * (len(SCAN_STEPS) + 1)
    return pl.pallas_call(
        _ssm_ops_kernel,
        grid=(N_SLABS,),
        in_specs=[
            pl.BlockSpec((None, 2, LANES, SSM_STATE), lambda q: (q, 0, 0, 0)),
            pl.BlockSpec((None, 2, LANES, SSM_STATE), lambda q: (q, 0, 0, 0)),
            pl.BlockSpec((None, 2, SUBLANES, SLAB_STATE), lambda q: (q, 0, 0, 0)),
        ],
        out_specs=[
            pl.BlockSpec((None, L * LANES, L * LANES), lambda q: (q, 0, 0)),
            pl.BlockSpec((None, L * LANES, 2 * SLAB_STATE), lambda q: (q, 0, 0)),
            pl.BlockSpec((None, 2 * SLAB_STATE, L * LANES), lambda q: (q, 0, 0)),
            pl.BlockSpec((None, n_planes, SUBLANES, SLAB_STATE), lambda q: (q, 0, 0, 0)),
        ],
        out_shape=[
            jax.ShapeDtypeStruct((N_SLABS, L * LANES, L * LANES), BF16),
            jax.ShapeDtypeStruct((N_SLABS, L * LANES, 2 * SLAB_STATE), BF16),
            jax.ShapeDtypeStruct((N_SLABS, 2 * SLAB_STATE, L * LANES), BF16),
            jax.ShapeDtypeStruct((N_SLABS, n_planes, SUBLANES, SLAB_STATE), F32),
        ],
        compiler_params=pltpu.CompilerParams(
            dimension_semantics=("arbitrary",), vmem_limit_bytes=VMEM_LIMIT_BYTES),
        name="ssm_ops",
    )(bt, ct, base)


def kernel(x, cond, w_mod, b_mod, ffn1_norm, ffn1_w_gate, ffn1_w_up, ffn1_w_down, mix_norm, w_in, conv_w, lambda_re, lambda_im, log_dt, ssm_b_re, ssm_b_im, ssm_c_re, ssm_c_im, ssm_d, glu_w, glu_b, out_norm_conv, out_norm_ssm, w_out, ffn2_norm, ffn2_w_gate, ffn2_w_up, ffn2_w_down, final_norm):
    depth = w_mod.shape[0]
    for l in range(depth):
        mod = _modulation(cond, w_mod[l], b_mod[l])
        later = (w_in[l], glu_w[l], w_out[l], ffn2_w_gate[l], ffn2_w_up[l], ffn2_w_down[l])
        x, (w_in_b, glu_w_b, w_out_b, wg2, wu2, wd2) = _ffn(
            x, mod, ffn1_norm[l], ffn1_w_gate[l], ffn1_w_up[l], ffn1_w_down[l], final_norm,
            mod_base=0, final_norm=False, cast=later)
        toep, wb, wc, tab = _ssm_operators(lambda_re[l], lambda_im[l], log_dt[l], ssm_b_re[l],
                                           ssm_b_im[l], ssm_c_re[l], ssm_c_im[l])
        x = _mixer(x, mod, mix_norm[l], w_in_b, conv_w[l], toep, wb, wc, tab,
                   ssm_d[l], glu_w_b, glu_b[l], out_norm_conv[l], out_norm_ssm[l], w_out_b)
        x, _ = _ffn(x, mod, ffn2_norm[l], wg2, wu2, wd2, final_norm, mod_base=6,
                    final_norm=(l == depth - 1))
    return x
```

```python
import functools
import math

import jax
import jax.numpy as jnp
from jax import lax
from jax.experimental import pallas as pl
from jax.experimental.pallas import tpu as pltpu

D_MODEL = 1024
D_CONV = 512
D_SSM = 512
CONV_WIDTH = 3
SSM_GROUP_CH = 16
SSM_GROUPS = 32
SSM_STATE = 64
D_FF = 2816
N_MOD = 9
EPS = 1e-6

SUBLANES = 8
LANES = 128
VMEM_LIMIT_BYTES = 56 * 1024 * 1024

MOD_COLS = 3072
MOD_ROWS = SUBLANES
FFN_ROWS = 1024
FFN_SUB = 256
FFN_COLS = 256
FFN_LOAD_CHUNKS = 16
FFN_LOAD_DEPTH = 4
MIX_ROWS = 1024
MIX_SUB = 256
SSM_CHUNK = 4
GROUPS_PER_SLAB = LANES // SSM_GROUP_CH
N_SLABS = D_SSM // LANES
SLAB_STATE = GROUPS_PER_SLAB * SSM_STATE
SCAN_STEPS = (1, 2, 4)

BF16 = jnp.bfloat16
F32 = jnp.float32


def _dot(a, b):
    return jnp.dot(a, b, preferred_element_type=F32)


def _sigmoid(x):
    return 1.0 / (1.0 + jnp.exp(-x))


def _rms_norm(x, gain):
    ms = jnp.mean(x * x, axis=-1, keepdims=True)
    return x * lax.rsqrt(ms + EPS) * gain


def _gelu_tanh(x):
    c = math.sqrt(2.0 / math.pi)
    return x * (0.5 * (1.0 + jnp.tanh(c * (x + 0.044715 * (x * x * x)))))


def _mod_kernel(c_ref, w_ref, b_ref, o_ref):
    c = c_ref[...]
    c = c * _sigmoid(c)
    pad = jnp.zeros((MOD_ROWS - c.shape[0], D_MODEL), F32)
    c = jnp.concatenate([c, pad], axis=0)
    res = _dot(c.astype(BF16), w_ref[...].astype(BF16)) + b_ref[...]
    for i in range(MOD_COLS // D_MODEL):
        o_ref[i] = res[:, i * D_MODEL:(i + 1) * D_MODEL]


def _modulation(cond, w_mod, b_mod):
    bsz = cond.shape[0]
    assert bsz <= MOD_ROWS
    n = w_mod.shape[1]
    tn = MOD_COLS
    return pl.pallas_call(
        _mod_kernel,
        grid=(n // tn,),
        in_specs=[
            pl.BlockSpec((bsz, D_MODEL), lambda j: (0, 0)),
            pl.BlockSpec((D_MODEL, tn), lambda j: (0, j)),
            pl.BlockSpec((1, tn), lambda j: (0, j)),
        ],
        out_specs=pl.BlockSpec((tn // D_MODEL, MOD_ROWS, D_MODEL), lambda j: (j, 0, 0)),
        out_shape=jax.ShapeDtypeStruct((N_MOD, MOD_ROWS, D_MODEL), F32),
        compiler_params=pltpu.CompilerParams(
            dimension_semantics=("arbitrary",), vmem_limit_bytes=VMEM_LIMIT_BYTES),
        name="adaln_mod",
    )(cond, w_mod, b_mod.reshape(1, n))


def _load_as_bf16(pairs, stage, sem):
    depth, rows = stage.shape[0], stage.shape[1]
    chunks = [(src, dst, r) for src, dst in pairs for r in range(0, src.shape[0], rows)]

    def copy(c):
        src, _, r = chunks[c]
        return pltpu.make_async_copy(src.at[pl.ds(r, rows), :], stage.at[c % depth],
                                     sem.at[c % depth])

    for c in range(min(depth - 1, len(chunks))):
        copy(c).start(priority=c % 2)
    for c in range(len(chunks)):
        if c + depth - 1 < len(chunks):
            copy(c + depth - 1).start(priority=(c + depth - 1) % 2)
        copy(c).wait()
        _, dst, r = chunks[c]
        dst[r:r + rows, :] = stage[c % depth].astype(BF16)


def _ffn_kernel(*refs, mod_base, final_norm, n_cast, f32_weights):
    x_ref, mod_ref, gain_ref, wg_ref, wu_ref, wd_ref, fgain_ref = refs[:7]
    cast_in = refs[7:7 + n_cast]
    o_ref = refs[7 + n_cast]
    cast_out = refs[8 + n_cast:8 + 2 * n_cast]
    act_ref = refs[8 + 2 * n_cast]
    if f32_weights:
        wg_hbm, wu_hbm, wd_hbm = wg_ref, wu_ref, wd_ref
        wg_ref, wu_ref, wd_ref, stage_up, stage_down, sem_up, sem_down = refs[9 + 2 * n_cast:]

        @pl.when((pl.program_id(0) == 0) & (pl.program_id(1) == 0))
        def _():
            _load_as_bf16([(wg_hbm, wg_ref), (wu_hbm, wu_ref)], stage_up, sem_up)
            _load_as_bf16([(wd_hbm, wd_ref)], stage_down, sem_down)
    for src, dst in zip(cast_in, cast_out):
        dst[...] = src[...].astype(BF16)
    batch = pl.ds(pl.program_id(0), 1)
    shift = mod_ref[mod_base, batch, :]
    scale = mod_ref[mod_base + 1, batch, :]
    gate = mod_ref[mod_base + 2, batch, :]
    for r in range(FFN_ROWS // FFN_SUB):
        rows = slice(r * FFN_SUB, (r + 1) * FFN_SUB)
        x = x_ref[rows, :]
        h = (_rms_norm(x, gain_ref[...]) * (1.0 + scale) + shift).astype(BF16)
        for f in range(D_FF // FFN_COLS):
            cols = slice(f * FFN_COLS, (f + 1) * FFN_COLS)
            g = _dot(h, wg_ref[:, cols])
            u = _dot(h, wu_ref[:, cols])
            act_ref[rows, cols] = (g * _sigmoid(g) * u).astype(BF16)
        y = _dot(act_ref[rows, :], wd_ref[...])
        out = x + (0.5 * gate) * y
        if final_norm:
            out = _rms_norm(out, fgain_ref[...])
        o_ref[rows, :] = out


def _const_spec(shape):
    zeros = (0,) * len(shape)
    return pl.BlockSpec(shape, lambda *_: zeros, pipeline_mode=pl.Buffered(1))


def _ffn(x, mod, gain, w_gate, w_up, w_down, final_gain, *, mod_base, final_norm, cast=()):
    bsz, seq, _ = x.shape
    tm = FFN_ROWS
    n_t = seq // tm
    steps = bsz * n_t
    f32_weights = w_gate.dtype == F32
    assert w_up.dtype == w_gate.dtype and w_down.dtype == w_gate.dtype
    kernel = functools.partial(_ffn_kernel, mod_base=mod_base, final_norm=final_norm,
                               n_cast=len(cast), f32_weights=f32_weights)
    band_specs = [pl.BlockSpec((w.shape[0] // steps, w.shape[1]), lambda b, t: (b * n_t + t, 0))
                  for w in cast]
    scratch = [pltpu.VMEM((tm, D_FF), BF16)]
    if f32_weights:
        weight_specs = [pl.BlockSpec(memory_space=pl.ANY)] * 3
        scratch += [
            pltpu.VMEM((D_MODEL, D_FF), BF16), pltpu.VMEM((D_MODEL, D_FF), BF16),
            pltpu.VMEM((D_FF, D_MODEL), BF16),
            pltpu.VMEM((FFN_LOAD_DEPTH, D_MODEL // FFN_LOAD_CHUNKS, D_FF), F32),
            pltpu.VMEM((FFN_LOAD_DEPTH, D_FF // FFN_LOAD_CHUNKS, D_MODEL), F32),
            pltpu.SemaphoreType.DMA((FFN_LOAD_DEPTH,)), pltpu.SemaphoreType.DMA((FFN_LOAD_DEPTH,)),
        ]
    else:
        weight_specs = [_const_spec((D_MODEL, D_FF)), _const_spec((D_MODEL, D_FF)),
                        _const_spec((D_FF, D_MODEL))]
    outs = pl.pallas_call(
        kernel,
        grid=(bsz, n_t),
        in_specs=[
            pl.BlockSpec((None, tm, D_MODEL), lambda b, t: (b, t, 0)),
            _const_spec((N_MOD, MOD_ROWS, D_MODEL)),
            _const_spec((1, D_MODEL)),
            *weight_specs,
            _const_spec((1, D_MODEL)),
        ] + band_specs,
        out_specs=[pl.BlockSpec((None, tm, D_MODEL), lambda b, t: (b, t, 0))] + band_specs,
        out_shape=[jax.ShapeDtypeStruct(x.shape, F32)]
                  + [jax.ShapeDtypeStruct(w.shape, BF16) for w in cast],
        scratch_shapes=scratch,
        compiler_params=pltpu.CompilerParams(
            dimension_semantics=("arbitrary", "arbitrary"),
            vmem_limit_bytes=VMEM_LIMIT_BYTES),
        name="ffn_final" if final_norm else "ffn",
    )(x, mod, gain.reshape(1, D_MODEL), w_gate, w_up, w_down,
      final_gain.reshape(1, D_MODEL), *cast)
    return outs[0], tuple(outs[1:])


def _mixer_kernel(x_ref, mod_ref, gain_ref, win_ref, convw_ref, toep_ref, wb_ref, wc_ref,
                  tab_ref, dskip_ref, gluw_ref, glub_ref, gain_c_ref, gain_s_ref, wout_ref,
                  o_ref, zbuf, ubuf, ybuf, hbuf, carry, ymix):
    tm = MIX_ROWS
    chunk = SSM_CHUNK
    m_rows = tm // chunk
    halo = SUBLANES

    @pl.when(pl.program_id(1) == 0)
    def _():
        zbuf[0:halo, :] = jnp.zeros((halo, D_CONV), F32)
        ubuf[:, 0:halo, :] = jnp.zeros((N_SLABS, halo, LANES), F32)
        carry[...] = jnp.zeros(carry.shape, F32)

    x = x_ref[...]
    batch = pl.ds(pl.program_id(0), 1)
    shift = mod_ref[3, batch, :]
    scale = mod_ref[4, batch, :]
    gate = mod_ref[5, batch, :]
    h = (_rms_norm(x, gain_ref[...]) * (1.0 + scale) + shift).astype(BF16)


    u = _dot(h, win_ref[:, 3 * D_CONV:])
    for q in range(N_SLABS):
        ubuf[q, halo:halo + tm, :] = u[:, q * LANES:(q + 1) * LANES]
    u_curs, injs = [], []
    for q in range(N_SLABS):
        cur = [ubuf[q, pl.ds(halo + s, m_rows, stride=chunk), :] for s in range(chunk)]
        prv = [ubuf[q, pl.ds(halo - chunk + s, m_rows, stride=chunk), :] for s in range(chunk)]
        u_curs.append(jnp.concatenate(cur, axis=1).astype(BF16))
        u_prv = jnp.concatenate(prv, axis=1).astype(BF16)
        ubuf[q, 0:halo, :] = ubuf[q, tm:tm + halo, :]
        injs.append(_dot(u_prv, wb_ref[q]))

    conv_proj = []
    for q in range(N_SLABS):
        if q < 3:
            conv_proj.append(_dot(h, win_ref[:, q * D_CONV:(q + 1) * D_CONV]))
        inj = injs[q]
        for i in range(SLAB_STATE // LANES):
            lanes_re = slice(i * LANES, (i + 1) * LANES)
            lanes_im = slice(SLAB_STATE + i * LANES, SLAB_STATE + (i + 1) * LANES)
            re = inj[:, lanes_re].reshape(m_rows // SUBLANES, SUBLANES, LANES)
            im = inj[:, lanes_im].reshape(m_rows // SUBLANES, SUBLANES, LANES)
            for k, step in enumerate(SCAN_STEPS):
                a_re = tab_ref[q, 2 * k, :, lanes_re]
                a_im = tab_ref[q, 2 * k + 1, :, lanes_re]
                s_re = pltpu.roll(re, step, axis=1)
                s_im = pltpu.roll(im, step, axis=1)
                re, im = re + a_re * s_re - a_im * s_im, im + a_re * s_im + a_im * s_re
            p_re = tab_ref[q, 2 * len(SCAN_STEPS), :, lanes_re]
            p_im = tab_ref[q, 2 * len(SCAN_STEPS) + 1, :, lanes_re]
            c_re = carry[q:q + 1, lanes_re]
            c_im = carry[q:q + 1, lanes_im]
            for r in range(m_rows // SUBLANES):
                rows = slice(r * SUBLANES, (r + 1) * SUBLANES)
                b_re = jnp.broadcast_to(c_re, (SUBLANES, LANES))
                b_im = jnp.broadcast_to(c_im, (SUBLANES, LANES))
                f_re = re[r] + p_re * b_re - p_im * b_im
                f_im = im[r] + p_re * b_im + p_im * b_re
                hbuf[q, rows, lanes_re] = f_re
                hbuf[q, rows, lanes_im] = f_im
                c_re = f_re[SUBLANES - 1:SUBLANES, :]
                c_im = f_im[SUBLANES - 1:SUBLANES, :]
            carry[q:q + 1, lanes_re] = c_re
            carry[q:q + 1, lanes_im] = c_im

    for q in range(N_SLABS):
        y2 = _dot(u_curs[q], toep_ref[q]) + _dot(hbuf[q].astype(BF16), wc_ref[q])
        for s in range(chunk):
            ybuf[q, pl.ds(s, m_rows, stride=chunk), :] = y2[:, s * LANES:(s + 1) * LANES]
    b_gate, c_gate, v = conv_proj
    z = c_gate * v
    zbuf[halo:halo + tm, :] = z
    z1 = zbuf[halo - 1:halo - 1 + tm, :]
    z2 = zbuf[halo - 2:halo - 2 + tm, :]
    y_conv = b_gate * (convw_ref[0:1, :] * z2 + convw_ref[1:2, :] * z1 + convw_ref[2:3, :] * z)
    zbuf[0:halo, :] = zbuf[tm:tm + halo, :]
    ymix[:, 0:D_CONV] = _rms_norm(y_conv, gain_c_ref[...]).astype(BF16)

    for r in range(MIX_ROWS // MIX_SUB):
        rows = slice(r * MIX_SUB, (r + 1) * MIX_SUB)
        u_rows = slice(halo + r * MIX_SUB, halo + (r + 1) * MIX_SUB)
        y = jnp.concatenate([ybuf[q, rows, :] for q in range(N_SLABS)], axis=1)
        y = y + dskip_ref[...] * jnp.concatenate([ubuf[q, u_rows, :] for q in range(N_SLABS)], axis=1)
        y = _gelu_tanh(y)
        y = y * _sigmoid(_dot(y.astype(BF16), gluw_ref[...]) + glub_ref[...])
        ymix[rows, D_CONV:] = _rms_norm(y, gain_s_ref[...]).astype(BF16)
        o_ref[rows, :] = x_ref[rows, :] + gate * _dot(ymix[rows, :], wout_ref[...])


def _mixer(x, mod, gain, w_in, conv_w, toep, wb, wc, tab, d_skip, glu_w, glu_b,
           gain_c, gain_s, w_out):
    bsz, seq, _ = x.shape
    tm = MIX_ROWS
    m_rows = tm // SSM_CHUNK
    return pl.pallas_call(
        _mixer_kernel,
        grid=(bsz, seq // tm),
        in_specs=[
            pl.BlockSpec((None, tm, D_MODEL), lambda b, t: (b, t, 0)),
            _const_spec((N_MOD, MOD_ROWS, D_MODEL)),
            _const_spec((1, D_MODEL)),
            _const_spec(w_in.shape),
            _const_spec(conv_w.shape),
            _const_spec(toep.shape),
            _const_spec(wb.shape),
            _const_spec(wc.shape),
            _const_spec(tab.shape),
            _const_spec((1, D_SSM)),
            _const_spec(glu_w.shape),
            _const_spec((1, D_SSM)),
            _const_spec((1, D_CONV)),
            _const_spec((1, D_SSM)),
            _const_spec(w_out.shape),
        ],
        out_specs=pl.BlockSpec((None, tm, D_MODEL), lambda b, t: (b, t, 0)),
        out_shape=jax.ShapeDtypeStruct(x.shape, F32),
        scratch_shapes=[
            pltpu.VMEM((SUBLANES + tm, D_CONV), F32),
            pltpu.VMEM((N_SLABS, SUBLANES + tm, LANES), F32),
            pltpu.VMEM((N_SLABS, tm, LANES), F32),
            pltpu.VMEM((N_SLABS, m_rows, 2 * SLAB_STATE), F32),
            pltpu.VMEM((N_SLABS, 2 * SLAB_STATE), F32),
            pltpu.VMEM((tm, D_MODEL), BF16),
        ],
        compiler_params=pltpu.CompilerParams(
            dimension_semantics=("arbitrary", "arbitrary"),
            vmem_limit_bytes=VMEM_LIMIT_BYTES),
        name="mixer",
    )(x, mod, gain.reshape(1, D_MODEL), w_in, conv_w, toep, wb, wc, tab,
      d_skip.reshape(1, D_SSM), glu_w, glu_b.reshape(1, D_SSM),
      gain_c.reshape(1, D_CONV), gain_s.reshape(1, D_SSM), w_out)


def _cmul(ar, ai, br, bi):
    return ar * br - ai * bi, ar * bi + ai * br


def _ssm_ops_kernel(bt_ref, ct_ref, base_ref, toep_ref, wb_ref, wc_ref, tab_ref):
    L = SSM_CHUNK
    bar = base_ref[0, 0:1, :], base_ref[1, 0:1, :]
    coef = base_ref[0, 1:2, :], base_ref[1, 1:2, :]
    pows = [(jnp.ones((1, SLAB_STATE), F32), jnp.zeros((1, SLAB_STATE), F32))]
    for _ in range(SUBLANES * L):
        pows.append(_cmul(*pows[-1], *bar))

    row = lax.broadcasted_iota(jnp.int32, (SUBLANES, SLAB_STATE), 0)
    for k, step in enumerate(SCAN_STEPS):
        for part in range(2):
            tab_ref[2 * k + part] = jnp.where(row >= step, pows[L * step][part], 0.0)
    for part in range(2):
        plane = jnp.zeros((SUBLANES, SLAB_STATE), F32)
        for t in range(SUBLANES):
            plane = jnp.where(row == t, pows[L * (t + 1)][part], plane)
        tab_ref[2 * len(SCAN_STEPS) + part] = plane

    shape = (LANES, SLAB_STATE)
    row_group = lax.broadcasted_iota(jnp.int32, shape, 0) >> (SSM_GROUP_CH.bit_length() - 1)
    col_group = lax.broadcasted_iota(jnp.int32, shape, 1) >> (SSM_STATE.bit_length() - 1)
    same_group = row_group == col_group
    tile_p = (lax.broadcasted_iota(jnp.int32, (SSM_STATE, SLAB_STATE), 0)
              == (lax.broadcasted_iota(jnp.int32, (SSM_STATE, SLAB_STATE), 1) & (SSM_STATE - 1))
              ).astype(BF16)

    def split(a):
        head = a.astype(BF16)
        return head, (a - head.astype(F32)).astype(BF16)

    def block_diag(a):
        head, rest = split(a)
        return jnp.where(same_group, _dot(head, tile_p) + _dot(rest, tile_p), 0.0)

    bt_re, bt_im = block_diag(bt_ref[0]), block_diag(bt_ref[1])
    ct_re, ct_im = block_diag(ct_ref[0]), block_diag(ct_ref[1])
    ct_re_parts, ct_im_parts = split(ct_re), split(ct_im)

    def dot_nt(a, b_parts):
        nt = (((1,), (1,)), ((), ()))
        a_head, a_rest = split(a)
        b_head, b_rest = b_parts
        return (lax.dot_general(a_head, b_head, nt, preferred_element_type=F32)
                + lax.dot_general(a_rest, b_head, nt, preferred_element_type=F32)
                + lax.dot_general(a_head, b_rest, nt, preferred_element_type=F32))

    taps = []
    for m in range(L):
        s = L - 1 - m
        w_re, w_im = _cmul(bt_re, bt_im, *_cmul(*coef, *pows[m]))
        wb_ref[s * LANES:(s + 1) * LANES, 0:SLAB_STATE] = w_re.astype(BF16)
        wb_ref[s * LANES:(s + 1) * LANES, SLAB_STATE:] = w_im.astype(BF16)
        taps.append(dot_nt(w_re, ct_re_parts) - dot_nt(w_im, ct_im_parts))
    for s in range(L):
        for t in range(L):
            blk = taps[t - s] if t >= s else jnp.zeros((LANES, LANES), F32)
            toep_ref[s * LANES:(s + 1) * LANES, t * LANES:(t + 1) * LANES] = blk.astype(BF16)
    for t in range(L):
        o_re, o_im = _cmul(ct_re, ct_im, *pows[t + 1])
        wc_ref[0:SLAB_STATE, t * LANES:(t + 1) * LANES] = o_re.T.astype(BF16)
        wc_ref[SLAB_STATE:, t * LANES:(t + 1) * LANES] = (-o_im).T.astype(BF16)


def _ssm_operators(lambda_re, lambda_im, log_dt, b_re, b_im, c_re, c_im):
    L = SSM_CHUNK
    lre, lim = lambda_re.astype(F32), lambda_im.astype(F32)
    dt = jnp.exp(log_dt.astype(F32))[:, None]
    mag = jnp.exp(lre * dt)
    bar_re, bar_im = mag * jnp.cos(lim * dt), mag * jnp.sin(lim * dt)
    den = lre * lre + lim * lim
    coef_re = ((bar_re - 1.0) * lre + bar_im * lim) / den
    coef_im = (bar_im * lre - (bar_re - 1.0) * lim) / den

    def slab(a):
        return a.reshape(N_SLABS, SLAB_STATE)

    zero = jnp.zeros((N_SLABS, SUBLANES - 2, SLAB_STATE), F32)
    base = jnp.stack([
        jnp.concatenate([slab(bar_re)[:, None], slab(coef_re)[:, None], zero], axis=1),
        jnp.concatenate([slab(bar_im)[:, None], slab(coef_im)[:, None], zero], axis=1)], axis=1)

    def compact(re, im):
        return jnp.stack([re, im], axis=1).astype(F32).reshape(
            N_SLABS, GROUPS_PER_SLAB, 2, SSM_GROUP_CH, SSM_STATE).transpose(0, 2, 1, 3, 4).reshape(
            N_SLABS, 2, LANES, SSM_STATE)

    bt = compact(b_re.transpose(0, 2, 1), b_im.transpose(0, 2, 1))
    ct = compact(c_re, c_im)

    n_planes = 2 * (len(SCAN_STEPS) + 1)
    return pl.pallas_call(
        _ssm_ops_kernel,
        grid=(N_SLABS,),
        in_specs=[
            pl.BlockSpec((None, 2, LANES, SSM_STATE), lambda q: (q, 0, 0, 0)),
            pl.BlockSpec((None, 2, LANES, SSM_STATE), lambda q: (q, 0, 0, 0)),
            pl.BlockSpec((None, 2, SUBLANES, SLAB_STATE), lambda q: (q, 0, 0, 0)),
        ],
        out_specs=[
            pl.BlockSpec((None, L * LANES, L * LANES), lambda q: (q, 0, 0)),
            pl.BlockSpec((None, L * LANES, 2 * SLAB_STATE), lambda q: (q, 0, 0)),
            pl.BlockSpec((None, 2 * SLAB_STATE, L * LANES), lambda q: (q, 0, 0)),
            pl.BlockSpec((None, n_planes, SUBLANES, SLAB_STATE), lambda q: (q, 0, 0, 0)),
        ],
        out_shape=[
            jax.ShapeDtypeStruct((N_SLABS, L * LANES, L * LANES), BF16),
            jax.ShapeDtypeStruct((N_SLABS, L * LANES, 2 * SLAB_STATE), BF16),
            jax.ShapeDtypeStruct((N_SLABS, 2 * SLAB_STATE, L * LANES), BF16),
            jax.ShapeDtypeStruct((N_SLABS, n_planes, SUBLANES, SLAB_STATE), F32),
        ],
        compiler_params=pltpu.CompilerParams(
            dimension_semantics=("arbitrary",), vmem_limit_bytes=VMEM_LIMIT_BYTES),
        name="ssm_ops",
    )(bt, ct, base)


def kernel(x, cond, w_mod, b_mod, ffn1_norm, ffn1_w_gate, ffn1_w_up, ffn1_w_down, mix_norm, w_in, conv_w, lambda_re, lambda_im, log_dt, ssm_b_re, ssm_b_im, ssm_c_re, ssm_c_im, ssm_d, glu_w, glu_b, out_norm_conv, out_norm_ssm, w_out, ffn2_norm, ffn2_w_gate, ffn2_w_up, ffn2_w_down, final_norm):
    depth = w_mod.shape[0]
    for l in range(depth):
        mod = _modulation(cond, w_mod[l], b_mod[l])
        later = (w_in[l], glu_w[l], w_out[l], ffn2_w_gate[l], ffn2_w_up[l], ffn2_w_down[l])
        x, (w_in_b, glu_w_b, w_out_b, wg2, wu2, wd2) = _ffn(
            x, mod, ffn1_norm[l], ffn1_w_gate[l], ffn1_w_up[l], ffn1_w_down[l], final_norm,
            mod_base=0, final_norm=False, cast=later)
        toep, wb, wc, tab = _ssm_operators(lambda_re[l], lambda_im[l], log_dt[l], ssm_b_re[l],
                                           ssm_b_im[l], ssm_c_re[l], ssm_c_im[l])
        x = _mixer(x, mod, mix_norm[l], w_in_b, conv_w[l], toep, wb, wc, tab,
                   ssm_d[l], glu_w_b, glu_b[l], out_norm_conv[l], out_norm_ssm[l], w_out_b)
        x, _ = _ffn(x, mod, ffn2_norm[l], wg2, wu2, wd2, final_norm, mod_base=6,
                    final_norm=(l == depth - 1))
    return x
```

```python
import functools
import math

import jax
import jax.numpy as jnp
from jax import lax
from jax.experimental import pallas as pl
from jax.experimental.pallas import tpu as pltpu

D_MODEL = 1024
D_CONV = 512
D_SSM = 512
CONV_WIDTH = 3
SSM_GROUP_CH = 16
SSM_GROUPS = 32
SSM_STATE = 64
D_FF = 2816
N_MOD = 9
EPS = 1e-6

SUBLANES = 8
LANES = 128
VMEM_LIMIT_BYTES = 56 * 1024 * 1024

MOD_COLS = 3072
MOD_ROWS = SUBLANES
FFN_ROWS = 1024
FFN_SUB = 256
FFN_COLS = 256
FFN_LOAD_DEPTH = 3
MIX_ROWS = 1024
MIX_SUB = 256
SSM_CHUNK = 4
GROUPS_PER_SLAB = LANES // SSM_GROUP_CH
N_SLABS = D_SSM // LANES
SLAB_STATE = GROUPS_PER_SLAB * SSM_STATE
SCAN_STEPS = (1, 2, 4)

BF16 = jnp.bfloat16
F32 = jnp.float32


def _dot(a, b):
    return jnp.dot(a, b, preferred_element_type=F32)


def _sigmoid(x):
    return 1.0 / (1.0 + jnp.exp(-x))


def _rms_norm(x, gain):
    ms = jnp.mean(x * x, axis=-1, keepdims=True)
    return x * lax.rsqrt(ms + EPS) * gain


def _gelu_tanh(x):
    c = math.sqrt(2.0 / math.pi)
    return x * (0.5 * (1.0 + jnp.tanh(c * (x + 0.044715 * (x * x * x)))))


def _mod_kernel(c_ref, w_ref, b_ref, o_ref):
    c = c_ref[...]
    c = c * _sigmoid(c)
    pad = jnp.zeros((MOD_ROWS - c.shape[0], D_MODEL), F32)
    c = jnp.concatenate([c, pad], axis=0)
    res = _dot(c.astype(BF16), w_ref[...].astype(BF16)) + b_ref[...]
    for i in range(MOD_COLS // D_MODEL):
        o_ref[i] = res[:, i * D_MODEL:(i + 1) * D_MODEL]


def _modulation(cond, w_mod, b_mod):
    bsz = cond.shape[0]
    assert bsz <= MOD_ROWS
    n = w_mod.shape[1]
    tn = MOD_COLS
    return pl.pallas_call(
        _mod_kernel,
        grid=(n // tn,),
        in_specs=[
            pl.BlockSpec((bsz, D_MODEL), lambda j: (0, 0)),
            pl.BlockSpec((D_MODEL, tn), lambda j: (0, j)),
            pl.BlockSpec((1, tn), lambda j: (0, j)),
        ],
        out_specs=pl.BlockSpec((tn // D_MODEL, MOD_ROWS, D_MODEL), lambda j: (j, 0, 0)),
        out_shape=jax.ShapeDtypeStruct((N_MOD, MOD_ROWS, D_MODEL), F32),
        compiler_params=pltpu.CompilerParams(
            dimension_semantics=("arbitrary",), vmem_limit_bytes=VMEM_LIMIT_BYTES),
        name="adaln_mod",
    )(cond, w_mod, b_mod.reshape(1, n))


def _load_as_bf16(pairs, stage, sem):
    depth, cols = stage.shape[0], stage.shape[2]
    chunks = [(src, dst, c) for src, dst in pairs for c in range(0, src.shape[1], cols)]

    def copy(i):
        src, _, c = chunks[i]
        return pltpu.make_async_copy(src.at[:, pl.ds(c, cols)], stage.at[i % depth],
                                     sem.at[i % depth])

    for i in range(min(depth - 1, len(chunks))):
        copy(i).start()
    for i in range(len(chunks)):
        if i + depth - 1 < len(chunks):
            copy(i + depth - 1).start()
        copy(i).wait()
        _, dst, c = chunks[i]
        dst[:, c:c + cols] = stage[i % depth].astype(BF16)


def _ffn_kernel(*refs, mod_base, final_norm, n_cast, f32_weights):
    x_ref, mod_ref, gain_ref, wg_ref, wu_ref, wd_ref, fgain_ref = refs[:7]
    cast_in = refs[7:7 + n_cast]
    o_ref = refs[7 + n_cast]
    cast_out = refs[8 + n_cast:8 + 2 * n_cast]
    act_ref = refs[8 + 2 * n_cast]
    if f32_weights:
        wg_hbm, wu_hbm, wd_hbm = wg_ref, wu_ref, wd_ref
        wg_ref, wu_ref, wd_ref, stage_up, stage_down, sem_up, sem_down = refs[9 + 2 * n_cast:]

        @pl.when((pl.program_id(0) == 0) & (pl.program_id(1) == 0))
        def _():
            _load_as_bf16([(wg_hbm, wg_ref), (wu_hbm, wu_ref)], stage_up, sem_up)
            _load_as_bf16([(wd_hbm, wd_ref)], stage_down, sem_down)
    for src, dst in zip(cast_in, cast_out):
        dst[...] = src[...].astype(BF16)
    batch = pl.ds(pl.program_id(0), 1)
    shift = mod_ref[mod_base, batch, :]
    scale = mod_ref[mod_base + 1, batch, :]
    gate = mod_ref[mod_base + 2, batch, :]
    for r in range(FFN_ROWS // FFN_SUB):
        rows = slice(r * FFN_SUB, (r + 1) * FFN_SUB)
        x = x_ref[rows, :]
        h = (_rms_norm(x, gain_ref[...]) * (1.0 + scale) + shift).astype(BF16)
        for f in range(D_FF // FFN_COLS):
            cols = slice(f * FFN_COLS, (f + 1) * FFN_COLS)
            g = _dot(h, wg_ref[:, cols])
            u = _dot(h, wu_ref[:, cols])
            act_ref[rows, cols] = (g * _sigmoid(g) * u).astype(BF16)
        y = _dot(act_ref[rows, :], wd_ref[...])
        out = x + (0.5 * gate) * y
        if final_norm:
            out = _rms_norm(out, fgain_ref[...])
        o_ref[rows, :] = out


def _const_spec(shape):
    zeros = (0,) * len(shape)
    return pl.BlockSpec(shape, lambda *_: zeros, pipeline_mode=pl.Buffered(1))


def _ffn(x, mod, gain, w_gate, w_up, w_down, final_gain, *, mod_base, final_norm, cast=()):
    bsz, seq, _ = x.shape
    tm = FFN_ROWS
    n_t = seq // tm
    steps = bsz * n_t
    f32_weights = w_gate.dtype == F32
    assert w_up.dtype == w_gate.dtype and w_down.dtype == w_gate.dtype
    kernel = functools.partial(_ffn_kernel, mod_base=mod_base, final_norm=final_norm,
                               n_cast=len(cast), f32_weights=f32_weights)
    band_specs = [pl.BlockSpec((w.shape[0] // steps, w.shape[1]), lambda b, t: (b * n_t + t, 0))
                  for w in cast]
    scratch = [pltpu.VMEM((tm, D_FF), BF16)]
    if f32_weights:
        weight_specs = [pl.BlockSpec(memory_space=pl.ANY)] * 3
        scratch += [
            pltpu.VMEM((D_MODEL, D_FF), BF16), pltpu.VMEM((D_MODEL, D_FF), BF16),
            pltpu.VMEM((D_FF, D_MODEL), BF16),
            pltpu.VMEM((FFN_LOAD_DEPTH, D_MODEL, FFN_COLS), F32),
            pltpu.VMEM((FFN_LOAD_DEPTH, D_FF, LANES), F32),
            pltpu.SemaphoreType.DMA((FFN_LOAD_DEPTH,)), pltpu.SemaphoreType.DMA((FFN_LOAD_DEPTH,)),
        ]
    else:
        weight_specs = [_const_spec((D_MODEL, D_FF)), _const_spec((D_MODEL, D_FF)),
                        _const_spec((D_FF, D_MODEL))]
    outs = pl.pallas_call(
        kernel,
        grid=(bsz, n_t),
        in_specs=[
            pl.BlockSpec((None, tm, D_MODEL), lambda b, t: (b, t, 0)),
            _const_spec((N_MOD, MOD_ROWS, D_MODEL)),
            _const_spec((1, D_MODEL)),
            *weight_specs,
            _const_spec((1, D_MODEL)),
        ] + band_specs,
        out_specs=[pl.BlockSpec((None, tm, D_MODEL), lambda b, t: (b, t, 0))] + band_specs,
        out_shape=[jax.ShapeDtypeStruct(x.shape, F32)]
                  + [jax.ShapeDtypeStruct(w.shape, BF16) for w in cast],
        scratch_shapes=scratch,
        compiler_params=pltpu.CompilerParams(
            dimension_semantics=("arbitrary", "arbitrary"),
            vmem_limit_bytes=VMEM_LIMIT_BYTES),
        name="ffn_final" if final_norm else "ffn",
    )(x, mod, gain.reshape(1, D_MODEL), w_gate, w_up, w_down,
      final_gain.reshape(1, D_MODEL), *cast)
    return outs[0], tuple(outs[1:])


def _mixer_kernel(x_ref, mod_ref, gain_ref, win_ref, convw_ref, toep_ref, wb_ref, wc_ref,
                  tab_ref, dskip_ref, gluw_ref, glub_ref, gain_c_ref, gain_s_ref, wout_ref,
                  o_ref, zbuf, ubuf, ybuf, hbuf, carry, ymix):
    tm = MIX_ROWS
    chunk = SSM_CHUNK
    m_rows = tm // chunk
    halo = SUBLANES

    @pl.when(pl.program_id(1) == 0)
    def _():
        zbuf[0:halo, :] = jnp.zeros((halo, D_CONV), F32)
        ubuf[:, 0:halo, :] = jnp.zeros((N_SLABS, halo, LANES), F32)
        carry[...] = jnp.zeros(carry.shape, F32)

    x = x_ref[...]
    batch = pl.ds(pl.program_id(0), 1)
    shift = mod_ref[3, batch, :]
    scale = mod_ref[4, batch, :]
    gate = mod_ref[5, batch, :]
    h = (_rms_norm(x, gain_ref[...]) * (1.0 + scale) + shift).astype(BF16)


    u = _dot(h, win_ref[:, 3 * D_CONV:])
    for q in range(N_SLABS):
        ubuf[q, halo:halo + tm, :] = u[:, q * LANES:(q + 1) * LANES]
    u_curs, injs = [], []
    for q in range(N_SLABS):
        cur = [ubuf[q, pl.ds(halo + s, m_rows, stride=chunk), :] for s in range(chunk)]
        prv = [ubuf[q, pl.ds(halo - chunk + s, m_rows, stride=chunk), :] for s in range(chunk)]
        u_curs.append(jnp.concatenate(cur, axis=1).astype(BF16))
        u_prv = jnp.concatenate(prv, axis=1).astype(BF16)
        ubuf[q, 0:halo, :] = ubuf[q, tm:tm + halo, :]
        injs.append(_dot(u_prv, wb_ref[q]))

    conv_proj = []
    for q in range(N_SLABS):
        if q < 3:
            conv_proj.append(_dot(h, win_ref[:, q * D_CONV:(q + 1) * D_CONV]))
        inj = injs[q]
        for i in range(SLAB_STATE // LANES):
            lanes_re = slice(i * LANES, (i + 1) * LANES)
            lanes_im = slice(SLAB_STATE + i * LANES, SLAB_STATE + (i + 1) * LANES)
            re = inj[:, lanes_re].reshape(m_rows // SUBLANES, SUBLANES, LANES)
            im = inj[:, lanes_im].reshape(m_rows // SUBLANES, SUBLANES, LANES)
            for k, step in enumerate(SCAN_STEPS):
                a_re = tab_ref[q, 2 * k, :, lanes_re]
                a_im = tab_ref[q, 2 * k + 1, :, lanes_re]
                s_re = pltpu.roll(re, step, axis=1)
                s_im = pltpu.roll(im, step, axis=1)
                re, im = re + a_re * s_re - a_im * s_im, im + a_re * s_im + a_im * s_re
            p_re = tab_ref[q, 2 * len(SCAN_STEPS), :, lanes_re]
            p_im = tab_ref[q, 2 * len(SCAN_STEPS) + 1, :, lanes_re]
            c_re = carry[q:q + 1, lanes_re]
            c_im = carry[q:q + 1, lanes_im]
            for r in range(m_rows // SUBLANES):
                rows = slice(r * SUBLANES, (r + 1) * SUBLANES)
                b_re = jnp.broadcast_to(c_re, (SUBLANES, LANES))
                b_im = jnp.broadcast_to(c_im, (SUBLANES, LANES))
                f_re = re[r] + p_re * b_re - p_im * b_im
                f_im = im[r] + p_re * b_im + p_im * b_re
                hbuf[q, rows, lanes_re] = f_re
                hbuf[q, rows, lanes_im] = f_im
                c_re = f_re[SUBLANES - 1:SUBLANES, :]
                c_im = f_im[SUBLANES - 1:SUBLANES, :]
            carry[q:q + 1, lanes_re] = c_re
            carry[q:q + 1, lanes_im] = c_im

    for q in range(N_SLABS):
        y2 = _dot(u_curs[q], toep_ref[q]) + _dot(hbuf[q].astype(BF16), wc_ref[q])
        for s in range(chunk):
            ybuf[q, pl.ds(s, m_rows, stride=chunk), :] = y2[:, s * LANES:(s + 1) * LANES]
    b_gate, c_gate, v = conv_proj
    z = c_gate * v
    zbuf[halo:halo + tm, :] = z
    z1 = zbuf[halo - 1:halo - 1 + tm, :]
    z2 = zbuf[halo - 2:halo - 2 + tm, :]
    y_conv = b_gate * (convw_ref[0:1, :] * z2 + convw_ref[1:2, :] * z1 + convw_ref[2:3, :] * z)
    zbuf[0:halo, :] = zbuf[tm:tm + halo, :]
    ymix[:, 0:D_CONV] = _rms_norm(y_conv, gain_c_ref[...]).astype(BF16)

    for r in range(MIX_ROWS // MIX_SUB):
        rows = slice(r * MIX_SUB, (r + 1) * MIX_SUB)
        u_rows = slice(halo + r * MIX_SUB, halo + (r + 1) * MIX_SUB)
        y = jnp.concatenate([ybuf[q, rows, :] for q in range(N_SLABS)], axis=1)
        y = y + dskip_ref[...] * jnp.concatenate([ubuf[q, u_rows, :] for q in range(N_SLABS)], axis=1)
        y = _gelu_tanh(y)
        y = y * _sigmoid(_dot(y.astype(BF16), gluw_ref[...]) + glub_ref[...])
        ymix[rows, D_CONV:] = _rms_norm(y, gain_s_ref[...]).astype(BF16)
        o_ref[rows, :] = x_ref[rows, :] + gate * _dot(ymix[rows, :], wout_ref[...])


def _mixer(x, mod, gain, w_in, conv_w, toep, wb, wc, tab, d_skip, glu_w, glu_b,
           gain_c, gain_s, w_out):
    bsz, seq, _ = x.shape
    tm = MIX_ROWS
    m_rows = tm // SSM_CHUNK
    return pl.pallas_call(
        _mixer_kernel,
        grid=(bsz, seq // tm),
        in_specs=[
            pl.BlockSpec((None, tm, D_MODEL), lambda b, t: (b, t, 0)),
            _const_spec((N_MOD, MOD_ROWS, D_MODEL)),
            _const_spec((1, D_MODEL)),
            _const_spec(w_in.shape),
            _const_spec(conv_w.shape),
            _const_spec(toep.shape),
            _const_spec(wb.shape),
            _const_spec(wc.shape),
            _const_spec(tab.shape),
            _const_spec((1, D_SSM)),
            _const_spec(glu_w.shape),
            _const_spec((1, D_SSM)),
            _const_spec((1, D_CONV)),
            _const_spec((1, D_SSM)),
            _const_spec(w_out.shape),
        ],
        out_specs=pl.BlockSpec((None, tm, D_MODEL), lambda b, t: (b, t, 0)),
        out_shape=jax.ShapeDtypeStruct(x.shape, F32),
        scratch_shapes=[
            pltpu.VMEM((SUBLANES + tm, D_CONV), F32),
            pltpu.VMEM((N_SLABS, SUBLANES + tm, LANES), F32),
            pltpu.VMEM((N_SLABS, tm, LANES), F32),
            pltpu.VMEM((N_SLABS, m_rows, 2 * SLAB_STATE), F32),
            pltpu.VMEM((N_SLABS, 2 * SLAB_STATE), F32),
            pltpu.VMEM((tm, D_MODEL), BF16),
        ],
        compiler_params=pltpu.CompilerParams(
            dimension_semantics=("arbitrary", "arbitrary"),
            vmem_limit_bytes=VMEM_LIMIT_BYTES),
        name="mixer",
    )(x, mod, gain.reshape(1, D_MODEL), w_in, conv_w, toep, wb, wc, tab,
      d_skip.reshape(1, D_SSM), glu_w, glu_b.reshape(1, D_SSM),
      gain_c.reshape(1, D_CONV), gain_s.reshape(1, D_SSM), w_out)


def _cmul(ar, ai, br, bi):
    return ar * br - ai * bi, ar * bi + ai * br


def _ssm_ops_kernel(bt_ref, ct_ref, base_ref, toep_ref, wb_ref, wc_ref, tab_ref):
    L = SSM_CHUNK
    bar = base_ref[0, 0:1, :], base_ref[1, 0:1, :]
    coef = base_ref[0, 1:2, :], base_ref[1, 1:2, :]
    pows = [(jnp.ones((1, SLAB_STATE), F32), jnp.zeros((1, SLAB_STATE), F32))]
    for _ in range(SUBLANES * L):
        pows.append(_cmul(*pows[-1], *bar))

    row = lax.broadcasted_iota(jnp.int32, (SUBLANES, SLAB_STATE), 0)
    for k, step in enumerate(SCAN_STEPS):
        for part in range(2):
            tab_ref[2 * k + part] = jnp.where(row >= step, pows[L * step][part], 0.0)
    for part in range(2):
        plane = jnp.zeros((SUBLANES, SLAB_STATE), F32)
        for t in range(SUBLANES):
            plane = jnp.where(row == t, pows[L * (t + 1)][part], plane)
        tab_ref[2 * len(SCAN_STEPS) + part] = plane

    shape = (LANES, SLAB_STATE)
    row_group = lax.broadcasted_iota(jnp.int32, shape, 0) >> (SSM_GROUP_CH.bit_length() - 1)
    col_group = lax.broadcasted_iota(jnp.int32, shape, 1) >> (SSM_STATE.bit_length() - 1)
    same_group = row_group == col_group
    tile_p = (lax.broadcasted_iota(jnp.int32, (SSM_STATE, SLAB_STATE), 0)
              == (lax.broadcasted_iota(jnp.int32, (SSM_STATE, SLAB_STATE), 1) & (SSM_STATE - 1))
              ).astype(BF16)

    def split(a):
        head = a.astype(BF16)
        return head, (a - head.astype(F32)).astype(BF16)

    def block_diag(a):
        head, rest = split(a)
        return jnp.where(same_group, _dot(head, tile_p) + _dot(rest, tile_p), 0.0)

    bt_re, bt_im = block_diag(bt_ref[0]), block_diag(bt_ref[1])
    ct_re, ct_im = block_diag(ct_ref[0]), block_diag(ct_ref[1])
    ct_re_parts, ct_im_parts = split(ct_re), split(ct_im)

    def dot_nt(a, b_parts):
        nt = (((1,), (1,)), ((), ()))
        a_head, a_rest = split(a)
        b_head, b_rest = b_parts
        return (lax.dot_general(a_head, b_head, nt, preferred_element_type=F32)
                + lax.dot_general(a_rest, b_head, nt, preferred_element_type=F32)
                + lax.dot_general(a_head, b_rest, nt, preferred_element_type=F32))

    taps = []
    for m in range(L):
        s = L - 1 - m
        w_re, w_im = _cmul(bt_re, bt_im, *_cmul(*coef, *pows[m]))
        wb_ref[s * LANES:(s + 1) * LANES, 0:SLAB_STATE] = w_re.astype(BF16)
        wb_ref[s * LANES:(s + 1) * LANES, SLAB_STATE:] = w_im.astype(BF16)
        taps.append(dot_nt(w_re, ct_re_parts) - dot_nt(w_im, ct_im_parts))
    for s in range(L):
        for t in range(L):
            blk = taps[t - s] if t >= s else jnp.zeros((LANES, LANES), F32)
            toep_ref[s * LANES:(s + 1) * LANES, t * LANES:(t + 1) * LANES] = blk.astype(BF16)
    for t in range(L):
        o_re, o_im = _cmul(ct_re, ct_im, *pows[t + 1])
        wc_ref[0:SLAB_STATE, t * LANES:(t + 1) * LANES] = o_re.T.astype(BF16)
        wc_ref[SLAB_STATE:, t * LANES:(t + 1) * LANES] = (-o_im).T.astype(BF16)


def _ssm_operators(lambda_re, lambda_im, log_dt, b_re, b_im, c_re, c_im):
    L = SSM_CHUNK
    lre, lim = lambda_re.astype(F32), lambda_im.astype(F32)
    dt = jnp.exp(log_dt.astype(F32))[:, None]
    mag = jnp.exp(lre * dt)
    bar_re, bar_im = mag * jnp.cos(lim * dt), mag * jnp.sin(lim * dt)
    den = lre * lre + lim * lim
    coef_re = ((bar_re - 1.0) * lre + bar_im * lim) / den
    coef_im = (bar_im * lre - (bar_re - 1.0) * lim) / den

    def slab(a):
        return a.reshape(N_SLABS, SLAB_STATE)

    zero = jnp.zeros((N_SLABS, SUBLANES - 2, SLAB_STATE), F32)
    base = jnp.stack([
        jnp.concatenate([slab(bar_re)[:, None], slab(coef_re)[:, None], zero], axis=1),
        jnp.concatenate([slab(bar_im)[:, None], slab(coef_im)[:, None], zero], axis=1)], axis=1)

    def compact(re, im):
        return jnp.stack([re, im], axis=1).astype(F32).reshape(
            N_SLABS, GROUPS_PER_SLAB, 2, SSM_GROUP_CH, SSM_STATE).transpose(0, 2, 1, 3, 4).reshape(
            N_SLABS, 2, LANES, SSM_STATE)

    bt = compact(b_re.transpose(0, 2, 1), b_im.transpose(0, 2, 1))
    ct = compact(c_re, c_im)

    n_planes = 2 * (len(SCAN_STEPS) + 1)
    return pl.pallas_call(
        _ssm_ops_kernel,
        grid=(N_SLABS,),
        in_specs=[
            pl.BlockSpec((None, 2, LANES, SSM_STATE), lambda q: (q, 0, 0, 0)),
            pl.BlockSpec((None, 2, LANES, SSM_STATE), lambda q: (q, 0, 0, 0)),
            pl.BlockSpec((None, 2, SUBLANES, SLAB_STATE), lambda q: (q, 0, 0, 0)),
        ],
        out_specs=[
            pl.BlockSpec((None, L * LANES, L * LANES), lambda q: (q, 0, 0)),
            pl.BlockSpec((None, L * LANES, 2 * SLAB_STATE), lambda q: (q, 0, 0)),
            pl.BlockSpec((None, 2 * SLAB_STATE, L * LANES), lambda q: (q, 0, 0)),
            pl.BlockSpec((None, n_planes, SUBLANES, SLAB_STATE), lambda q: (q, 0, 0, 0)),
        ],
        out_shape=[
            jax.ShapeDtypeStruct((N_SLABS, L * LANES, L * LANES), BF16),
            jax.ShapeDtypeStruct((N_SLABS, L * LANES, 2 * SLAB_STATE), BF16),
            jax.ShapeDtypeStruct((N_SLABS, 2 * SLAB_STATE, L * LANES), BF16),
            jax.ShapeDtypeStruct((N_SLABS, n_planes, SUBLANES, SLAB_STATE), F32),
        ],
        compiler_params=pltpu.CompilerParams(
            dimension_semantics=("arbitrary",), vmem_limit_bytes=VMEM_LIMIT_BYTES),
        name="ssm_ops",
    )(bt, ct, base)


def kernel(x, cond, w_mod, b_mod, ffn1_norm, ffn1_w_gate, ffn1_w_up, ffn1_w_down, mix_norm, w_in, conv_w, lambda_re, lambda_im, log_dt, ssm_b_re, ssm_b_im, ssm_c_re, ssm_c_im, ssm_d, glu_w, glu_b, out_norm_conv, out_norm_ssm, w_out, ffn2_norm, ffn2_w_gate, ffn2_w_up, ffn2_w_down, final_norm):
    depth = w_mod.shape[0]
    for l in range(depth):
        mod = _modulation(cond, w_mod[l], b_mod[l])
        later = (w_in[l], glu_w[l], w_out[l], ffn2_w_gate[l], ffn2_w_up[l], ffn2_w_down[l])
        x, (w_in_b, glu_w_b, w_out_b, wg2, wu2, wd2) = _ffn(
            x, mod, ffn1_norm[l], ffn1_w_gate[l], ffn1_w_up[l], ffn1_w_down[l], final_norm,
            mod_base=0, final_norm=False, cast=later)
        toep, wb, wc, tab = _ssm_operators(lambda_re[l], lambda_im[l], log_dt[l], ssm_b_re[l],
                                           ssm_b_im[l], ssm_c_re[l], ssm_c_im[l])
        x = _mixer(x, mod, mix_norm[l], w_in_b, conv_w[l], toep, wb, wc, tab,
                   ssm_d[l], glu_w_b, glu_b[l], out_norm_conv[l], out_norm_ssm[l], w_out_b)
        x, _ = _ffn(x, mod, ffn2_norm[l], wg2, wu2, wd2, final_norm, mod_base=6,
                    final_norm=(l == depth - 1))
    return x
```

```python
import functools
import math

import jax
import jax.numpy as jnp
from jax import lax
from jax.experimental import pallas as pl
from jax.experimental.pallas import tpu as pltpu

D_MODEL = 1024
D_CONV = 512
D_SSM = 512
CONV_WIDTH = 3
SSM_GROUP_CH = 16
SSM_GROUPS = 32
SSM_STATE = 64
D_FF = 2816
N_MOD = 9
EPS = 1e-6

SUBLANES = 8
LANES = 128
VMEM_LIMIT_BYTES = 56 * 1024 * 1024

MOD_COLS = 3072
MOD_ROWS = SUBLANES
FFN_ROWS = 1024
FFN_SUB = 256
FFN_COLS = 256
MIX_ROWS = 1024
MIX_SUB = 256
SSM_CHUNK = 4
GROUPS_PER_SLAB = LANES // SSM_GROUP_CH
N_SLABS = D_SSM // LANES
SLAB_STATE = GROUPS_PER_SLAB * SSM_STATE
SCAN_STEPS = (1, 2, 4)

BF16 = jnp.bfloat16
F32 = jnp.float32


def _dot(a, b):
    return jnp.dot(a, b, preferred_element_type=F32)


def _sigmoid(x):
    return 1.0 / (1.0 + jnp.exp(-x))


def _rms_norm(x, gain):
    ms = jnp.mean(x * x, axis=-1, keepdims=True)
    return x * lax.rsqrt(ms + EPS) * gain


def _gelu_tanh(x):
    c = math.sqrt(2.0 / math.pi)
    return x * (0.5 * (1.0 + jnp.tanh(c * (x + 0.044715 * (x * x * x)))))


def _mod_kernel(c_ref, w_ref, b_ref, o_ref):
    c = c_ref[...]
    c = c * _sigmoid(c)
    pad = jnp.zeros((MOD_ROWS - c.shape[0], D_MODEL), F32)
    c = jnp.concatenate([c, pad], axis=0)
    res = _dot(c.astype(BF16), w_ref[...].astype(BF16)) + b_ref[...]
    for i in range(MOD_COLS // D_MODEL):
        o_ref[i] = res[:, i * D_MODEL:(i + 1) * D_MODEL]


def _modulation(cond, w_mod, b_mod):
    bsz = cond.shape[0]
    assert bsz <= MOD_ROWS
    n = w_mod.shape[1]
    tn = MOD_COLS
    return pl.pallas_call(
        _mod_kernel,
        grid=(n // tn,),
        in_specs=[
            pl.BlockSpec((bsz, D_MODEL), lambda j: (0, 0)),
            pl.BlockSpec((D_MODEL, tn), lambda j: (0, j)),
            pl.BlockSpec((1, tn), lambda j: (0, j)),
        ],
        out_specs=pl.BlockSpec((tn // D_MODEL, MOD_ROWS, D_MODEL), lambda j: (j, 0, 0)),
        out_shape=jax.ShapeDtypeStruct((N_MOD, MOD_ROWS, D_MODEL), F32),
        compiler_params=pltpu.CompilerParams(
            dimension_semantics=("arbitrary",), vmem_limit_bytes=VMEM_LIMIT_BYTES),
        name="adaln_mod",
    )(cond, w_mod, b_mod.reshape(1, n))


def _ffn_kernel(*refs, mod_base, final_norm, n_cast):
    x_ref, mod_ref, gain_ref, wg_ref, wu_ref, wd_ref, fgain_ref = refs[:7]
    cast_in = refs[7:7 + n_cast]
    o_ref = refs[7 + n_cast]
    cast_out = refs[8 + n_cast:8 + 2 * n_cast]
    act_ref = refs[8 + 2 * n_cast]
    for src, dst in zip(cast_in, cast_out):
        dst[...] = src[...].astype(BF16)
    batch = pl.ds(pl.program_id(0), 1)
    shift = mod_ref[mod_base, batch, :]
    scale = mod_ref[mod_base + 1, batch, :]
    gate = mod_ref[mod_base + 2, batch, :]
    for r in range(FFN_ROWS // FFN_SUB):
        rows = slice(r * FFN_SUB, (r + 1) * FFN_SUB)
        x = x_ref[rows, :]
        h = (_rms_norm(x, gain_ref[...]) * (1.0 + scale) + shift).astype(BF16)
        for f in range(D_FF // FFN_COLS):
            cols = slice(f * FFN_COLS, (f + 1) * FFN_COLS)
            g = _dot(h, wg_ref[:, cols])
            u = _dot(h, wu_ref[:, cols])
            act_ref[rows, cols] = (g * _sigmoid(g) * u).astype(BF16)
        y = _dot(act_ref[rows, :], wd_ref[...])
        out = x + (0.5 * gate) * y
        if final_norm:
            out = _rms_norm(out, fgain_ref[...])
        o_ref[rows, :] = out


def _const_spec(shape):
    zeros = (0,) * len(shape)
    return pl.BlockSpec(shape, lambda *_: zeros, pipeline_mode=pl.Buffered(1))


def _ffn(x, mod, gain, w_gate, w_up, w_down, final_gain, *, mod_base, final_norm, cast=()):
    bsz, seq, _ = x.shape
    tm = FFN_ROWS
    n_t = seq // tm
    steps = bsz * n_t
    kernel = functools.partial(_ffn_kernel, mod_base=mod_base, final_norm=final_norm,
                               n_cast=len(cast))
    band_specs = [pl.BlockSpec((w.shape[0] // steps, w.shape[1]), lambda b, t: (b * n_t + t, 0))
                  for w in cast]
    outs = pl.pallas_call(
        kernel,
        grid=(bsz, n_t),
        in_specs=[
            pl.BlockSpec((None, tm, D_MODEL), lambda b, t: (b, t, 0)),
            _const_spec((N_MOD, MOD_ROWS, D_MODEL)),
            _const_spec((1, D_MODEL)),
            _const_spec((D_MODEL, D_FF)),
            _const_spec((D_MODEL, D_FF)),
            _const_spec((D_FF, D_MODEL)),
            _const_spec((1, D_MODEL)),
        ] + band_specs,
        out_specs=[pl.BlockSpec((None, tm, D_MODEL), lambda b, t: (b, t, 0))] + band_specs,
        out_shape=[jax.ShapeDtypeStruct(x.shape, F32)]
                  + [jax.ShapeDtypeStruct(w.shape, BF16) for w in cast],
        scratch_shapes=[pltpu.VMEM((tm, D_FF), BF16)],
        compiler_params=pltpu.CompilerParams(
            dimension_semantics=("arbitrary", "arbitrary"),
            vmem_limit_bytes=VMEM_LIMIT_BYTES),
        name="ffn_final" if final_norm else "ffn",
    )(x, mod, gain.reshape(1, D_MODEL), w_gate, w_up, w_down,
      final_gain.reshape(1, D_MODEL), *cast)
    return outs[0], tuple(outs[1:])


def _mixer_kernel(x_ref, mod_ref, gain_ref, win_ref, convw_ref, toep_ref, wb_ref, wc_ref,
                  tab_ref, dskip_ref, gluw_ref, glub_ref, gain_c_ref, gain_s_ref, wout_ref,
                  o_ref, zbuf, ubuf, ybuf, hbuf, carry, ymix):
    tm = MIX_ROWS
    chunk = SSM_CHUNK
    m_rows = tm // chunk
    halo = SUBLANES

    @pl.when(pl.program_id(1) == 0)
    def _():
        zbuf[0:halo, :] = jnp.zeros((halo, D_CONV), F32)
        ubuf[:, 0:halo, :] = jnp.zeros((N_SLABS, halo, LANES), F32)
        carry[...] = jnp.zeros(carry.shape, F32)

    x = x_ref[...]
    batch = pl.ds(pl.program_id(0), 1)
    shift = mod_ref[3, batch, :]
    scale = mod_ref[4, batch, :]
    gate = mod_ref[5, batch, :]
    h = (_rms_norm(x, gain_ref[...]) * (1.0 + scale) + shift).astype(BF16)


    u = _dot(h, win_ref[:, 3 * D_CONV:])
    for q in range(N_SLABS):
        ubuf[q, halo:halo + tm, :] = u[:, q * LANES:(q + 1) * LANES]
    u_curs, u_prvs = [], []
    for q in range(N_SLABS):
        cur = [ubuf[q, pl.ds(halo + s, m_rows, stride=chunk), :] for s in range(chunk)]
        prv = [ubuf[q, pl.ds(halo - chunk + s, m_rows, stride=chunk), :] for s in range(chunk)]
        u_curs.append(jnp.concatenate(cur, axis=1).astype(BF16))
        u_prvs.append(jnp.concatenate(prv, axis=1).astype(BF16))
        ubuf[q, 0:halo, :] = ubuf[q, tm:tm + halo, :]

    def chunk_outputs(q):
        y2 = _dot(u_curs[q], toep_ref[q]) + _dot(hbuf[q].astype(BF16), wc_ref[q])
        for s in range(chunk):
            ybuf[q, pl.ds(s, m_rows, stride=chunk), :] = y2[:, s * LANES:(s + 1) * LANES]

    conv_proj = []
    for q in range(N_SLABS):
        inj = _dot(u_prvs[q], wb_ref[q])
        if q < 3:
            conv_proj.append(_dot(h, win_ref[:, q * D_CONV:(q + 1) * D_CONV]))
        for i in range(SLAB_STATE // LANES):
            lanes_re = slice(i * LANES, (i + 1) * LANES)
            lanes_im = slice(SLAB_STATE + i * LANES, SLAB_STATE + (i + 1) * LANES)
            re = inj[:, lanes_re].reshape(m_rows // SUBLANES, SUBLANES, LANES)
            im = inj[:, lanes_im].reshape(m_rows // SUBLANES, SUBLANES, LANES)
            for k, step in enumerate(SCAN_STEPS):
                a_re = tab_ref[q, 2 * k, :, lanes_re]
                a_im = tab_ref[q, 2 * k + 1, :, lanes_re]
                s_re = pltpu.roll(re, step, axis=1)
                s_im = pltpu.roll(im, step, axis=1)
                re, im = re + a_re * s_re - a_im * s_im, im + a_re * s_im + a_im * s_re
            p_re = tab_ref[q, 2 * len(SCAN_STEPS), :, lanes_re]
            p_im = tab_ref[q, 2 * len(SCAN_STEPS) + 1, :, lanes_re]
            c_re = carry[q:q + 1, lanes_re]
            c_im = carry[q:q + 1, lanes_im]
            for r in range(m_rows // SUBLANES):
                rows = slice(r * SUBLANES, (r + 1) * SUBLANES)
                b_re = jnp.broadcast_to(c_re, (SUBLANES, LANES))
                b_im = jnp.broadcast_to(c_im, (SUBLANES, LANES))
                f_re = re[r] + p_re * b_re - p_im * b_im
                f_im = im[r] + p_re * b_im + p_im * b_re
                hbuf[q, rows, lanes_re] = f_re
                hbuf[q, rows, lanes_im] = f_im
                c_re = f_re[SUBLANES - 1:SUBLANES, :]
                c_im = f_im[SUBLANES - 1:SUBLANES, :]
            carry[q:q + 1, lanes_re] = c_re
            carry[q:q + 1, lanes_im] = c_im

    for q in range(N_SLABS):
        chunk_outputs(q)
    b_gate, c_gate, v = conv_proj
    z = c_gate * v
    zbuf[halo:halo + tm, :] = z
    z1 = zbuf[halo - 1:halo - 1 + tm, :]
    z2 = zbuf[halo - 2:halo - 2 + tm, :]
    y_conv = b_gate * (convw_ref[0:1, :] * z2 + convw_ref[1:2, :] * z1 + convw_ref[2:3, :] * z)
    zbuf[0:halo, :] = zbuf[tm:tm + halo, :]
    ymix[:, 0:D_CONV] = _rms_norm(y_conv, gain_c_ref[...]).astype(BF16)

    for r in range(MIX_ROWS // MIX_SUB):
        rows = slice(r * MIX_SUB, (r + 1) * MIX_SUB)
        u_rows = slice(halo + r * MIX_SUB, halo + (r + 1) * MIX_SUB)
        y = jnp.concatenate([ybuf[q, rows, :] for q in range(N_SLABS)], axis=1)
        y = y + dskip_ref[...] * jnp.concatenate([ubuf[q, u_rows, :] for q in range(N_SLABS)], axis=1)
        y = _gelu_tanh(y)
        y = y * _sigmoid(_dot(y.astype(BF16), gluw_ref[...]) + glub_ref[...])
        ymix[rows, D_CONV:] = _rms_norm(y, gain_s_ref[...]).astype(BF16)
        o_ref[rows, :] = x_ref[rows, :] + gate * _dot(ymix[rows, :], wout_ref[...])


def _mixer(x, mod, gain, w_in, conv_w, toep, wb, wc, tab, d_skip, glu_w, glu_b,
           gain_c, gain_s, w_out):
    bsz, seq, _ = x.shape
    tm = MIX_ROWS
    m_rows = tm // SSM_CHUNK
    return pl.pallas_call(
        _mixer_kernel,
        grid=(bsz, seq // tm),
        in_specs=[
            pl.BlockSpec((None, tm, D_MODEL), lambda b, t: (b, t, 0)),
            _const_spec((N_MOD, MOD_ROWS, D_MODEL)),
            _const_spec((1, D_MODEL)),
            _const_spec(w_in.shape),
            _const_spec(conv_w.shape),
            _const_spec(toep.shape),
            _const_spec(wb.shape),
            _const_spec(wc.shape),
            _const_spec(tab.shape),
            _const_spec((1, D_SSM)),
            _const_spec(glu_w.shape),
            _const_spec((1, D_SSM)),
            _const_spec((1, D_CONV)),
            _const_spec((1, D_SSM)),
            _const_spec(w_out.shape),
        ],
        out_specs=pl.BlockSpec((None, tm, D_MODEL), lambda b, t: (b, t, 0)),
        out_shape=jax.ShapeDtypeStruct(x.shape, F32),
        scratch_shapes=[
            pltpu.VMEM((SUBLANES + tm, D_CONV), F32),
            pltpu.VMEM((N_SLABS, SUBLANES + tm, LANES), F32),
            pltpu.VMEM((N_SLABS, tm, LANES), F32),
            pltpu.VMEM((N_SLABS, m_rows, 2 * SLAB_STATE), F32),
            pltpu.VMEM((N_SLABS, 2 * SLAB_STATE), F32),
            pltpu.VMEM((tm, D_MODEL), BF16),
        ],
        compiler_params=pltpu.CompilerParams(
            dimension_semantics=("arbitrary", "arbitrary"),
            vmem_limit_bytes=VMEM_LIMIT_BYTES),
        name="mixer",
    )(x, mod, gain.reshape(1, D_MODEL), w_in, conv_w, toep, wb, wc, tab,
      d_skip.reshape(1, D_SSM), glu_w, glu_b.reshape(1, D_SSM),
      gain_c.reshape(1, D_CONV), gain_s.reshape(1, D_SSM), w_out)


def _cmul(ar, ai, br, bi):
    return ar * br - ai * bi, ar * bi + ai * br


def _ssm_ops_kernel(bt_ref, ct_ref, base_ref, toep_ref, wb_ref, wc_ref, tab_ref):
    L = SSM_CHUNK
    bar = base_ref[0, 0:1, :], base_ref[1, 0:1, :]
    coef = base_ref[0, 1:2, :], base_ref[1, 1:2, :]
    pows = [(jnp.ones((1, SLAB_STATE), F32), jnp.zeros((1, SLAB_STATE), F32))]
    for _ in range(SUBLANES * L):
        pows.append(_cmul(*pows[-1], *bar))

    row = lax.broadcasted_iota(jnp.int32, (SUBLANES, SLAB_STATE), 0)
    for k, step in enumerate(SCAN_STEPS):
        for part in range(2):
            tab_ref[2 * k + part] = jnp.where(row >= step, pows[L * step][part], 0.0)
    for part in range(2):
        plane = jnp.zeros((SUBLANES, SLAB_STATE), F32)
        for t in range(SUBLANES):
            plane = jnp.where(row == t, pows[L * (t + 1)][part], plane)
        tab_ref[2 * len(SCAN_STEPS) + part] = plane

    shape = (LANES, SLAB_STATE)
    row_group = lax.broadcasted_iota(jnp.int32, shape, 0) >> (SSM_GROUP_CH.bit_length() - 1)
    col_group = lax.broadcasted_iota(jnp.int32, shape, 1) >> (SSM_STATE.bit_length() - 1)
    same_group = row_group == col_group
    tile_p = (lax.broadcasted_iota(jnp.int32, (SSM_STATE, SLAB_STATE), 0)
              == (lax.broadcasted_iota(jnp.int32, (SSM_STATE, SLAB_STATE), 1) & (SSM_STATE - 1))
              ).astype(BF16)

    def split(a):
        head = a.astype(BF16)
        return head, (a - head.astype(F32)).astype(BF16)

    def block_diag(a):
        head, rest = split(a)
        return jnp.where(same_group, _dot(head, tile_p) + _dot(rest, tile_p), 0.0)

    bt_re, bt_im = block_diag(bt_ref[0]), block_diag(bt_ref[1])
    ct_re, ct_im = block_diag(ct_ref[0]), block_diag(ct_ref[1])
    ct_re_parts, ct_im_parts = split(ct_re), split(ct_im)

    def dot_nt(a, b_parts):
        nt = (((1,), (1,)), ((), ()))
        a_head, a_rest = split(a)
        b_head, b_rest = b_parts
        return (lax.dot_general(a_head, b_head, nt, preferred_element_type=F32)
                + lax.dot_general(a_rest, b_head, nt, preferred_element_type=F32)
                + lax.dot_general(a_head, b_rest, nt, preferred_element_type=F32))

    taps = []
    for m in range(L):
        s = L - 1 - m
        w_re, w_im = _cmul(bt_re, bt_im, *_cmul(*coef, *pows[m]))
        wb_ref[s * LANES:(s + 1) * LANES, 0:SLAB_STATE] = w_re.astype(BF16)
        wb_ref[s * LANES:(s + 1) * LANES, SLAB_STATE:] = w_im.astype(BF16)
        taps.append(dot_nt(w_re, ct_re_parts) - dot_nt(w_im, ct_im_parts))
    for s in range(L):
        for t in range(L):
            blk = taps[t - s] if t >= s else jnp.zeros((LANES, LANES), F32)
            toep_ref[s * LANES:(s + 1) * LANES, t * LANES:(t + 1) * LANES] = blk.astype(BF16)
    for t in range(L):
        o_re, o_im = _cmul(ct_re, ct_im, *pows[t + 1])
        wc_ref[0:SLAB_STATE, t * LANES:(t + 1) * LANES] = o_re.T.astype(BF16)
        wc_ref[SLAB_STATE:, t * LANES:(t + 1) * LANES] = (-o_im).T.astype(BF16)


def _ssm_operators(lambda_re, lambda_im, log_dt, b_re, b_im, c_re, c_im):
    L = SSM_CHUNK
    lre, lim = lambda_re.astype(F32), lambda_im.astype(F32)
    dt = jnp.exp(log_dt.astype(F32))[:, None]
    mag = jnp.exp(lre * dt)
    bar_re, bar_im = mag * jnp.cos(lim * dt), mag * jnp.sin(lim * dt)
    den = lre * lre + lim * lim
    coef_re = ((bar_re - 1.0) * lre + bar_im * lim) / den
    coef_im = (bar_im * lre - (bar_re - 1.0) * lim) / den

    def slab(a):
        return a.reshape(N_SLABS, SLAB_STATE)

    zero = jnp.zeros((N_SLABS, SUBLANES - 2, SLAB_STATE), F32)
    base = jnp.stack([
        jnp.concatenate([slab(bar_re)[:, None], slab(coef_re)[:, None], zero], axis=1),
        jnp.concatenate([slab(bar_im)[:, None], slab(coef_im)[:, None], zero], axis=1)], axis=1)

    def compact(re, im):
        return jnp.stack([re, im], axis=1).astype(F32).reshape(
            N_SLABS, GROUPS_PER_SLAB, 2, SSM_GROUP_CH, SSM_STATE).transpose(0, 2, 1, 3, 4).reshape(
            N_SLABS, 2, LANES, SSM_STATE)

    bt = compact(b_re.transpose(0, 2, 1), b_im.transpose(0, 2, 1))
    ct = compact(c_re, c_im)

    n_planes = 2 * (len(SCAN_STEPS) + 1)
    return pl.pallas_call(
        _ssm_ops_kernel,
        grid=(N_SLABS,),
        in_specs=[
            pl.BlockSpec((None, 2, LANES, SSM_STATE), lambda q: (q, 0, 0, 0)),
            pl.BlockSpec((None, 2, LANES, SSM_STATE), lambda q: (q, 0, 0, 0)),
            pl.BlockSpec((None, 2, SUBLANES, SLAB_STATE), lambda q: (q, 0, 0, 0)),
        ],
        out_specs=[
            pl.BlockSpec((None, L * LANES, L * LANES), lambda q: (q, 0, 0)),
            pl.BlockSpec((None, L * LANES, 2 * SLAB_STATE), lambda q: (q, 0, 0)),
            pl.BlockSpec((None, 2 * SLAB_STATE, L * LANES), lambda q: (q, 0, 0)),
            pl.BlockSpec((None, n_planes, SUBLANES, SLAB_STATE), lambda q: (q, 0, 0, 0)),
        ],
        out_shape=[
            jax.ShapeDtypeStruct((N_SLABS, L * LANES, L * LANES), BF16),
            jax.ShapeDtypeStruct((N_SLABS, L * LANES, 2 * SLAB_STATE), BF16),
            jax.ShapeDtypeStruct((N_SLABS, 2 * SLAB_STATE, L * LANES), BF16),
            jax.ShapeDtypeStruct((N_SLABS, n_planes, SUBLANES, SLAB_STATE), F32),
        ],
        compiler_params=pltpu.CompilerParams(
            dimension_semantics=("arbitrary",), vmem_limit_bytes=VMEM_LIMIT_BYTES),
        name="ssm_ops",
    )(bt, ct, base)


def kernel(x, cond, w_mod, b_mod, ffn1_norm, ffn1_w_gate, ffn1_w_up, ffn1_w_down, mix_norm, w_in, conv_w, lambda_re, lambda_im, log_dt, ssm_b_re, ssm_b_im, ssm_c_re, ssm_c_im, ssm_d, glu_w, glu_b, out_norm_conv, out_norm_ssm, w_out, ffn2_norm, ffn2_w_gate, ffn2_w_up, ffn2_w_down, final_norm):
    depth = w_mod.shape[0]
    for l in range(depth):
        mod = _modulation(cond, w_mod[l], b_mod[l])
        later = (w_in[l], glu_w[l], w_out[l], ffn2_w_gate[l], ffn2_w_up[l], ffn2_w_down[l])
        x, (w_in_b, glu_w_b, w_out_b, wg2, wu2, wd2) = _ffn(
            x, mod, ffn1_norm[l], ffn1_w_gate[l].astype(BF16), ffn1_w_up[l].astype(BF16),
            ffn1_w_down[l].astype(BF16), final_norm, mod_base=0, final_norm=False, cast=later)
        toep, wb, wc, tab = _ssm_operators(lambda_re[l], lambda_im[l], log_dt[l], ssm_b_re[l],
                                           ssm_b_im[l], ssm_c_re[l], ssm_c_im[l])
        x = _mixer(x, mod, mix_norm[l], w_in_b, conv_w[l], toep, wb, wc, tab,
                   ssm_d[l], glu_w_b, glu_b[l], out_norm_conv[l], out_norm_ssm[l], w_out_b)
        x, _ = _ffn(x, mod, ffn2_norm[l], wg2, wu2, wd2, final_norm, mod_base=6,
                    final_norm=(l == depth - 1))
    return x
```

```python
import functools
import math

import jax
import jax.numpy as jnp
from jax import lax
from jax.experimental import pallas as pl
from jax.experimental.pallas import tpu as pltpu

D_MODEL = 1024
D_CONV = 512
D_SSM = 512
N_CONV_PROJ = 3
CONV_WIDTH = 3
SSM_GROUP_CH = 16
SSM_GROUPS = 32
SSM_STATE = 64
D_FF = 2816
N_MOD = 9
EPS = 1e-6

SUBLANES = 8
LANES = 128
VMEM_LIMIT_BYTES = 56 * 1024 * 1024

MOD_COLS = 3072
MOD_ROWS = SUBLANES
FFN_ROWS = 1024
FFN_SUB = 256
FFN_COLS = 256
MIX_ROWS = 1024
MIX_SUB = 256
SSM_CHUNK = 4
GROUPS_PER_SLAB = LANES // SSM_GROUP_CH
N_SLABS = D_SSM // LANES
SLAB_STATE = GROUPS_PER_SLAB * SSM_STATE
SCAN_STEPS = (1, 2, 4)

BF16 = jnp.bfloat16
F32 = jnp.float32


def _dot(a, b):
    return jnp.dot(a, b, preferred_element_type=F32)


def _sigmoid(x):
    return 1.0 / (1.0 + jnp.exp(-x))


def _rms_norm(x, gain):
    ms = jnp.mean(x * x, axis=-1, keepdims=True)
    return x * lax.rsqrt(ms + EPS) * gain


def _gelu_tanh(x):
    c = math.sqrt(2.0 / math.pi)
    return x * (0.5 * (1.0 + jnp.tanh(c * (x + 0.044715 * (x * x * x)))))


def _mod_kernel(c_ref, w_ref, b_ref, o_ref):
    c = c_ref[...]
    c = c * _sigmoid(c)
    pad = jnp.zeros((MOD_ROWS - c.shape[0], D_MODEL), F32)
    c = jnp.concatenate([c, pad], axis=0)
    res = _dot(c.astype(BF16), w_ref[...].astype(BF16)) + b_ref[...]
    for i in range(MOD_COLS // D_MODEL):
        o_ref[i] = res[:, i * D_MODEL:(i + 1) * D_MODEL]


def _modulation(cond, w_mod, b_mod):
    bsz = cond.shape[0]
    assert bsz <= MOD_ROWS
    n = w_mod.shape[1]
    tn = MOD_COLS
    return pl.pallas_call(
        _mod_kernel,
        grid=(n // tn,),
        in_specs=[
            pl.BlockSpec((bsz, D_MODEL), lambda j: (0, 0)),
            pl.BlockSpec((D_MODEL, tn), lambda j: (0, j)),
            pl.BlockSpec((1, tn), lambda j: (0, j)),
        ],
        out_specs=pl.BlockSpec((tn // D_MODEL, MOD_ROWS, D_MODEL), lambda j: (j, 0, 0)),
        out_shape=jax.ShapeDtypeStruct((N_MOD, MOD_ROWS, D_MODEL), F32),
        compiler_params=pltpu.CompilerParams(
            dimension_semantics=("arbitrary",), vmem_limit_bytes=VMEM_LIMIT_BYTES),
        name="adaln_mod",
    )(cond, w_mod, b_mod.reshape(1, n))


def _ffn_kernel(*refs, mod_base, final_norm, n_cast):
    x_ref, mod_ref, gain_ref, wg_ref, wu_ref, wd_ref, fgain_ref = refs[:7]
    cast_in = refs[7:7 + n_cast]
    o_ref = refs[7 + n_cast]
    cast_out = refs[8 + n_cast:8 + 2 * n_cast]
    act_ref = refs[8 + 2 * n_cast]
    for src, dst in zip(cast_in, cast_out):
        dst[...] = src[...].astype(BF16)
    batch = pl.ds(pl.program_id(0), 1)
    shift = mod_ref[mod_base, batch, :]
    scale = mod_ref[mod_base + 1, batch, :]
    gate = mod_ref[mod_base + 2, batch, :]
    for r in range(FFN_ROWS // FFN_SUB):
        rows = slice(r * FFN_SUB, (r + 1) * FFN_SUB)
        x = x_ref[rows, :]
        h = (_rms_norm(x, gain_ref[...]) * (1.0 + scale) + shift).astype(BF16)
        for f in range(D_FF // FFN_COLS):
            cols = slice(f * FFN_COLS, (f + 1) * FFN_COLS)
            g = _dot(h, wg_ref[:, cols])
            u = _dot(h, wu_ref[:, cols])
            act_ref[rows, cols] = (g * _sigmoid(g) * u).astype(BF16)
        y = _dot(act_ref[rows, :], wd_ref[...])
        out = x + (0.5 * gate) * y
        if final_norm:
            out = _rms_norm(out, fgain_ref[...])
        o_ref[rows, :] = out


def _const_spec(shape):
    zeros = (0,) * len(shape)
    return pl.BlockSpec(shape, lambda *_: zeros, pipeline_mode=pl.Buffered(1))


def _ffn(x, mod, gain, w_gate, w_up, w_down, final_gain, *, mod_base, final_norm, cast=()):
    bsz, seq, _ = x.shape
    tm = FFN_ROWS
    n_t = seq // tm
    steps = bsz * n_t
    kernel = functools.partial(_ffn_kernel, mod_base=mod_base, final_norm=final_norm,
                               n_cast=len(cast))
    band_specs = [pl.BlockSpec((w.shape[0] // steps, w.shape[1]), lambda b, t: (b * n_t + t, 0))
                  for w in cast]
    outs = pl.pallas_call(
        kernel,
        grid=(bsz, n_t),
        in_specs=[
            pl.BlockSpec((None, tm, D_MODEL), lambda b, t: (b, t, 0)),
            _const_spec((N_MOD, MOD_ROWS, D_MODEL)),
            _const_spec((1, D_MODEL)),
            _const_spec((D_MODEL, D_FF)),
            _const_spec((D_MODEL, D_FF)),
            _const_spec((D_FF, D_MODEL)),
            _const_spec((1, D_MODEL)),
        ] + band_specs,
        out_specs=[pl.BlockSpec((None, tm, D_MODEL), lambda b, t: (b, t, 0))] + band_specs,
        out_shape=[jax.ShapeDtypeStruct(x.shape, F32)]
                  + [jax.ShapeDtypeStruct(w.shape, BF16) for w in cast],
        scratch_shapes=[pltpu.VMEM((tm, D_FF), BF16)],
        compiler_params=pltpu.CompilerParams(
            dimension_semantics=("arbitrary", "arbitrary"),
            vmem_limit_bytes=VMEM_LIMIT_BYTES),
        name="ffn_final" if final_norm else "ffn",
    )(x, mod, gain.reshape(1, D_MODEL), w_gate, w_up, w_down,
      final_gain.reshape(1, D_MODEL), *cast)
    return outs[0], tuple(outs[1:])


def _mixer_kernel(x_ref, mod_ref, gain_ref, win_ref, convw_ref, toep_ref, wb_ref, wc_ref,
                  tab_ref, dskip_ref, gluw_ref, glub_ref, gain_c_ref, gain_s_ref, wout_ref,
                  o_ref, zbuf, ubuf, ybuf, hbuf, carry, ymix):
    tm = MIX_ROWS
    chunk = SSM_CHUNK
    m_rows = tm // chunk
    halo = SUBLANES

    @pl.when(pl.program_id(1) == 0)
    def _():
        zbuf[0:halo, :] = jnp.zeros((halo, D_CONV), F32)
        ubuf[:, 0:halo, :] = jnp.zeros((N_SLABS, halo, LANES), F32)
        carry[...] = jnp.zeros(carry.shape, F32)

    x = x_ref[...]
    batch = pl.ds(pl.program_id(0), 1)
    shift = mod_ref[3, batch, :]
    scale = mod_ref[4, batch, :]
    gate = mod_ref[5, batch, :]
    h = (_rms_norm(x, gain_ref[...]) * (1.0 + scale) + shift).astype(BF16)


    u = _dot(h, win_ref[:, 3 * D_CONV:])
    for q in range(N_SLABS):
        ubuf[q, halo:halo + tm, :] = u[:, q * LANES:(q + 1) * LANES]
    u_curs, u_prvs = [], []
    for q in range(N_SLABS):
        cur = [ubuf[q, pl.ds(halo + s, m_rows, stride=chunk), :] for s in range(chunk)]
        prv = [ubuf[q, pl.ds(halo - chunk + s, m_rows, stride=chunk), :] for s in range(chunk)]
        u_curs.append(jnp.concatenate(cur, axis=1).astype(BF16))
        u_prvs.append(jnp.concatenate(prv, axis=1).astype(BF16))
        ubuf[q, 0:halo, :] = ubuf[q, tm:tm + halo, :]

    def chunk_outputs(q):
        y2 = _dot(u_curs[q], toep_ref[q]) + _dot(hbuf[q].astype(BF16), wc_ref[q])
        for s in range(chunk):
            ybuf[q, pl.ds(s, m_rows, stride=chunk), :] = y2[:, s * LANES:(s + 1) * LANES]

    conv_proj = []
    for q in range(N_SLABS):
        inj = _dot(u_prvs[q], wb_ref[q])
        if q < N_CONV_PROJ:
            conv_proj.append(_dot(h, win_ref[:, q * D_CONV:(q + 1) * D_CONV]))
        for i in range(SLAB_STATE // LANES):
            lanes_re = slice(i * LANES, (i + 1) * LANES)
            lanes_im = slice(SLAB_STATE + i * LANES, SLAB_STATE + (i + 1) * LANES)
            re = inj[:, lanes_re].reshape(m_rows // SUBLANES, SUBLANES, LANES)
            im = inj[:, lanes_im].reshape(m_rows // SUBLANES, SUBLANES, LANES)
            for k, step in enumerate(SCAN_STEPS):
                a_re = tab_ref[q, 2 * k, :, lanes_re]
                a_im = tab_ref[q, 2 * k + 1, :, lanes_re]
                s_re = pltpu.roll(re, step, axis=1)
                s_im = pltpu.roll(im, step, axis=1)
                re, im = re + a_re * s_re - a_im * s_im, im + a_re * s_im + a_im * s_re
            p_re = tab_ref[q, 2 * len(SCAN_STEPS), :, lanes_re]
            p_im = tab_ref[q, 2 * len(SCAN_STEPS) + 1, :, lanes_re]
            c_re = carry[q:q + 1, lanes_re]
            c_im = carry[q:q + 1, lanes_im]
            for r in range(m_rows // SUBLANES):
                rows = slice(r * SUBLANES, (r + 1) * SUBLANES)
                b_re = jnp.broadcast_to(c_re, (SUBLANES, LANES))
                b_im = jnp.broadcast_to(c_im, (SUBLANES, LANES))
                f_re = re[r] + p_re * b_re - p_im * b_im
                f_im = im[r] + p_re * b_im + p_im * b_re
                hbuf[q, rows, lanes_re] = f_re
                hbuf[q, rows, lanes_im] = f_im
                c_re = f_re[SUBLANES - 1:SUBLANES, :]
                c_im = f_im[SUBLANES - 1:SUBLANES, :]
            carry[q:q + 1, lanes_re] = c_re
            carry[q:q + 1, lanes_im] = c_im

    for q in range(N_SLABS):
        chunk_outputs(q)
    b_gate, c_gate, v = conv_proj
    z = c_gate * v
    zbuf[halo:halo + tm, :] = z
    z1 = zbuf[halo - 1:halo - 1 + tm, :]
    z2 = zbuf[halo - 2:halo - 2 + tm, :]
    y_conv = b_gate * (convw_ref[0:1, :] * z2 + convw_ref[1:2, :] * z1 + convw_ref[2:3, :] * z)
    zbuf[0:halo, :] = zbuf[tm:tm + halo, :]
    ymix[:, 0:D_CONV] = _rms_norm(y_conv, gain_c_ref[...]).astype(BF16)

    for r in range(MIX_ROWS // MIX_SUB):
        rows = slice(r * MIX_SUB, (r + 1) * MIX_SUB)
        u_rows = slice(halo + r * MIX_SUB, halo + (r + 1) * MIX_SUB)
        y = jnp.concatenate([ybuf[q, rows, :] for q in range(N_SLABS)], axis=1)
        y = y + dskip_ref[...] * jnp.concatenate([ubuf[q, u_rows, :] for q in range(N_SLABS)], axis=1)
        y = _gelu_tanh(y)
        y = y * _sigmoid(_dot(y.astype(BF16), gluw_ref[...]) + glub_ref[...])
        ymix[rows, D_CONV:] = _rms_norm(y, gain_s_ref[...]).astype(BF16)
        o_ref[rows, :] = x_ref[rows, :] + gate * _dot(ymix[rows, :], wout_ref[...])


def _mixer(x, mod, gain, w_in, conv_w, toep, wb, wc, tab, d_skip, glu_w, glu_b,
           gain_c, gain_s, w_out):
    bsz, seq, _ = x.shape
    tm = MIX_ROWS
    m_rows = tm // SSM_CHUNK
    return pl.pallas_call(
        _mixer_kernel,
        grid=(bsz, seq // tm),
        in_specs=[
            pl.BlockSpec((None, tm, D_MODEL), lambda b, t: (b, t, 0)),
            _const_spec((N_MOD, MOD_ROWS, D_MODEL)),
            _const_spec((1, D_MODEL)),
            _const_spec(w_in.shape),
            _const_spec(conv_w.shape),
            _const_spec(toep.shape),
            _const_spec(wb.shape),
            _const_spec(wc.shape),
            _const_spec(tab.shape),
            _const_spec((1, D_SSM)),
            _const_spec(glu_w.shape),
            _const_spec((1, D_SSM)),
            _const_spec((1, D_CONV)),
            _const_spec((1, D_SSM)),
            _const_spec(w_out.shape),
        ],
        out_specs=pl.BlockSpec((None, tm, D_MODEL), lambda b, t: (b, t, 0)),
        out_shape=jax.ShapeDtypeStruct(x.shape, F32),
        scratch_shapes=[
            pltpu.VMEM((SUBLANES + tm, D_CONV), F32),
            pltpu.VMEM((N_SLABS, SUBLANES + tm, LANES), F32),
            pltpu.VMEM((N_SLABS, tm, LANES), F32),
            pltpu.VMEM((N_SLABS, m_rows, 2 * SLAB_STATE), F32),
            pltpu.VMEM((N_SLABS, 2 * SLAB_STATE), F32),
            pltpu.VMEM((tm, D_MODEL), BF16),
        ],
        compiler_params=pltpu.CompilerParams(
            dimension_semantics=("arbitrary", "arbitrary"),
            vmem_limit_bytes=VMEM_LIMIT_BYTES),
        name="mixer",
    )(x, mod, gain.reshape(1, D_MODEL), w_in, conv_w, toep, wb, wc, tab,
      d_skip.reshape(1, D_SSM), glu_w, glu_b.reshape(1, D_SSM),
      gain_c.reshape(1, D_CONV), gain_s.reshape(1, D_SSM), w_out)


def _cmul(ar, ai, br, bi):
    return ar * br - ai * bi, ar * bi + ai * br


def _ssm_ops_kernel(bt_ref, ct_ref, base_ref, toep_ref, wb_ref, wc_ref, tab_ref):
    L = SSM_CHUNK
    bar = base_ref[0, 0:1, :], base_ref[1, 0:1, :]
    coef = base_ref[0, 1:2, :], base_ref[1, 1:2, :]
    pows = [(jnp.ones((1, SLAB_STATE), F32), jnp.zeros((1, SLAB_STATE), F32))]
    for _ in range(SUBLANES * L):
        pows.append(_cmul(*pows[-1], *bar))

    row = lax.broadcasted_iota(jnp.int32, (SUBLANES, SLAB_STATE), 0)
    for k, step in enumerate(SCAN_STEPS):
        for part in range(2):
            tab_ref[2 * k + part] = jnp.where(row >= step, pows[L * step][part], 0.0)
    for part in range(2):
        plane = jnp.zeros((SUBLANES, SLAB_STATE), F32)
        for t in range(SUBLANES):
            plane = jnp.where(row == t, pows[L * (t + 1)][part], plane)
        tab_ref[2 * len(SCAN_STEPS) + part] = plane

    shape = (LANES, SLAB_STATE)
    row_group = lax.broadcasted_iota(jnp.int32, shape, 0) >> (SSM_GROUP_CH.bit_length() - 1)
    col_group = lax.broadcasted_iota(jnp.int32, shape, 1) >> (SSM_STATE.bit_length() - 1)
    same_group = row_group == col_group
    tile_p = (lax.broadcasted_iota(jnp.int32, (SSM_STATE, SLAB_STATE), 0)
              == (lax.broadcasted_iota(jnp.int32, (SSM_STATE, SLAB_STATE), 1) & (SSM_STATE - 1))
              ).astype(BF16)

    def split(a):
        head = a.astype(BF16)
        return head, (a - head.astype(F32)).astype(BF16)

    def block_diag(a):
        head, rest = split(a)
        return jnp.where(same_group, _dot(head, tile_p) + _dot(rest, tile_p), 0.0)

    bt_re, bt_im = block_diag(bt_ref[0]), block_diag(bt_ref[1])
    ct_re, ct_im = block_diag(ct_ref[0]), block_diag(ct_ref[1])
    ct_re_parts, ct_im_parts = split(ct_re), split(ct_im)

    def dot_nt(a, b_parts):
        nt = (((1,), (1,)), ((), ()))
        a_head, a_rest = split(a)
        b_head, b_rest = b_parts
        return (lax.dot_general(a_head, b_head, nt, preferred_element_type=F32)
                + lax.dot_general(a_rest, b_head, nt, preferred_element_type=F32)
                + lax.dot_general(a_head, b_rest, nt, preferred_element_type=F32))

    taps = []
    for m in range(L):
        s = L - 1 - m
        w_re, w_im = _cmul(bt_re, bt_im, *_cmul(*coef, *pows[m]))
        wb_ref[s * LANES:(s + 1) * LANES, 0:SLAB_STATE] = w_re.astype(BF16)
        wb_ref[s * LANES:(s + 1) * LANES, SLAB_STATE:] = w_im.astype(BF16)
        taps.append(dot_nt(w_re, ct_re_parts) - dot_nt(w_im, ct_im_parts))
    for s in range(L):
        for t in range(L):
            blk = taps[t - s] if t >= s else jnp.zeros((LANES, LANES), F32)
            toep_ref[s * LANES:(s + 1) * LANES, t * LANES:(t + 1) * LANES] = blk.astype(BF16)
    for t in range(L):
        o_re, o_im = _cmul(ct_re, ct_im, *pows[t + 1])
        wc_ref[0:SLAB_STATE, t * LANES:(t + 1) * LANES] = o_re.T.astype(BF16)
        wc_ref[SLAB_STATE:, t * LANES:(t + 1) * LANES] = (-o_im).T.astype(BF16)


def _ssm_operators(lambda_re, lambda_im, log_dt, b_re, b_im, c_re, c_im):
    L = SSM_CHUNK
    lre, lim = lambda_re.astype(F32), lambda_im.astype(F32)
    dt = jnp.exp(log_dt.astype(F32))[:, None]
    mag = jnp.exp(lre * dt)
    bar_re, bar_im = mag * jnp.cos(lim * dt), mag * jnp.sin(lim * dt)
    den = lre * lre + lim * lim
    coef_re = ((bar_re - 1.0) * lre + bar_im * lim) / den
    coef_im = (bar_im * lre - (bar_re - 1.0) * lim) / den

    def slab(a):
        return a.reshape(N_SLABS, SLAB_STATE)

    zero = jnp.zeros((N_SLABS, SUBLANES - 2, SLAB_STATE), F32)
    base = jnp.stack([
        jnp.concatenate([slab(bar_re)[:, None], slab(coef_re)[:, None], zero], axis=1),
        jnp.concatenate([slab(bar_im)[:, None], slab(coef_im)[:, None], zero], axis=1)], axis=1)

    def compact(re, im):
        return jnp.stack([re, im], axis=1).astype(F32).reshape(
            N_SLABS, GROUPS_PER_SLAB, 2, SSM_GROUP_CH, SSM_STATE).transpose(0, 2, 1, 3, 4).reshape(
            N_SLABS, 2, LANES, SSM_STATE)

    bt = compact(b_re.transpose(0, 2, 1), b_im.transpose(0, 2, 1))
    ct = compact(c_re, c_im)

    n_planes = 2 * (len(SCAN_STEPS) + 1)
    return pl.pallas_call(
        _ssm_ops_kernel,
        grid=(N_SLABS,),
        in_specs=[
            pl.BlockSpec((None, 2, LANES, SSM_STATE), lambda q: (q, 0, 0, 0)),
            pl.BlockSpec((None, 2, LANES, SSM_STATE), lambda q: (q, 0, 0, 0)),
            pl.BlockSpec((None, 2, SUBLANES, SLAB_STATE), lambda q: (q, 0, 0, 0)),
        ],
        out_specs=[
            pl.BlockSpec((None, L * LANES, L * LANES), lambda q: (q, 0, 0)),
            pl.BlockSpec((None, L * LANES, 2 * SLAB_STATE), lambda q: (q, 0, 0)),
            pl.BlockSpec((None, 2 * SLAB_STATE, L * LANES), lambda q: (q, 0, 0)),
            pl.BlockSpec((None, n_planes, SUBLANES, SLAB_STATE), lambda q: (q, 0, 0, 0)),
        ],
        out_shape=[
            jax.ShapeDtypeStruct((N_SLABS, L * LANES, L * LANES), BF16),
            jax.ShapeDtypeStruct((N_SLABS, L * LANES, 2 * SLAB_STATE), BF16),
            jax.ShapeDtypeStruct((N_SLABS, 2 * SLAB_STATE, L * LANES), BF16),
            jax.ShapeDtypeStruct((N_SLABS, n_planes, SUBLANES, SLAB_STATE), F32),
        ],
        compiler_params=pltpu.CompilerParams(
            dimension_semantics=("arbitrary",), vmem_limit_bytes=VMEM_LIMIT_BYTES),
        name="ssm_ops",
    )(bt, ct, base)


def kernel(x, cond, w_mod, b_mod, ffn1_norm, ffn1_w_gate, ffn1_w_up, ffn1_w_down, mix_norm, w_in, conv_w, lambda_re, lambda_im, log_dt, ssm_b_re, ssm_b_im, ssm_c_re, ssm_c_im, ssm_d, glu_w, glu_b, out_norm_conv, out_norm_ssm, w_out, ffn2_norm, ffn2_w_gate, ffn2_w_up, ffn2_w_down, final_norm):
    depth = w_mod.shape[0]
    _, seq, d_model = x.shape
    assert d_model == D_MODEL and seq % FFN_ROWS == 0 and seq % MIX_ROWS == 0
    assert w_mod.shape[1:] == (D_MODEL, N_MOD * D_MODEL) and ffn1_w_gate.shape[1:] == (D_MODEL, D_FF)
    assert w_in.shape[1:] == (D_MODEL, N_CONV_PROJ * D_CONV + D_SSM)
    assert conv_w.shape[1:] == (CONV_WIDTH, D_CONV)
    assert lambda_re.shape[1:] == (SSM_GROUPS, SSM_STATE) and SSM_GROUPS * SSM_GROUP_CH == D_SSM
    for l in range(depth):
        mod = _modulation(cond, w_mod[l], b_mod[l])
        later = (w_in[l], glu_w[l], w_out[l], ffn2_w_gate[l], ffn2_w_up[l], ffn2_w_down[l])
        x, (w_in_b, glu_w_b, w_out_b, wg2, wu2, wd2) = _ffn(
            x, mod, ffn1_norm[l], ffn1_w_gate[l].astype(BF16), ffn1_w_up[l].astype(BF16),
            ffn1_w_down[l].astype(BF16), final_norm, mod_base=0, final_norm=False, cast=later)
        toep, wb, wc, tab = _ssm_operators(lambda_re[l], lambda_im[l], log_dt[l], ssm_b_re[l],
                                           ssm_b_im[l], ssm_c_re[l], ssm_c_im[l])
        x = _mixer(x, mod, mix_norm[l], w_in_b, conv_w[l], toep, wb, wc, tab,
                   ssm_d[l], glu_w_b, glu_b[l], out_norm_conv[l], out_norm_ssm[l], w_out_b)
        x, _ = _ffn(x, mod, ffn2_norm[l], wg2, wu2, wd2, final_norm, mod_base=6,
                    final_norm=(l == depth - 1))
    return x
```

```python
import functools
import math

import jax
import jax.numpy as jnp
from jax import lax
from jax.experimental import pallas as pl
from jax.experimental.pallas import tpu as pltpu

D_MODEL = 1024
D_CONV = 512
D_SSM = 512
N_CONV_PROJ = 3
CONV_WIDTH = 3
SSM_GROUP_CH = 16
SSM_GROUPS = 32
SSM_STATE = 64
D_FF = 2816
N_MOD = 9
EPS = 1e-6

SUBLANES = 8
LANES = 128
VMEM_LIMIT_BYTES = 56 * 1024 * 1024

MOD_COLS = 3072
MOD_ROWS = SUBLANES
FFN_ROWS = 1024
FFN_SUB = 256
FFN_COLS = 256
MIX_ROWS = 1024
MIX_SUB = 256
SSM_CHUNK = 4
GROUPS_PER_SLAB = LANES // SSM_GROUP_CH
N_SLABS = D_SSM // LANES
SLAB_STATE = GROUPS_PER_SLAB * SSM_STATE
SCAN_STEPS = (1, 2, 4)

BF16 = jnp.bfloat16
F32 = jnp.float32


def _dot(a, b):
    return jnp.dot(a, b, preferred_element_type=F32)


def _sigmoid(x):
    return 1.0 / (1.0 + jnp.exp(-x))


def _rms_norm(x, gain):
    ms = jnp.mean(x * x, axis=-1, keepdims=True)
    return x * lax.rsqrt(ms + EPS) * gain


def _gelu_tanh(x):
    c = math.sqrt(2.0 / math.pi)
    return x * (0.5 * (1.0 + jnp.tanh(c * (x + 0.044715 * (x * x * x)))))


def _mod_kernel(c_ref, w_ref, b_ref, o_ref):
    c = c_ref[...]
    c = c * _sigmoid(c)
    pad = jnp.zeros((MOD_ROWS - c.shape[0], D_MODEL), F32)
    c = jnp.concatenate([c, pad], axis=0)
    res = _dot(c.astype(BF16), w_ref[...].astype(BF16)) + b_ref[...]
    for i in range(MOD_COLS // D_MODEL):
        o_ref[i] = res[:, i * D_MODEL:(i + 1) * D_MODEL]


def _modulation(cond, w_mod, b_mod):
    bsz = cond.shape[0]
    assert bsz <= MOD_ROWS
    n = w_mod.shape[1]
    tn = MOD_COLS
    return pl.pallas_call(
        _mod_kernel,
        grid=(n // tn,),
        in_specs=[
            pl.BlockSpec((bsz, D_MODEL), lambda j: (0, 0)),
            pl.BlockSpec((D_MODEL, tn), lambda j: (0, j)),
            pl.BlockSpec((1, tn), lambda j: (0, j)),
        ],
        out_specs=pl.BlockSpec((tn // D_MODEL, MOD_ROWS, D_MODEL), lambda j: (j, 0, 0)),
        out_shape=jax.ShapeDtypeStruct((N_MOD, MOD_ROWS, D_MODEL), F32),
        compiler_params=pltpu.CompilerParams(
            dimension_semantics=("arbitrary",), vmem_limit_bytes=VMEM_LIMIT_BYTES),
        name="adaln_mod",
    )(cond, w_mod, b_mod.reshape(1, n))


def _ffn_kernel(*refs, mod_base, final_norm, n_cast):
    x_ref, mod_ref, gain_ref, wg_ref, wu_ref, wd_ref, fgain_ref = refs[:7]
    cast_in = refs[7:7 + n_cast]
    o_ref = refs[7 + n_cast]
    cast_out = refs[8 + n_cast:8 + 2 * n_cast]
    act_ref = refs[8 + 2 * n_cast]
    for src, dst in zip(cast_in, cast_out):
        dst[...] = src[...].astype(BF16)
    batch = pl.ds(pl.program_id(0), 1)
    shift = mod_ref[mod_base, batch, :]
    scale = mod_ref[mod_base + 1, batch, :]
    gate = mod_ref[mod_base + 2, batch, :]
    for r in range(FFN_ROWS // FFN_SUB):
        rows = slice(r * FFN_SUB, (r + 1) * FFN_SUB)
        x = x_ref[rows, :]
        h = (_rms_norm(x, gain_ref[...]) * (1.0 + scale) + shift).astype(BF16)
        for f in range(D_FF // FFN_COLS):
            cols = slice(f * FFN_COLS, (f + 1) * FFN_COLS)
            g = _dot(h, wg_ref[:, cols])
            u = _dot(h, wu_ref[:, cols])
            act_ref[rows, cols] = (g * _sigmoid(g) * u).astype(BF16)
        y = _dot(act_ref[rows, :], wd_ref[...])
        out = x + (0.5 * gate) * y
        if final_norm:
            out = _rms_norm(out, fgain_ref[...])
        o_ref[rows, :] = out


def _const_spec(shape):
    zeros = (0,) * len(shape)
    return pl.BlockSpec(shape, lambda *_: zeros, pipeline_mode=pl.Buffered(1))


def _ffn(x, mod, gain, w_gate, w_up, w_down, final_gain, *, mod_base, final_norm, cast=()):
    bsz, seq, _ = x.shape
    tm = FFN_ROWS
    n_t = seq // tm
    steps = bsz * n_t
    kernel = functools.partial(_ffn_kernel, mod_base=mod_base, final_norm=final_norm,
                               n_cast=len(cast))
    band_specs = [pl.BlockSpec((w.shape[0] // steps, w.shape[1]), lambda b, t: (b * n_t + t, 0))
                  for w in cast]
    outs = pl.pallas_call(
        kernel,
        grid=(bsz, n_t),
        in_specs=[
            pl.BlockSpec((None, tm, D_MODEL), lambda b, t: (b, t, 0)),
            _const_spec((N_MOD, MOD_ROWS, D_MODEL)),
            _const_spec((1, D_MODEL)),
            _const_spec((D_MODEL, D_FF)),
            _const_spec((D_MODEL, D_FF)),
            _const_spec((D_FF, D_MODEL)),
            _const_spec((1, D_MODEL)),
        ] + band_specs,
        out_specs=[pl.BlockSpec((None, tm, D_MODEL), lambda b, t: (b, t, 0))] + band_specs,
        out_shape=[jax.ShapeDtypeStruct(x.shape, F32)]
                  + [jax.ShapeDtypeStruct(w.shape, BF16) for w in cast],
        scratch_shapes=[pltpu.VMEM((tm, D_FF), BF16)],
        compiler_params=pltpu.CompilerParams(
            dimension_semantics=("arbitrary", "arbitrary"),
            vmem_limit_bytes=VMEM_LIMIT_BYTES),
        name="ffn_final" if final_norm else "ffn",
    )(x, mod, gain.reshape(1, D_MODEL), w_gate, w_up, w_down,
      final_gain.reshape(1, D_MODEL), *cast)
    return outs[0], tuple(outs[1:])


def _mixer_kernel(x_ref, mod_ref, gain_ref, win_ref, convw_ref, toep_ref, wb_ref, wc_ref,
                  tab_ref, dskip_ref, gluw_ref, glub_ref, gain_c_ref, gain_s_ref, wout_ref,
                  o_ref, zbuf, ubuf, ybuf, hbuf, carry, ymix):
    tm = MIX_ROWS
    chunk = SSM_CHUNK
    m_rows = tm // chunk
    halo = SUBLANES

    @pl.when(pl.program_id(1) == 0)
    def _():
        zbuf[0:halo, :] = jnp.zeros((halo, D_CONV), F32)
        ubuf[:, 0:halo, :] = jnp.zeros((N_SLABS, halo, LANES), F32)
        carry[...] = jnp.zeros(carry.shape, F32)

    x = x_ref[...]
    batch = pl.ds(pl.program_id(0), 1)
    shift = mod_ref[3, batch, :]
    scale = mod_ref[4, batch, :]
    gate = mod_ref[5, batch, :]
    h = (_rms_norm(x, gain_ref[...]) * (1.0 + scale) + shift).astype(BF16)


    u = _dot(h, win_ref[:, 3 * D_CONV:])
    for q in range(N_SLABS):
        ubuf[q, halo:halo + tm, :] = u[:, q * LANES:(q + 1) * LANES]
    u_curs, u_prvs = [], []
    for q in range(N_SLABS):
        cur = [ubuf[q, pl.ds(halo + s, m_rows, stride=chunk), :] for s in range(chunk)]
        prv = [ubuf[q, pl.ds(halo - chunk + s, m_rows, stride=chunk), :] for s in range(chunk)]
        u_curs.append(jnp.concatenate(cur, axis=1).astype(BF16))
        u_prvs.append(jnp.concatenate(prv, axis=1).astype(BF16))
        ubuf[q, 0:halo, :] = ubuf[q, tm:tm + halo, :]

    def chunk_outputs(q):
        y2 = _dot(u_curs[q], toep_ref[q]) + _dot(hbuf[q].astype(BF16), wc_ref[q])
        for s in range(chunk):
            ybuf[q, pl.ds(s, m_rows, stride=chunk), :] = y2[:, s * LANES:(s + 1) * LANES]

    conv_proj = []
    for q in range(N_SLABS):
        inj = _dot(u_prvs[q], wb_ref[q])
        if q < N_CONV_PROJ:
            conv_proj.append(_dot(h, win_ref[:, q * D_CONV:(q + 1) * D_CONV]))
        for i in range(SLAB_STATE // LANES):
            lanes_re = slice(i * LANES, (i + 1) * LANES)
            lanes_im = slice(SLAB_STATE + i * LANES, SLAB_STATE + (i + 1) * LANES)
            re = inj[:, lanes_re].reshape(m_rows // SUBLANES, SUBLANES, LANES)
            im = inj[:, lanes_im].reshape(m_rows // SUBLANES, SUBLANES, LANES)
            for k, step in enumerate(SCAN_STEPS):
                a_re = tab_ref[q, 2 * k, :, lanes_re]
                a_im = tab_ref[q, 2 * k + 1, :, lanes_re]
                s_re = pltpu.roll(re, step, axis=1)
                s_im = pltpu.roll(im, step, axis=1)
                re, im = re + a_re * s_re - a_im * s_im, im + a_re * s_im + a_im * s_re
            p_re = tab_ref[q, 2 * len(SCAN_STEPS), :, lanes_re]
            p_im = tab_ref[q, 2 * len(SCAN_STEPS) + 1, :, lanes_re]
            c_re = carry[q:q + 1, lanes_re]
            c_im = carry[q:q + 1, lanes_im]
            for r in range(m_rows // SUBLANES):
                rows = slice(r * SUBLANES, (r + 1) * SUBLANES)
                b_re = jnp.broadcast_to(c_re, (SUBLANES, LANES))
                b_im = jnp.broadcast_to(c_im, (SUBLANES, LANES))
                f_re = re[r] + p_re * b_re - p_im * b_im
                f_im = im[r] + p_re * b_im + p_im * b_re
                hbuf[q, rows, lanes_re] = f_re
                hbuf[q, rows, lanes_im] = f_im
                c_re = f_re[SUBLANES - 1:SUBLANES, :]
                c_im = f_im[SUBLANES - 1:SUBLANES, :]
            carry[q:q + 1, lanes_re] = c_re
            carry[q:q + 1, lanes_im] = c_im

    for q in range(N_SLABS):
        chunk_outputs(q)
    b_gate, c_gate, v = conv_proj
    z = c_gate * v
    zbuf[halo:halo + tm, :] = z
    z1 = zbuf[halo - 1:halo - 1 + tm, :]
    z2 = zbuf[halo - 2:halo - 2 + tm, :]
    y_conv = b_gate * (convw_ref[0:1, :] * z2 + convw_ref[1:2, :] * z1 + convw_ref[2:3, :] * z)
    zbuf[0:halo, :] = zbuf[tm:tm + halo, :]
    ymix[:, 0:D_CONV] = _rms_norm(y_conv, gain_c_ref[...]).astype(BF16)

    for r in range(MIX_ROWS // MIX_SUB):
        rows = slice(r * MIX_SUB, (r + 1) * MIX_SUB)
        u_rows = slice(halo + r * MIX_SUB, halo + (r + 1) * MIX_SUB)
        y = jnp.concatenate([ybuf[q, rows, :] for q in range(N_SLABS)], axis=1)
        y = y + dskip_ref[...] * jnp.concatenate([ubuf[q, u_rows, :] for q in range(N_SLABS)], axis=1)
        y = _gelu_tanh(y)
        y = y * _sigmoid(_dot(y.astype(BF16), gluw_ref[...]) + glub_ref[...])
        ymix[rows, D_CONV:] = _rms_norm(y, gain_s_ref[...]).astype(BF16)
        o_ref[rows, :] = x_ref[rows, :] + gate * _dot(ymix[rows, :], wout_ref[...])


def _mixer(x, mod, gain, w_in, conv_w, toep, wb, wc, tab, d_skip, glu_w, glu_b,
           gain_c, gain_s, w_out):
    bsz, seq, _ = x.shape
    tm = MIX_ROWS
    m_rows = tm // SSM_CHUNK
    return pl.pallas_call(
        _mixer_kernel,
        grid=(bsz, seq // tm),
        in_specs=[
            pl.BlockSpec((None, tm, D_MODEL), lambda b, t: (b, t, 0)),
            _const_spec((N_MOD, MOD_ROWS, D_MODEL)),
            _const_spec((1, D_MODEL)),
            _const_spec(w_in.shape),
            _const_spec(conv_w.shape),
            _const_spec(toep.shape),
            _const_spec(wb.shape),
            _const_spec(wc.shape),
            _const_spec(tab.shape),
            _const_spec((1, D_SSM)),
            _const_spec(glu_w.shape),
            _const_spec((1, D_SSM)),
            _const_spec((1, D_CONV)),
            _const_spec((1, D_SSM)),
            _const_spec(w_out.shape),
        ],
        out_specs=pl.BlockSpec((None, tm, D_MODEL), lambda b, t: (b, t, 0)),
        out_shape=jax.ShapeDtypeStruct(x.shape, F32),
        scratch_shapes=[
            pltpu.VMEM((SUBLANES + tm, D_CONV), F32),
            pltpu.VMEM((N_SLABS, SUBLANES + tm, LANES), F32),
            pltpu.VMEM((N_SLABS, tm, LANES), F32),
            pltpu.VMEM((N_SLABS, m_rows, 2 * SLAB_STATE), F32),
            pltpu.VMEM((N_SLABS, 2 * SLAB_STATE), F32),
            pltpu.VMEM((tm, D_MODEL), BF16),
        ],
        compiler_params=pltpu.CompilerParams(
            dimension_semantics=("arbitrary", "arbitrary"),
            vmem_limit_bytes=VMEM_LIMIT_BYTES),
        name="mixer",
    )(x, mod, gain.reshape(1, D_MODEL), w_in, conv_w, toep, wb, wc, tab,
      d_skip.reshape(1, D_SSM), glu_w, glu_b.reshape(1, D_SSM),
      gain_c.reshape(1, D_CONV), gain_s.reshape(1, D_SSM), w_out)


def _cmul(ar, ai, br, bi):
    return ar * br - ai * bi, ar * bi + ai * br


def _ssm_ops_kernel(*refs, n_cast):
    bt_ref, ct_ref, base_ref = refs[:3]
    cast_in = refs[3:3 + n_cast]
    toep_ref, wb_ref, wc_ref, tab_ref = refs[3 + n_cast:7 + n_cast]
    cast_out = refs[7 + n_cast:]
    for src, dst in zip(cast_in, cast_out):
        dst[...] = src[...].astype(BF16)
    L = SSM_CHUNK
    bar = base_ref[0, 0:1, :], base_ref[1, 0:1, :]
    coef = base_ref[0, 1:2, :], base_ref[1, 1:2, :]
    pows = [(jnp.ones((1, SLAB_STATE), F32), jnp.zeros((1, SLAB_STATE), F32))]
    for _ in range(SUBLANES * L):
        pows.append(_cmul(*pows[-1], *bar))

    row = lax.broadcasted_iota(jnp.int32, (SUBLANES, SLAB_STATE), 0)
    for k, step in enumerate(SCAN_STEPS):
        for part in range(2):
            tab_ref[2 * k + part] = jnp.where(row >= step, pows[L * step][part], 0.0)
    for part in range(2):
        plane = jnp.zeros((SUBLANES, SLAB_STATE), F32)
        for t in range(SUBLANES):
            plane = jnp.where(row == t, pows[L * (t + 1)][part], plane)
        tab_ref[2 * len(SCAN_STEPS) + part] = plane

    shape = (LANES, SLAB_STATE)
    row_group = lax.broadcasted_iota(jnp.int32, shape, 0) >> (SSM_GROUP_CH.bit_length() - 1)
    col_group = lax.broadcasted_iota(jnp.int32, shape, 1) >> (SSM_STATE.bit_length() - 1)
    same_group = row_group == col_group
    tile_p = (lax.broadcasted_iota(jnp.int32, (SSM_STATE, SLAB_STATE), 0)
              == (lax.broadcasted_iota(jnp.int32, (SSM_STATE, SLAB_STATE), 1) & (SSM_STATE - 1))
              ).astype(BF16)

    def split(a):
        head = a.astype(BF16)
        return head, (a - head.astype(F32)).astype(BF16)

    def block_diag(a):
        head, rest = split(a)
        return jnp.where(same_group, _dot(head, tile_p) + _dot(rest, tile_p), 0.0)

    bt_re, bt_im = block_diag(bt_ref[0]), block_diag(bt_ref[1])
    ct_re, ct_im = block_diag(ct_ref[0]), block_diag(ct_ref[1])
    ct_re_parts, ct_im_parts = split(ct_re), split(ct_im)

    def dot_nt(a, b_parts):
        nt = (((1,), (1,)), ((), ()))
        a_head, a_rest = split(a)
        b_head, b_rest = b_parts
        return (lax.dot_general(a_head, b_head, nt, preferred_element_type=F32)
                + lax.dot_general(a_rest, b_head, nt, preferred_element_type=F32)
                + lax.dot_general(a_head, b_rest, nt, preferred_element_type=F32))

    taps = []
    for m in range(L):
        s = L - 1 - m
        w_re, w_im = _cmul(bt_re, bt_im, *_cmul(*coef, *pows[m]))
        wb_ref[s * LANES:(s + 1) * LANES, 0:SLAB_STATE] = w_re.astype(BF16)
        wb_ref[s * LANES:(s + 1) * LANES, SLAB_STATE:] = w_im.astype(BF16)
        taps.append(dot_nt(w_re, ct_re_parts) - dot_nt(w_im, ct_im_parts))
    for s in range(L):
        for t in range(L):
            blk = taps[t - s] if t >= s else jnp.zeros((LANES, LANES), F32)
            toep_ref[s * LANES:(s + 1) * LANES, t * LANES:(t + 1) * LANES] = blk.astype(BF16)
    for t in range(L):
        o_re, o_im = _cmul(ct_re, ct_im, *pows[t + 1])
        wc_ref[0:SLAB_STATE, t * LANES:(t + 1) * LANES] = o_re.T.astype(BF16)
        wc_ref[SLAB_STATE:, t * LANES:(t + 1) * LANES] = (-o_im).T.astype(BF16)


def _ssm_operators(lambda_re, lambda_im, log_dt, b_re, b_im, c_re, c_im, cast=()):
    L = SSM_CHUNK
    lre, lim = lambda_re.astype(F32), lambda_im.astype(F32)
    dt = jnp.exp(log_dt.astype(F32))[:, None]
    mag = jnp.exp(lre * dt)
    bar_re, bar_im = mag * jnp.cos(lim * dt), mag * jnp.sin(lim * dt)
    den = lre * lre + lim * lim
    coef_re = ((bar_re - 1.0) * lre + bar_im * lim) / den
    coef_im = (bar_im * lre - (bar_re - 1.0) * lim) / den

    def slab(a):
        return a.reshape(N_SLABS, SLAB_STATE)

    zero = jnp.zeros((N_SLABS, SUBLANES - 2, SLAB_STATE), F32)
    base = jnp.stack([
        jnp.concatenate([slab(bar_re)[:, None], slab(coef_re)[:, None], zero], axis=1),
        jnp.concatenate([slab(bar_im)[:, None], slab(coef_im)[:, None], zero], axis=1)], axis=1)

    def compact(re, im):
        return jnp.stack([re, im], axis=1).astype(F32).reshape(
            N_SLABS, GROUPS_PER_SLAB, 2, SSM_GROUP_CH, SSM_STATE).transpose(0, 2, 1, 3, 4).reshape(
            N_SLABS, 2, LANES, SSM_STATE)

    bt = compact(b_re.transpose(0, 2, 1), b_im.transpose(0, 2, 1))
    ct = compact(c_re, c_im)

    n_planes = 2 * (len(SCAN_STEPS) + 1)
    band_specs = [pl.BlockSpec((w.shape[0] // N_SLABS, w.shape[1]), lambda q: (q, 0)) for w in cast]
    outs = pl.pallas_call(
        functools.partial(_ssm_ops_kernel, n_cast=len(cast)),
        grid=(N_SLABS,),
        in_specs=[
            pl.BlockSpec((None, 2, LANES, SSM_STATE), lambda q: (q, 0, 0, 0)),
            pl.BlockSpec((None, 2, LANES, SSM_STATE), lambda q: (q, 0, 0, 0)),
            pl.BlockSpec((None, 2, SUBLANES, SLAB_STATE), lambda q: (q, 0, 0, 0)),
        ] + band_specs,
        out_specs=[
            pl.BlockSpec((None, L * LANES, L * LANES), lambda q: (q, 0, 0)),
            pl.BlockSpec((None, L * LANES, 2 * SLAB_STATE), lambda q: (q, 0, 0)),
            pl.BlockSpec((None, 2 * SLAB_STATE, L * LANES), lambda q: (q, 0, 0)),
            pl.BlockSpec((None, n_planes, SUBLANES, SLAB_STATE), lambda q: (q, 0, 0, 0)),
        ] + band_specs,
        out_shape=[
            jax.ShapeDtypeStruct((N_SLABS, L * LANES, L * LANES), BF16),
            jax.ShapeDtypeStruct((N_SLABS, L * LANES, 2 * SLAB_STATE), BF16),
            jax.ShapeDtypeStruct((N_SLABS, 2 * SLAB_STATE, L * LANES), BF16),
            jax.ShapeDtypeStruct((N_SLABS, n_planes, SUBLANES, SLAB_STATE), F32),
        ] + [jax.ShapeDtypeStruct(w.shape, BF16) for w in cast],
        compiler_params=pltpu.CompilerParams(
            dimension_semantics=("arbitrary",), vmem_limit_bytes=VMEM_LIMIT_BYTES),
        name="ssm_ops",
    )(bt, ct, base, *cast)
    return outs[:4], tuple(outs[4:])


def kernel(x, cond, w_mod, b_mod, ffn1_norm, ffn1_w_gate, ffn1_w_up, ffn1_w_down, mix_norm, w_in, conv_w, lambda_re, lambda_im, log_dt, ssm_b_re, ssm_b_im, ssm_c_re, ssm_c_im, ssm_d, glu_w, glu_b, out_norm_conv, out_norm_ssm, w_out, ffn2_norm, ffn2_w_gate, ffn2_w_up, ffn2_w_down, final_norm):
    depth = w_mod.shape[0]
    _, seq, d_model = x.shape
    assert d_model == D_MODEL and seq % FFN_ROWS == 0 and seq % MIX_ROWS == 0
    assert w_mod.shape[1:] == (D_MODEL, N_MOD * D_MODEL) and ffn1_w_gate.shape[1:] == (D_MODEL, D_FF)
    assert w_in.shape[1:] == (D_MODEL, N_CONV_PROJ * D_CONV + D_SSM)
    assert conv_w.shape[1:] == (CONV_WIDTH, D_CONV)
    assert lambda_re.shape[1:] == (SSM_GROUPS, SSM_STATE) and SSM_GROUPS * SSM_GROUP_CH == D_SSM
    for l in range(depth):
        mod = _modulation(cond, w_mod[l], b_mod[l])
        (toep, wb, wc, tab), (wg1, wu1, wd1) = _ssm_operators(
            lambda_re[l], lambda_im[l], log_dt[l], ssm_b_re[l], ssm_b_im[l], ssm_c_re[l],
            ssm_c_im[l], cast=(ffn1_w_gate[l], ffn1_w_up[l], ffn1_w_down[l]))
        later = (w_in[l], glu_w[l], w_out[l], ffn2_w_gate[l], ffn2_w_up[l], ffn2_w_down[l])
        x, (w_in_b, glu_w_b, w_out_b, wg2, wu2, wd2) = _ffn(
            x, mod, ffn1_norm[l], wg1, wu1, wd1, final_norm, mod_base=0, final_norm=False,
            cast=later)
        x = _mixer(x, mod, mix_norm[l], w_in_b, conv_w[l], toep, wb, wc, tab,
                   ssm_d[l], glu_w_b, glu_b[l], out_norm_conv[l], out_norm_ssm[l], w_out_b)
        x, _ = _ffn(x, mod, ffn2_norm[l], wg2, wu2, wd2, final_norm, mod_base=6,
                    final_norm=(l == depth - 1))
    return x
```

```python
import functools
import math

import jax
import jax.numpy as jnp
from jax import lax
from jax.experimental import pallas as pl
from jax.experimental.pallas import tpu as pltpu

D_MODEL = 1024
D_CONV = 512
D_SSM = 512
N_CONV_PROJ = 3
CONV_WIDTH = 3
SSM_GROUP_CH = 16
SSM_GROUPS = 32
SSM_STATE = 64
D_FF = 2816
N_MOD = 9
EPS = 1e-6

SUBLANES = 8
LANES = 128
VMEM_LIMIT_BYTES = 56 * 1024 * 1024

MOD_FIRST = 3
MOD_HALF = 512
MOD_ROWS = SUBLANES
FFN_ROWS = 1024
FFN_SUB = 256
FFN_COLS = 256
MIX_ROWS = 1024
MIX_SUB = 256
SSM_CHUNK = 4
GROUPS_PER_SLAB = LANES // SSM_GROUP_CH
N_SLABS = D_SSM // LANES
SLAB_STATE = GROUPS_PER_SLAB * SSM_STATE
SCAN_STEPS = (1, 2, 4)

BF16 = jnp.bfloat16
F32 = jnp.float32


def _dot(a, b):
    return jnp.dot(a, b, preferred_element_type=F32)


def _sigmoid(x):
    return 1.0 / (1.0 + jnp.exp(-x))


def _rms_norm(x, gain):
    ms = jnp.mean(x * x, axis=-1, keepdims=True)
    return x * lax.rsqrt(ms + EPS) * gain


def _gelu_tanh(x):
    c = math.sqrt(2.0 / math.pi)
    return x * (0.5 * (1.0 + jnp.tanh(c * (x + 0.044715 * (x * x * x)))))


def _mod_columns(c_ref, w_ref, b_ref):
    c = c_ref[...]
    c = c * _sigmoid(c)
    pad = jnp.zeros((MOD_ROWS - c.shape[0], D_MODEL), F32)
    c = jnp.concatenate([c, pad], axis=0)
    return _dot(c.astype(BF16), w_ref[...].astype(BF16)) + b_ref[...]


def _mod_kernel(c_ref, w_ref, b_ref, o_ref):
    o_ref[0] = _mod_columns(c_ref, w_ref, b_ref)


def _modulation(cond, w_mod, b_mod, n_planes):
    bsz = cond.shape[0]
    assert bsz <= MOD_ROWS
    return pl.pallas_call(
        _mod_kernel,
        grid=(n_planes,),
        in_specs=[
            pl.BlockSpec((bsz, D_MODEL), lambda j: (0, 0)),
            pl.BlockSpec((D_MODEL, D_MODEL), lambda j: (0, j)),
            pl.BlockSpec((1, D_MODEL), lambda j: (0, j)),
        ],
        out_specs=pl.BlockSpec((1, MOD_ROWS, D_MODEL), lambda j: (j, 0, 0)),
        out_shape=jax.ShapeDtypeStruct((n_planes, MOD_ROWS, D_MODEL), F32),
        compiler_params=pltpu.CompilerParams(
            dimension_semantics=("arbitrary",), vmem_limit_bytes=VMEM_LIMIT_BYTES),
        name="adaln_mod",
    )(cond, w_mod, b_mod.reshape(1, -1))


def _ffn_kernel(*refs, mod_base, final_norm, n_cast, later_mod):
    x_ref, mod_ref, gain_ref, wg_ref, wu_ref, wd_ref, fgain_ref = refs[:7]
    n_extra = n_cast + (3 if later_mod else 0)
    cast_in = refs[7:7 + n_cast]
    o_ref = refs[7 + n_extra]
    cast_out = refs[8 + n_extra:8 + n_extra + n_cast]
    act_ref = refs[-1]
    for src, dst in zip(cast_in, cast_out):
        dst[...] = src[...].astype(BF16)
    if later_mod:
        cond_ref, wmod_ref, bmod_ref = refs[7 + n_cast:7 + n_extra]
        refs[8 + n_extra + n_cast][0] = _mod_columns(cond_ref, wmod_ref, bmod_ref)
    batch = pl.ds(pl.program_id(0), 1)
    shift = mod_ref[mod_base, batch, :]
    scale = mod_ref[mod_base + 1, batch, :]
    gate = mod_ref[mod_base + 2, batch, :]
    for r in range(FFN_ROWS // FFN_SUB):
        rows = slice(r * FFN_SUB, (r + 1) * FFN_SUB)
        x = x_ref[rows, :]
        h = (_rms_norm(x, gain_ref[...]) * (1.0 + scale) + shift).astype(BF16)
        for f in range(D_FF // FFN_COLS):
            cols = slice(f * FFN_COLS, (f + 1) * FFN_COLS)
            g = _dot(h, wg_ref[:, cols])
            u = _dot(h, wu_ref[:, cols])
            act_ref[rows, cols] = (g * _sigmoid(g) * u).astype(BF16)
        y = _dot(act_ref[rows, :], wd_ref[...])
        out = x + (0.5 * gate) * y
        if final_norm:
            out = _rms_norm(out, fgain_ref[...])
        o_ref[rows, :] = out


def _const_spec(shape):
    zeros = (0,) * len(shape)
    return pl.BlockSpec(shape, lambda *_: zeros, pipeline_mode=pl.Buffered(1))


def _ffn(x, mod, gain, w_gate, w_up, w_down, final_gain, *, mod_base, final_norm, cast=(),
         later_mod=None):
    bsz, seq, _ = x.shape
    tm = FFN_ROWS
    n_t = seq // tm
    steps = bsz * n_t
    kernel = functools.partial(_ffn_kernel, mod_base=mod_base, final_norm=final_norm,
                               n_cast=len(cast), later_mod=later_mod is not None)
    band_specs = [pl.BlockSpec((w.shape[0] // steps, w.shape[1]), lambda b, t: (b * n_t + t, 0))
                  for w in cast]
    extra_in, extra_in_specs, extra_out_specs, extra_out_shapes = [], [], [], []
    if later_mod is not None:
        cond, w_mod, b_mod, first_plane = later_mod
        n_planes = N_MOD - first_plane
        halves = D_MODEL // MOD_HALF
        assert n_planes * halves <= steps

        def half(b, t):
            return jnp.minimum(b * n_t + t, n_planes * halves - 1)

        extra_in = [cond, w_mod, b_mod.reshape(1, -1)]
        extra_in_specs = [
            _const_spec(cond.shape),
            pl.BlockSpec((D_MODEL, MOD_HALF), lambda b, t: (0, first_plane * halves + half(b, t))),
            pl.BlockSpec((1, MOD_HALF), lambda b, t: (0, first_plane * halves + half(b, t))),
        ]
        extra_out_specs = [pl.BlockSpec((1, MOD_ROWS, MOD_HALF),
                                        lambda b, t: (half(b, t) // halves, 0, half(b, t) % halves))]
        extra_out_shapes = [jax.ShapeDtypeStruct((n_planes, MOD_ROWS, D_MODEL), F32)]
    outs = pl.pallas_call(
        kernel,
        grid=(bsz, n_t),
        in_specs=[
            pl.BlockSpec((None, tm, D_MODEL), lambda b, t: (b, t, 0)),
            _const_spec(mod.shape),
            _const_spec((1, D_MODEL)),
            _const_spec((D_MODEL, D_FF)),
            _const_spec((D_MODEL, D_FF)),
            _const_spec((D_FF, D_MODEL)),
            _const_spec((1, D_MODEL)),
        ] + band_specs + extra_in_specs,
        out_specs=[pl.BlockSpec((None, tm, D_MODEL), lambda b, t: (b, t, 0))] + band_specs
                  + extra_out_specs,
        out_shape=[jax.ShapeDtypeStruct(x.shape, F32)]
                  + [jax.ShapeDtypeStruct(w.shape, BF16) for w in cast] + extra_out_shapes,
        scratch_shapes=[pltpu.VMEM((tm, D_FF), BF16)],
        compiler_params=pltpu.CompilerParams(
            dimension_semantics=("arbitrary", "arbitrary"),
            vmem_limit_bytes=VMEM_LIMIT_BYTES),
        name="ffn_final" if final_norm else "ffn",
    )(x, mod, gain.reshape(1, D_MODEL), w_gate, w_up, w_down,
      final_gain.reshape(1, D_MODEL), *cast, *extra_in)
    return outs[0], tuple(outs[1:])


def _mixer_kernel(x_ref, mod_ref, gain_ref, win_ref, convw_ref, toep_ref, wb_ref, wc_ref,
                  tab_ref, dskip_ref, gluw_ref, glub_ref, gain_c_ref, gain_s_ref, wout_ref,
                  o_ref, zbuf, ubuf, ybuf, hbuf, carry, ymix):
    tm = MIX_ROWS
    chunk = SSM_CHUNK
    m_rows = tm // chunk
    halo = SUBLANES

    @pl.when(pl.program_id(1) == 0)
    def _():
        zbuf[0:halo, :] = jnp.zeros((halo, D_CONV), F32)
        ubuf[:, 0:halo, :] = jnp.zeros((N_SLABS, halo, LANES), F32)
        carry[...] = jnp.zeros(carry.shape, F32)

    x = x_ref[...]
    batch = pl.ds(pl.program_id(0), 1)
    shift = mod_ref[0, batch, :]
    scale = mod_ref[1, batch, :]
    gate = mod_ref[2, batch, :]
    h = (_rms_norm(x, gain_ref[...]) * (1.0 + scale) + shift).astype(BF16)


    u = _dot(h, win_ref[:, 3 * D_CONV:])
    for q in range(N_SLABS):
        ubuf[q, halo:halo + tm, :] = u[:, q * LANES:(q + 1) * LANES]
    u_curs, u_prvs = [], []
    for q in range(N_SLABS):
        cur = [ubuf[q, pl.ds(halo + s, m_rows, stride=chunk), :] for s in range(chunk)]
        prv = [ubuf[q, pl.ds(halo - chunk + s, m_rows, stride=chunk), :] for s in range(chunk)]
        u_curs.append(jnp.concatenate(cur, axis=1).astype(BF16))
        u_prvs.append(jnp.concatenate(prv, axis=1).astype(BF16))
        ubuf[q, 0:halo, :] = ubuf[q, tm:tm + halo, :]

    def chunk_outputs(q):
        y2 = _dot(u_curs[q], toep_ref[q]) + _dot(hbuf[q].astype(BF16), wc_ref[q])
        for s in range(chunk):
            ybuf[q, pl.ds(s, m_rows, stride=chunk), :] = y2[:, s * LANES:(s + 1) * LANES]

    conv_proj = []
    for q in range(N_SLABS):
        inj = _dot(u_prvs[q], wb_ref[q])
        if q < N_CONV_PROJ:
            conv_proj.append(_dot(h, win_ref[:, q * D_CONV:(q + 1) * D_CONV]))
        for i in range(SLAB_STATE // LANES):
            lanes_re = slice(i * LANES, (i + 1) * LANES)
            lanes_im = slice(SLAB_STATE + i * LANES, SLAB_STATE + (i + 1) * LANES)
            re = inj[:, lanes_re].reshape(m_rows // SUBLANES, SUBLANES, LANES)
            im = inj[:, lanes_im].reshape(m_rows // SUBLANES, SUBLANES, LANES)
            for k, step in enumerate(SCAN_STEPS):
                a_re = tab_ref[q, 2 * k, :, lanes_re]
                a_im = tab_ref[q, 2 * k + 1, :, lanes_re]
                s_re = pltpu.roll(re, step, axis=1)
                s_im = pltpu.roll(im, step, axis=1)
                re, im = re + a_re * s_re - a_im * s_im, im + a_re * s_im + a_im * s_re
            p_re = tab_ref[q, 2 * len(SCAN_STEPS), :, lanes_re]
            p_im = tab_ref[q, 2 * len(SCAN_STEPS) + 1, :, lanes_re]
            c_re = carry[q:q + 1, lanes_re]
            c_im = carry[q:q + 1, lanes_im]
            for r in range(m_rows // SUBLANES):
                rows = slice(r * SUBLANES, (r + 1) * SUBLANES)
                b_re = jnp.broadcast_to(c_re, (SUBLANES, LANES))
                b_im = jnp.broadcast_to(c_im, (SUBLANES, LANES))
                f_re = re[r] + p_re * b_re - p_im * b_im
                f_im = im[r] + p_re * b_im + p_im * b_re
                hbuf[q, rows, lanes_re] = f_re
                hbuf[q, rows, lanes_im] = f_im
                c_re = f_re[SUBLANES - 1:SUBLANES, :]
                c_im = f_im[SUBLANES - 1:SUBLANES, :]
            carry[q:q + 1, lanes_re] = c_re
            carry[q:q + 1, lanes_im] = c_im

    for q in range(N_SLABS):
        chunk_outputs(q)
    b_gate, c_gate, v = conv_proj
    z = c_gate * v
    zbuf[halo:halo + tm, :] = z
    z1 = zbuf[halo - 1:halo - 1 + tm, :]
    z2 = zbuf[halo - 2:halo - 2 + tm, :]
    y_conv = b_gate * (convw_ref[0:1, :] * z2 + convw_ref[1:2, :] * z1 + convw_ref[2:3, :] * z)
    zbuf[0:halo, :] = zbuf[tm:tm + halo, :]
    ymix[:, 0:D_CONV] = _rms_norm(y_conv, gain_c_ref[...]).astype(BF16)

    for r in range(MIX_ROWS // MIX_SUB):
        rows = slice(r * MIX_SUB, (r + 1) * MIX_SUB)
        u_rows = slice(halo + r * MIX_SUB, halo + (r + 1) * MIX_SUB)
        y = jnp.concatenate([ybuf[q, rows, :] for q in range(N_SLABS)], axis=1)
        y = y + dskip_ref[...] * jnp.concatenate([ubuf[q, u_rows, :] for q in range(N_SLABS)], axis=1)
        y = _gelu_tanh(y)
        y = y * _sigmoid(_dot(y.astype(BF16), gluw_ref[...]) + glub_ref[...])
        ymix[rows, D_CONV:] = _rms_norm(y, gain_s_ref[...]).astype(BF16)
        o_ref[rows, :] = x_ref[rows, :] + gate * _dot(ymix[rows, :], wout_ref[...])


def _mixer(x, mod, gain, w_in, conv_w, toep, wb, wc, tab, d_skip, glu_w, glu_b,
           gain_c, gain_s, w_out):
    bsz, seq, _ = x.shape
    tm = MIX_ROWS
    m_rows = tm // SSM_CHUNK
    return pl.pallas_call(
        _mixer_kernel,
        grid=(bsz, seq // tm),
        in_specs=[
            pl.BlockSpec((None, tm, D_MODEL), lambda b, t: (b, t, 0)),
            _const_spec(mod.shape),
            _const_spec((1, D_MODEL)),
            _const_spec(w_in.shape),
            _const_spec(conv_w.shape),
            _const_spec(toep.shape),
            _const_spec(wb.shape),
            _const_spec(wc.shape),
            _const_spec(tab.shape),
            _const_spec((1, D_SSM)),
            _const_spec(glu_w.shape),
            _const_spec((1, D_SSM)),
            _const_spec((1, D_CONV)),
            _const_spec((1, D_SSM)),
            _const_spec(w_out.shape),
        ],
        out_specs=pl.BlockSpec((None, tm, D_MODEL), lambda b, t: (b, t, 0)),
        out_shape=jax.ShapeDtypeStruct(x.shape, F32),
        scratch_shapes=[
            pltpu.VMEM((SUBLANES + tm, D_CONV), F32),
            pltpu.VMEM((N_SLABS, SUBLANES + tm, LANES), F32),
            pltpu.VMEM((N_SLABS, tm, LANES), F32),
            pltpu.VMEM((N_SLABS, m_rows, 2 * SLAB_STATE), F32),
            pltpu.VMEM((N_SLABS, 2 * SLAB_STATE), F32),
            pltpu.VMEM((tm, D_MODEL), BF16),
        ],
        compiler_params=pltpu.CompilerParams(
            dimension_semantics=("arbitrary", "arbitrary"),
            vmem_limit_bytes=VMEM_LIMIT_BYTES),
        name="mixer",
    )(x, mod, gain.reshape(1, D_MODEL), w_in, conv_w, toep, wb, wc, tab,
      d_skip.reshape(1, D_SSM), glu_w, glu_b.reshape(1, D_SSM),
      gain_c.reshape(1, D_CONV), gain_s.reshape(1, D_SSM), w_out)


def _cmul(ar, ai, br, bi):
    return ar * br - ai * bi, ar * bi + ai * br


def _ssm_ops_kernel(*refs, n_cast):
    bt_ref, ct_ref, base_ref = refs[:3]
    cast_in = refs[3:3 + n_cast]
    toep_ref, wb_ref, wc_ref, tab_ref = refs[3 + n_cast:7 + n_cast]
    cast_out = refs[7 + n_cast:]
    for src, dst in zip(cast_in, cast_out):
        dst[...] = src[...].astype(BF16)
    L = SSM_CHUNK
    bar = base_ref[0, 0:1, :], base_ref[1, 0:1, :]
    coef = base_ref[0, 1:2, :], base_ref[1, 1:2, :]
    pows = [(jnp.ones((1, SLAB_STATE), F32), jnp.zeros((1, SLAB_STATE), F32))]
    for _ in range(SUBLANES * L):
        pows.append(_cmul(*pows[-1], *bar))

    row = lax.broadcasted_iota(jnp.int32, (SUBLANES, SLAB_STATE), 0)
    for k, step in enumerate(SCAN_STEPS):
        for part in range(2):
            tab_ref[2 * k + part] = jnp.where(row >= step, pows[L * step][part], 0.0)
    for part in range(2):
        plane = jnp.zeros((SUBLANES, SLAB_STATE), F32)
        for t in range(SUBLANES):
            plane = jnp.where(row == t, pows[L * (t + 1)][part], plane)
        tab_ref[2 * len(SCAN_STEPS) + part] = plane

    shape = (LANES, SLAB_STATE)
    row_group = lax.broadcasted_iota(jnp.int32, shape, 0) >> (SSM_GROUP_CH.bit_length() - 1)
    col_group = lax.broadcasted_iota(jnp.int32, shape, 1) >> (SSM_STATE.bit_length() - 1)
    same_group = row_group == col_group
    tile_p = (lax.broadcasted_iota(jnp.int32, (SSM_STATE, SLAB_STATE), 0)
              == (lax.broadcasted_iota(jnp.int32, (SSM_STATE, SLAB_STATE), 1) & (SSM_STATE - 1))
              ).astype(BF16)

    def split(a):
        head = a.astype(BF16)
        return head, (a - head.astype(F32)).astype(BF16)

    def block_diag(a):
        head, rest = split(a)
        return jnp.where(same_group, _dot(head, tile_p) + _dot(rest, tile_p), 0.0)

    bt_re, bt_im = block_diag(bt_ref[0]), block_diag(bt_ref[1])
    ct_re, ct_im = block_diag(ct_ref[0]), block_diag(ct_ref[1])
    ct_re_parts, ct_im_parts = split(ct_re), split(ct_im)

    def dot_nt(a, b_parts):
        nt = (((1,), (1,)), ((), ()))
        a_head, a_rest = split(a)
        b_head, b_rest = b_parts
        return (lax.dot_general(a_head, b_head, nt, preferred_element_type=F32)
                + lax.dot_general(a_rest, b_head, nt, preferred_element_type=F32)
                + lax.dot_general(a_head, b_rest, nt, preferred_element_type=F32))

    taps = []
    for m in range(L):
        s = L - 1 - m
        w_re, w_im = _cmul(bt_re, bt_im, *_cmul(*coef, *pows[m]))
        wb_ref[s * LANES:(s + 1) * LANES, 0:SLAB_STATE] = w_re.astype(BF16)
        wb_ref[s * LANES:(s + 1) * LANES, SLAB_STATE:] = w_im.astype(BF16)
        taps.append(dot_nt(w_re, ct_re_parts) - dot_nt(w_im, ct_im_parts))
    for s in range(L):
        for t in range(L):
            blk = taps[t - s] if t >= s else jnp.zeros((LANES, LANES), F32)
            toep_ref[s * LANES:(s + 1) * LANES, t * LANES:(t + 1) * LANES] = blk.astype(BF16)
    for t in range(L):
        o_re, o_im = _cmul(ct_re, ct_im, *pows[t + 1])
        wc_ref[0:SLAB_STATE, t * LANES:(t + 1) * LANES] = o_re.T.astype(BF16)
        wc_ref[SLAB_STATE:, t * LANES:(t + 1) * LANES] = (-o_im).T.astype(BF16)


def _ssm_operators(lambda_re, lambda_im, log_dt, b_re, b_im, c_re, c_im, cast=()):
    L = SSM_CHUNK
    lre, lim = lambda_re.astype(F32), lambda_im.astype(F32)
    dt = jnp.exp(log_dt.astype(F32))[:, None]
    mag = jnp.exp(lre * dt)
    bar_re, bar_im = mag * jnp.cos(lim * dt), mag * jnp.sin(lim * dt)
    den = lre * lre + lim * lim
    coef_re = ((bar_re - 1.0) * lre + bar_im * lim) / den
    coef_im = (bar_im * lre - (bar_re - 1.0) * lim) / den

    def slab(a):
        return a.reshape(N_SLABS, SLAB_STATE)

    zero = jnp.zeros((N_SLABS, SUBLANES - 2, SLAB_STATE), F32)
    base = jnp.stack([
        jnp.concatenate([slab(bar_re)[:, None], slab(coef_re)[:, None], zero], axis=1),
        jnp.concatenate([slab(bar_im)[:, None], slab(coef_im)[:, None], zero], axis=1)], axis=1)

    def compact(re, im):
        return jnp.stack([re, im], axis=1).astype(F32).reshape(
            N_SLABS, GROUPS_PER_SLAB, 2, SSM_GROUP_CH, SSM_STATE).transpose(0, 2, 1, 3, 4).reshape(
            N_SLABS, 2, LANES, SSM_STATE)

    bt = compact(b_re.transpose(0, 2, 1), b_im.transpose(0, 2, 1))
    ct = compact(c_re, c_im)

    n_planes = 2 * (len(SCAN_STEPS) + 1)
    band_specs = [pl.BlockSpec((w.shape[0] // N_SLABS, w.shape[1]), lambda q: (q, 0)) for w in cast]
    outs = pl.pallas_call(
        functools.partial(_ssm_ops_kernel, n_cast=len(cast)),
        grid=(N_SLABS,),
        in_specs=[
            pl.BlockSpec((None, 2, LANES, SSM_STATE), lambda q: (q, 0, 0, 0)),
            pl.BlockSpec((None, 2, LANES, SSM_STATE), lambda q: (q, 0, 0, 0)),
            pl.BlockSpec((None, 2, SUBLANES, SLAB_STATE), lambda q: (q, 0, 0, 0)),
        ] + band_specs,
        out_specs=[
            pl.BlockSpec((None, L * LANES, L * LANES), lambda q: (q, 0, 0)),
            pl.BlockSpec((None, L * LANES, 2 * SLAB_STATE), lambda q: (q, 0, 0)),
            pl.BlockSpec((None, 2 * SLAB_STATE, L * LANES), lambda q: (q, 0, 0)),
            pl.BlockSpec((None, n_planes, SUBLANES, SLAB_STATE), lambda q: (q, 0, 0, 0)),
        ] + band_specs,
        out_shape=[
            jax.ShapeDtypeStruct((N_SLABS, L * LANES, L * LANES), BF16),
            jax.ShapeDtypeStruct((N_SLABS, L * LANES, 2 * SLAB_STATE), BF16),
            jax.ShapeDtypeStruct((N_SLABS, 2 * SLAB_STATE, L * LANES), BF16),
            jax.ShapeDtypeStruct((N_SLABS, n_planes, SUBLANES, SLAB_STATE), F32),
        ] + [jax.ShapeDtypeStruct(w.shape, BF16) for w in cast],
        compiler_params=pltpu.CompilerParams(
            dimension_semantics=("arbitrary",), vmem_limit_bytes=VMEM_LIMIT_BYTES),
        name="ssm_ops",
    )(bt, ct, base, *cast)
    return outs[:4], tuple(outs[4:])


def kernel(x, cond, w_mod, b_mod, ffn1_norm, ffn1_w_gate, ffn1_w_up, ffn1_w_down, mix_norm, w_in, conv_w, lambda_re, lambda_im, log_dt, ssm_b_re, ssm_b_im, ssm_c_re, ssm_c_im, ssm_d, glu_w, glu_b, out_norm_conv, out_norm_ssm, w_out, ffn2_norm, ffn2_w_gate, ffn2_w_up, ffn2_w_down, final_norm):
    depth = w_mod.shape[0]
    _, seq, d_model = x.shape
    assert d_model == D_MODEL and seq % FFN_ROWS == 0 and seq % MIX_ROWS == 0
    assert w_mod.shape[1:] == (D_MODEL, N_MOD * D_MODEL) and ffn1_w_gate.shape[1:] == (D_MODEL, D_FF)
    assert w_in.shape[1:] == (D_MODEL, N_CONV_PROJ * D_CONV + D_SSM)
    assert conv_w.shape[1:] == (CONV_WIDTH, D_CONV)
    assert lambda_re.shape[1:] == (SSM_GROUPS, SSM_STATE) and SSM_GROUPS * SSM_GROUP_CH == D_SSM
    for l in range(depth):
        mod1 = _modulation(cond, w_mod[l], b_mod[l], MOD_FIRST)
        (toep, wb, wc, tab), (wg1, wu1, wd1) = _ssm_operators(
            lambda_re[l], lambda_im[l], log_dt[l], ssm_b_re[l], ssm_b_im[l], ssm_c_re[l],
            ssm_c_im[l], cast=(ffn1_w_gate[l], ffn1_w_up[l], ffn1_w_down[l]))
        later = (w_in[l], glu_w[l], w_out[l], ffn2_w_gate[l], ffn2_w_up[l], ffn2_w_down[l])
        x, (w_in_b, glu_w_b, w_out_b, wg2, wu2, wd2, mod2) = _ffn(
            x, mod1, ffn1_norm[l], wg1, wu1, wd1, final_norm, mod_base=0, final_norm=False,
            cast=later, later_mod=(cond, w_mod[l], b_mod[l], MOD_FIRST))
        x = _mixer(x, mod2, mix_norm[l], w_in_b, conv_w[l], toep, wb, wc, tab,
                   ssm_d[l], glu_w_b, glu_b[l], out_norm_conv[l], out_norm_ssm[l], w_out_b)
        x, _ = _ffn(x, mod2, ffn2_norm[l], wg2, wu2, wd2, final_norm, mod_base=3,
                    final_norm=(l == depth - 1))
    return x
```

```python
import functools
import math

import jax
import jax.numpy as jnp
from jax import lax
from jax.experimental import pallas as pl
from jax.experimental.pallas import tpu as pltpu

D_MODEL = 1024
D_CONV = 512
D_SSM = 512
N_CONV_PROJ = 3
CONV_WIDTH = 3
SSM_GROUP_CH = 16
SSM_GROUPS = 32
SSM_STATE = 64
D_FF = 2816
N_MOD = 9
EPS = 1e-6

SUBLANES = 8
LANES = 128
VMEM_LIMIT_BYTES = 56 * 1024 * 1024

MOD_FIRST = 3
MOD_HALF = 512
MOD_ROWS = SUBLANES
FFN_ROWS = 1024
FFN_SUB = 256
FFN_COLS = 256
MIX_ROWS = 1024
MIX_SUB = 256
SSM_CHUNK = 4
GROUPS_PER_SLAB = LANES // SSM_GROUP_CH
N_SLABS = D_SSM // LANES
SLAB_STATE = GROUPS_PER_SLAB * SSM_STATE
SCAN_STEPS = (1, 2, 4)
SSM_OPS_SPLIT = 2

BF16 = jnp.bfloat16
F32 = jnp.float32


def _dot(a, b):
    return jnp.dot(a, b, preferred_element_type=F32)


def _sigmoid(x):
    return 1.0 / (1.0 + jnp.exp(-x))


def _rms_norm(x, gain):
    ms = jnp.mean(x * x, axis=-1, keepdims=True)
    return x * lax.rsqrt(ms + EPS) * gain


def _gelu_tanh(x):
    c = math.sqrt(2.0 / math.pi)
    return x * (0.5 * (1.0 + jnp.tanh(c * (x + 0.044715 * (x * x * x)))))


def _mod_columns(c_ref, w_ref, b_ref):
    c = c_ref[...]
    c = c * _sigmoid(c)
    pad = jnp.zeros((MOD_ROWS - c.shape[0], D_MODEL), F32)
    c = jnp.concatenate([c, pad], axis=0)
    return _dot(c.astype(BF16), w_ref[...].astype(BF16)) + b_ref[...]


def _mod_kernel(c_ref, w_ref, b_ref, o_ref):
    o_ref[0] = _mod_columns(c_ref, w_ref, b_ref)


def _modulation(cond, w_mod, b_mod, n_planes):
    bsz = cond.shape[0]
    assert bsz <= MOD_ROWS
    return pl.pallas_call(
        _mod_kernel,
        grid=(n_planes,),
        in_specs=[
            pl.BlockSpec((bsz, D_MODEL), lambda j: (0, 0)),
            pl.BlockSpec((D_MODEL, D_MODEL), lambda j: (0, j)),
            pl.BlockSpec((1, D_MODEL), lambda j: (0, j)),
        ],
        out_specs=pl.BlockSpec((1, MOD_ROWS, D_MODEL), lambda j: (j, 0, 0)),
        out_shape=jax.ShapeDtypeStruct((n_planes, MOD_ROWS, D_MODEL), F32),
        compiler_params=pltpu.CompilerParams(
            dimension_semantics=("arbitrary",), vmem_limit_bytes=VMEM_LIMIT_BYTES),
        name="adaln_mod",
    )(cond, w_mod, b_mod.reshape(1, -1))


def _ffn_kernel(*refs, mod_base, final_norm, n_cast, later_mod):
    x_ref, mod_ref, gain_ref, wg_ref, wu_ref, wd_ref, fgain_ref = refs[:7]
    n_extra = n_cast + (3 if later_mod else 0)
    cast_in = refs[7:7 + n_cast]
    o_ref = refs[7 + n_extra]
    cast_out = refs[8 + n_extra:8 + n_extra + n_cast]
    act_ref = refs[-1]
    for src, dst in zip(cast_in, cast_out):
        dst[...] = src[...].astype(BF16)
    if later_mod:
        cond_ref, wmod_ref, bmod_ref = refs[7 + n_cast:7 + n_extra]
        refs[8 + n_extra + n_cast][0] = _mod_columns(cond_ref, wmod_ref, bmod_ref)
    batch = pl.ds(pl.program_id(0), 1)
    shift = mod_ref[mod_base, batch, :]
    scale = mod_ref[mod_base + 1, batch, :]
    gate = mod_ref[mod_base + 2, batch, :]
    for r in range(FFN_ROWS // FFN_SUB):
        rows = slice(r * FFN_SUB, (r + 1) * FFN_SUB)
        x = x_ref[rows, :]
        h = (_rms_norm(x, gain_ref[...]) * (1.0 + scale) + shift).astype(BF16)
        for f in range(D_FF // FFN_COLS):
            cols = slice(f * FFN_COLS, (f + 1) * FFN_COLS)
            g = _dot(h, wg_ref[:, cols])
            u = _dot(h, wu_ref[:, cols])
            act_ref[rows, cols] = (g * _sigmoid(g) * u).astype(BF16)
        y = _dot(act_ref[rows, :], wd_ref[...])
        out = x + (0.5 * gate) * y
        if final_norm:
            out = _rms_norm(out, fgain_ref[...])
        o_ref[rows, :] = out


def _const_spec(shape):
    zeros = (0,) * len(shape)
    return pl.BlockSpec(shape, lambda *_: zeros, pipeline_mode=pl.Buffered(1))


def _ffn(x, mod, gain, w_gate, w_up, w_down, final_gain, *, mod_base, final_norm, cast=(),
         later_mod=None):
    bsz, seq, _ = x.shape
    tm = FFN_ROWS
    n_t = seq // tm
    steps = bsz * n_t
    kernel = functools.partial(_ffn_kernel, mod_base=mod_base, final_norm=final_norm,
                               n_cast=len(cast), later_mod=later_mod is not None)
    band_specs = [pl.BlockSpec((w.shape[0] // steps, w.shape[1]), lambda b, t: (b * n_t + t, 0))
                  for w in cast]
    extra_in, extra_in_specs, extra_out_specs, extra_out_shapes = [], [], [], []
    if later_mod is not None:
        cond, w_mod, b_mod, first_plane = later_mod
        n_planes = N_MOD - first_plane
        halves = D_MODEL // MOD_HALF
        assert n_planes * halves <= steps

        def half(b, t):
            return jnp.minimum(b * n_t + t, n_planes * halves - 1)

        extra_in = [cond, w_mod, b_mod.reshape(1, -1)]
        extra_in_specs = [
            _const_spec(cond.shape),
            pl.BlockSpec((D_MODEL, MOD_HALF), lambda b, t: (0, first_plane * halves + half(b, t))),
            pl.BlockSpec((1, MOD_HALF), lambda b, t: (0, first_plane * halves + half(b, t))),
        ]
        extra_out_specs = [pl.BlockSpec((1, MOD_ROWS, MOD_HALF),
                                        lambda b, t: (half(b, t) // halves, 0, half(b, t) % halves))]
        extra_out_shapes = [jax.ShapeDtypeStruct((n_planes, MOD_ROWS, D_MODEL), F32)]
    outs = pl.pallas_call(
        kernel,
        grid=(bsz, n_t),
        in_specs=[
            pl.BlockSpec((None, tm, D_MODEL), lambda b, t: (b, t, 0)),
            _const_spec(mod.shape),
            _const_spec((1, D_MODEL)),
            _const_spec((D_MODEL, D_FF)),
            _const_spec((D_MODEL, D_FF)),
            _const_spec((D_FF, D_MODEL)),
            _const_spec((1, D_MODEL)),
        ] + band_specs + extra_in_specs,
        out_specs=[pl.BlockSpec((None, tm, D_MODEL), lambda b, t: (b, t, 0))] + band_specs
                  + extra_out_specs,
        out_shape=[jax.ShapeDtypeStruct(x.shape, F32)]
                  + [jax.ShapeDtypeStruct(w.shape, BF16) for w in cast] + extra_out_shapes,
        scratch_shapes=[pltpu.VMEM((tm, D_FF), BF16)],
        compiler_params=pltpu.CompilerParams(
            dimension_semantics=("arbitrary", "arbitrary"),
            vmem_limit_bytes=VMEM_LIMIT_BYTES),
        name="ffn_final" if final_norm else "ffn",
    )(x, mod, gain.reshape(1, D_MODEL), w_gate, w_up, w_down,
      final_gain.reshape(1, D_MODEL), *cast, *extra_in)
    return outs[0], tuple(outs[1:])


def _mixer_kernel(x_ref, mod_ref, gain_ref, win_ref, convw_ref, toep_ref, wb_ref, wc_ref,
                  tab_ref, dskip_ref, gluw_ref, glub_ref, gain_c_ref, gain_s_ref, wout_ref,
                  o_ref, zbuf, ubuf, ybuf, hbuf, carry, ymix):
    tm = MIX_ROWS
    chunk = SSM_CHUNK
    m_rows = tm // chunk
    halo = SUBLANES

    @pl.when(pl.program_id(1) == 0)
    def _():
        zbuf[0:halo, :] = jnp.zeros((halo, D_CONV), F32)
        ubuf[:, 0:halo, :] = jnp.zeros((N_SLABS, halo, LANES), F32)
        carry[...] = jnp.zeros(carry.shape, F32)

    x = x_ref[...]
    batch = pl.ds(pl.program_id(0), 1)
    shift = mod_ref[0, batch, :]
    scale = mod_ref[1, batch, :]
    gate = mod_ref[2, batch, :]
    h = (_rms_norm(x, gain_ref[...]) * (1.0 + scale) + shift).astype(BF16)


    u = _dot(h, win_ref[:, 3 * D_CONV:])
    for q in range(N_SLABS):
        ubuf[q, halo:halo + tm, :] = u[:, q * LANES:(q + 1) * LANES]
    u_curs, u_prvs = [], []
    for q in range(N_SLABS):
        cur = [ubuf[q, pl.ds(halo + s, m_rows, stride=chunk), :] for s in range(chunk)]
        prv = [ubuf[q, pl.ds(halo - chunk + s, m_rows, stride=chunk), :] for s in range(chunk)]
        u_curs.append(jnp.concatenate(cur, axis=1).astype(BF16))
        u_prvs.append(jnp.concatenate(prv, axis=1).astype(BF16))
        ubuf[q, 0:halo, :] = ubuf[q, tm:tm + halo, :]

    def chunk_outputs(q):
        y2 = _dot(u_curs[q], toep_ref[q]) + _dot(hbuf[q].astype(BF16), wc_ref[q])
        for s in range(chunk):
            ybuf[q, pl.ds(s, m_rows, stride=chunk), :] = y2[:, s * LANES:(s + 1) * LANES]

    conv_proj = []
    for q in range(N_SLABS):
        inj = _dot(u_prvs[q], wb_ref[q])
        if q < N_CONV_PROJ:
            conv_proj.append(_dot(h, win_ref[:, q * D_CONV:(q + 1) * D_CONV]))
        for i in range(SLAB_STATE // LANES):
            lanes_re = slice(i * LANES, (i + 1) * LANES)
            lanes_im = slice(SLAB_STATE + i * LANES, SLAB_STATE + (i + 1) * LANES)
            re = inj[:, lanes_re].reshape(m_rows // SUBLANES, SUBLANES, LANES)
            im = inj[:, lanes_im].reshape(m_rows // SUBLANES, SUBLANES, LANES)
            for k, step in enumerate(SCAN_STEPS):
                a_re = tab_ref[q, 2 * k, :, lanes_re]
                a_im = tab_ref[q, 2 * k + 1, :, lanes_re]
                s_re = pltpu.roll(re, step, axis=1)
                s_im = pltpu.roll(im, step, axis=1)
                re, im = re + a_re * s_re - a_im * s_im, im + a_re * s_im + a_im * s_re
            p_re = tab_ref[q, 2 * len(SCAN_STEPS), :, lanes_re]
            p_im = tab_ref[q, 2 * len(SCAN_STEPS) + 1, :, lanes_re]
            c_re = carry[q:q + 1, lanes_re]
            c_im = carry[q:q + 1, lanes_im]
            for r in range(m_rows // SUBLANES):
                rows = slice(r * SUBLANES, (r + 1) * SUBLANES)
                b_re = jnp.broadcast_to(c_re, (SUBLANES, LANES))
                b_im = jnp.broadcast_to(c_im, (SUBLANES, LANES))
                f_re = re[r] + p_re * b_re - p_im * b_im
                f_im = im[r] + p_re * b_im + p_im * b_re
                hbuf[q, rows, lanes_re] = f_re
                hbuf[q, rows, lanes_im] = f_im
                c_re = f_re[SUBLANES - 1:SUBLANES, :]
                c_im = f_im[SUBLANES - 1:SUBLANES, :]
            carry[q:q + 1, lanes_re] = c_re
            carry[q:q + 1, lanes_im] = c_im

    for q in range(N_SLABS):
        chunk_outputs(q)
    b_gate, c_gate, v = conv_proj
    z = c_gate * v
    zbuf[halo:halo + tm, :] = z
    z1 = zbuf[halo - 1:halo - 1 + tm, :]
    z2 = zbuf[halo - 2:halo - 2 + tm, :]
    y_conv = b_gate * (convw_ref[0:1, :] * z2 + convw_ref[1:2, :] * z1 + convw_ref[2:3, :] * z)
    zbuf[0:halo, :] = zbuf[tm:tm + halo, :]
    ymix[:, 0:D_CONV] = _rms_norm(y_conv, gain_c_ref[...]).astype(BF16)

    for r in range(MIX_ROWS // MIX_SUB):
        rows = slice(r * MIX_SUB, (r + 1) * MIX_SUB)
        u_rows = slice(halo + r * MIX_SUB, halo + (r + 1) * MIX_SUB)
        y = jnp.concatenate([ybuf[q, rows, :] for q in range(N_SLABS)], axis=1)
        y = y + dskip_ref[...] * jnp.concatenate([ubuf[q, u_rows, :] for q in range(N_SLABS)], axis=1)
        y = _gelu_tanh(y)
        y = y * _sigmoid(_dot(y.astype(BF16), gluw_ref[...]) + glub_ref[...])
        ymix[rows, D_CONV:] = _rms_norm(y, gain_s_ref[...]).astype(BF16)
        o_ref[rows, :] = x_ref[rows, :] + gate * _dot(ymix[rows, :], wout_ref[...])


def _mixer(x, mod, gain, w_in, conv_w, toep, wb, wc, tab, d_skip, glu_w, glu_b,
           gain_c, gain_s, w_out):
    bsz, seq, _ = x.shape
    tm = MIX_ROWS
    m_rows = tm // SSM_CHUNK
    return pl.pallas_call(
        _mixer_kernel,
        grid=(bsz, seq // tm),
        in_specs=[
            pl.BlockSpec((None, tm, D_MODEL), lambda b, t: (b, t, 0)),
            _const_spec(mod.shape),
            _const_spec((1, D_MODEL)),
            _const_spec(w_in.shape),
            _const_spec(conv_w.shape),
            _const_spec(toep.shape),
            _const_spec(wb.shape),
            _const_spec(wc.shape),
            _const_spec(tab.shape),
            _const_spec((1, D_SSM)),
            _const_spec(glu_w.shape),
            _const_spec((1, D_SSM)),
            _const_spec((1, D_CONV)),
            _const_spec((1, D_SSM)),
            _const_spec(w_out.shape),
        ],
        out_specs=pl.BlockSpec((None, tm, D_MODEL), lambda b, t: (b, t, 0)),
        out_shape=jax.ShapeDtypeStruct(x.shape, F32),
        scratch_shapes=[
            pltpu.VMEM((SUBLANES + tm, D_CONV), F32),
            pltpu.VMEM((N_SLABS, SUBLANES + tm, LANES), F32),
            pltpu.VMEM((N_SLABS, tm, LANES), F32),
            pltpu.VMEM((N_SLABS, m_rows, 2 * SLAB_STATE), F32),
            pltpu.VMEM((N_SLABS, 2 * SLAB_STATE), F32),
            pltpu.VMEM((tm, D_MODEL), BF16),
        ],
        compiler_params=pltpu.CompilerParams(
            dimension_semantics=("arbitrary", "arbitrary"),
            vmem_limit_bytes=VMEM_LIMIT_BYTES),
        name="mixer",
    )(x, mod, gain.reshape(1, D_MODEL), w_in, conv_w, toep, wb, wc, tab,
      d_skip.reshape(1, D_SSM), glu_w, glu_b.reshape(1, D_SSM),
      gain_c.reshape(1, D_CONV), gain_s.reshape(1, D_SSM), w_out)


def _cmul(ar, ai, br, bi):
    return ar * br - ai * bi, ar * bi + ai * br


def _ssm_ops_kernel(*refs, n_cast):
    cast_in = refs[3:3 + n_cast]
    cast_out = refs[7 + n_cast:]
    for src, dst in zip(cast_in, cast_out):
        dst[...] = src[...].astype(BF16)

    @pl.when(lax.rem(pl.program_id(0), SSM_OPS_SPLIT) == 0)
    def _():
        _ssm_ops_build(*refs[:3], *refs[3 + n_cast:7 + n_cast])


def _ssm_ops_build(bt_ref, ct_ref, base_ref, toep_ref, wb_ref, wc_ref, tab_ref):
    L = SSM_CHUNK
    bar = base_ref[0, 0:1, :], base_ref[1, 0:1, :]
    coef = base_ref[0, 1:2, :], base_ref[1, 1:2, :]
    pows = [(jnp.ones((1, SLAB_STATE), F32), jnp.zeros((1, SLAB_STATE), F32))]
    for _ in range(SUBLANES * L):
        pows.append(_cmul(*pows[-1], *bar))

    row = lax.broadcasted_iota(jnp.int32, (SUBLANES, SLAB_STATE), 0)
    for k, step in enumerate(SCAN_STEPS):
        for part in range(2):
            tab_ref[2 * k + part] = jnp.where(row >= step, pows[L * step][part], 0.0)
    for part in range(2):
        plane = jnp.zeros((SUBLANES, SLAB_STATE), F32)
        for t in range(SUBLANES):
            plane = jnp.where(row == t, pows[L * (t + 1)][part], plane)
        tab_ref[2 * len(SCAN_STEPS) + part] = plane

    shape = (LANES, SLAB_STATE)
    row_group = lax.broadcasted_iota(jnp.int32, shape, 0) >> (SSM_GROUP_CH.bit_length() - 1)
    col_group = lax.broadcasted_iota(jnp.int32, shape, 1) >> (SSM_STATE.bit_length() - 1)
    same_group = row_group == col_group
    tile_p = (lax.broadcasted_iota(jnp.int32, (SSM_STATE, SLAB_STATE), 0)
              == (lax.broadcasted_iota(jnp.int32, (SSM_STATE, SLAB_STATE), 1) & (SSM_STATE - 1))
              ).astype(BF16)

    def split(a):
        head = a.astype(BF16)
        return head, (a - head.astype(F32)).astype(BF16)

    def block_diag(a):
        head, rest = split(a)
        return jnp.where(same_group, _dot(head, tile_p) + _dot(rest, tile_p), 0.0)

    bt_re, bt_im = block_diag(bt_ref[0]), block_diag(bt_ref[1])
    ct_re, ct_im = block_diag(ct_ref[0]), block_diag(ct_ref[1])
    ct_re_parts, ct_im_parts = split(ct_re), split(ct_im)

    def dot_nt(a, b_parts):
        nt = (((1,), (1,)), ((), ()))
        a_head, a_rest = split(a)
        b_head, b_rest = b_parts
        return (lax.dot_general(a_head, b_head, nt, preferred_element_type=F32)
                + lax.dot_general(a_rest, b_head, nt, preferred_element_type=F32)
                + lax.dot_general(a_head, b_rest, nt, preferred_element_type=F32))

    taps = []
    for m in range(L):
        s = L - 1 - m
        w_re, w_im = _cmul(bt_re, bt_im, *_cmul(*coef, *pows[m]))
        wb_ref[s * LANES:(s + 1) * LANES, 0:SLAB_STATE] = w_re.astype(BF16)
        wb_ref[s * LANES:(s + 1) * LANES, SLAB_STATE:] = w_im.astype(BF16)
        taps.append(dot_nt(w_re, ct_re_parts) - dot_nt(w_im, ct_im_parts))
    for s in range(L):
        for t in range(L):
            blk = taps[t - s] if t >= s else jnp.zeros((LANES, LANES), F32)
            toep_ref[s * LANES:(s + 1) * LANES, t * LANES:(t + 1) * LANES] = blk.astype(BF16)
    for t in range(L):
        o_re, o_im = _cmul(ct_re, ct_im, *pows[t + 1])
        wc_ref[0:SLAB_STATE, t * LANES:(t + 1) * LANES] = o_re.T.astype(BF16)
        wc_ref[SLAB_STATE:, t * LANES:(t + 1) * LANES] = (-o_im).T.astype(BF16)


def _ssm_operators(lambda_re, lambda_im, log_dt, b_re, b_im, c_re, c_im, cast=()):
    L = SSM_CHUNK
    lre, lim = lambda_re.astype(F32), lambda_im.astype(F32)
    dt = jnp.exp(log_dt.astype(F32))[:, None]
    mag = jnp.exp(lre * dt)
    bar_re, bar_im = mag * jnp.cos(lim * dt), mag * jnp.sin(lim * dt)
    den = lre * lre + lim * lim
    coef_re = ((bar_re - 1.0) * lre + bar_im * lim) / den
    coef_im = (bar_im * lre - (bar_re - 1.0) * lim) / den

    def slab(a):
        return a.reshape(N_SLABS, SLAB_STATE)

    zero = jnp.zeros((N_SLABS, SUBLANES - 2, SLAB_STATE), F32)
    base = jnp.stack([
        jnp.concatenate([slab(bar_re)[:, None], slab(coef_re)[:, None], zero], axis=1),
        jnp.concatenate([slab(bar_im)[:, None], slab(coef_im)[:, None], zero], axis=1)], axis=1)

    def compact(re, im):
        return jnp.stack([re, im], axis=1).astype(F32).reshape(
            N_SLABS, GROUPS_PER_SLAB, 2, SSM_GROUP_CH, SSM_STATE).transpose(0, 2, 1, 3, 4).reshape(
            N_SLABS, 2, LANES, SSM_STATE)

    bt = compact(b_re.transpose(0, 2, 1), b_im.transpose(0, 2, 1))
    ct = compact(c_re, c_im)

    n_planes = 2 * (len(SCAN_STEPS) + 1)
    steps = N_SLABS * SSM_OPS_SPLIT
    band_specs = [pl.BlockSpec((w.shape[0] // steps, w.shape[1]), lambda i: (i, 0)) for w in cast]

    def slab3(i):
        return (i // SSM_OPS_SPLIT, 0, 0)

    def slab4(i):
        return (i // SSM_OPS_SPLIT, 0, 0, 0)

    outs = pl.pallas_call(
        functools.partial(_ssm_ops_kernel, n_cast=len(cast)),
        grid=(steps,),
        in_specs=[
            pl.BlockSpec((None, 2, LANES, SSM_STATE), slab4),
            pl.BlockSpec((None, 2, LANES, SSM_STATE), slab4),
            pl.BlockSpec((None, 2, SUBLANES, SLAB_STATE), slab4),
        ] + band_specs,
        out_specs=[
            pl.BlockSpec((None, L * LANES, L * LANES), slab3),
            pl.BlockSpec((None, L * LANES, 2 * SLAB_STATE), slab3),
            pl.BlockSpec((None, 2 * SLAB_STATE, L * LANES), slab3),
            pl.BlockSpec((None, n_planes, SUBLANES, SLAB_STATE), slab4),
        ] + band_specs,
        out_shape=[
            jax.ShapeDtypeStruct((N_SLABS, L * LANES, L * LANES), BF16),
            jax.ShapeDtypeStruct((N_SLABS, L * LANES, 2 * SLAB_STATE), BF16),
            jax.ShapeDtypeStruct((N_SLABS, 2 * SLAB_STATE, L * LANES), BF16),
            jax.ShapeDtypeStruct((N_SLABS, n_planes, SUBLANES, SLAB_STATE), F32),
        ] + [jax.ShapeDtypeStruct(w.shape, BF16) for w in cast],
        compiler_params=pltpu.CompilerParams(
            dimension_semantics=("arbitrary",), vmem_limit_bytes=VMEM_LIMIT_BYTES),
        name="ssm_ops",
    )(bt, ct, base, *cast)
    return outs[:4], tuple(outs[4:])


def kernel(x, cond, w_mod, b_mod, ffn1_norm, ffn1_w_gate, ffn1_w_up, ffn1_w_down, mix_norm, w_in, conv_w, lambda_re, lambda_im, log_dt, ssm_b_re, ssm_b_im, ssm_c_re, ssm_c_im, ssm_d, glu_w, glu_b, out_norm_conv, out_norm_ssm, w_out, ffn2_norm, ffn2_w_gate, ffn2_w_up, ffn2_w_down, final_norm):
    depth = w_mod.shape[0]
    _, seq, d_model = x.shape
    assert d_model == D_MODEL and seq % FFN_ROWS == 0 and seq % MIX_ROWS == 0
    assert w_mod.shape[1:] == (D_MODEL, N_MOD * D_MODEL) and ffn1_w_gate.shape[1:] == (D_MODEL, D_FF)
    assert w_in.shape[1:] == (D_MODEL, N_CONV_PROJ * D_CONV + D_SSM)
    assert conv_w.shape[1:] == (CONV_WIDTH, D_CONV)
    assert lambda_re.shape[1:] == (SSM_GROUPS, SSM_STATE) and SSM_GROUPS * SSM_GROUP_CH == D_SSM
    for l in range(depth):
        mod1 = _modulation(cond, w_mod[l], b_mod[l], MOD_FIRST)
        (toep, wb, wc, tab), (wg1, wu1, wd1) = _ssm_operators(
            lambda_re[l], lambda_im[l], log_dt[l], ssm_b_re[l], ssm_b_im[l], ssm_c_re[l],
            ssm_c_im[l], cast=(ffn1_w_gate[l], ffn1_w_up[l], ffn1_w_down[l]))
        later = (w_in[l], glu_w[l], w_out[l], ffn2_w_gate[l], ffn2_w_up[l], ffn2_w_down[l])
        x, (w_in_b, glu_w_b, w_out_b, wg2, wu2, wd2, mod2) = _ffn(
            x, mod1, ffn1_norm[l], wg1, wu1, wd1, final_norm, mod_base=0, final_norm=False,
            cast=later, later_mod=(cond, w_mod[l], b_mod[l], MOD_FIRST))
        x = _mixer(x, mod2, mix_norm[l], w_in_b, conv_w[l], toep, wb, wc, tab,
                   ssm_d[l], glu_w_b, glu_b[l], out_norm_conv[l], out_norm_ssm[l], w_out_b)
        x, _ = _ffn(x, mod2, ffn2_norm[l], wg2, wu2, wd2, final_norm, mod_base=3,
                    final_norm=(l == depth - 1))
    return x
```

```python
import functools
import math

import jax
import jax.numpy as jnp
from jax import lax
from jax.experimental import pallas as pl
from jax.experimental.pallas import tpu as pltpu

D_MODEL = 1024
D_CONV = 512
D_SSM = 512
N_CONV_PROJ = 3
CONV_WIDTH = 3
SSM_GROUP_CH = 16
SSM_GROUPS = 32
SSM_STATE = 64
D_FF = 2816
N_MOD = 9
EPS = 1e-6

SUBLANES = 8
LANES = 128
VMEM_LIMIT_BYTES = 56 * 1024 * 1024

MOD_FIRST = 3
MOD_HALF = 512
MOD_ROWS = SUBLANES
FFN_ROWS = 1024
FFN_SUB = 256
FFN_COLS = 256
MIX_ROWS = 1024
MIX_SUB = 256
SSM_CHUNK = 4
GROUPS_PER_SLAB = LANES // SSM_GROUP_CH
N_SLABS = D_SSM // LANES
SLAB_STATE = GROUPS_PER_SLAB * SSM_STATE
SCAN_STEPS = (1, 2, 4)

BF16 = jnp.bfloat16
F32 = jnp.float32


def _dot(a, b):
    return jnp.dot(a, b, preferred_element_type=F32)


def _sigmoid(x):
    return 1.0 / (1.0 + jnp.exp(-x))


def _rms_norm(x, gain):
    ms = jnp.mean(x * x, axis=-1, keepdims=True)
    return x * lax.rsqrt(ms + EPS) * gain


def _gelu_tanh(x):
    c = math.sqrt(2.0 / math.pi)
    return x * (0.5 * (1.0 + jnp.tanh(c * (x + 0.044715 * (x * x * x)))))


def _mod_columns(c_ref, w_ref, b_ref):
    c = c_ref[...]
    c = c * _sigmoid(c)
    pad = jnp.zeros((MOD_ROWS - c.shape[0], D_MODEL), F32)
    c = jnp.concatenate([c, pad], axis=0)
    return _dot(c.astype(BF16), w_ref[...].astype(BF16)) + b_ref[...]


def _mod_kernel(c_ref, w_ref, b_ref, o_ref):
    o_ref[0] = _mod_columns(c_ref, w_ref, b_ref)


def _modulation(cond, w_mod, b_mod, n_planes):
    bsz = cond.shape[0]
    assert bsz <= MOD_ROWS
    return pl.pallas_call(
        _mod_kernel,
        grid=(n_planes,),
        in_specs=[
            pl.BlockSpec((bsz, D_MODEL), lambda j: (0, 0)),
            pl.BlockSpec((D_MODEL, D_MODEL), lambda j: (0, j)),
            pl.BlockSpec((1, D_MODEL), lambda j: (0, j)),
        ],
        out_specs=pl.BlockSpec((1, MOD_ROWS, D_MODEL), lambda j: (j, 0, 0)),
        out_shape=jax.ShapeDtypeStruct((n_planes, MOD_ROWS, D_MODEL), F32),
        compiler_params=pltpu.CompilerParams(
            dimension_semantics=("arbitrary",), vmem_limit_bytes=VMEM_LIMIT_BYTES),
        name="adaln_mod",
    )(cond, w_mod, b_mod.reshape(1, -1))


def _ffn_kernel(*refs, mod_base, final_norm, n_cast, later_mod):
    x_ref, mod_ref, gain_ref, wg_ref, wu_ref, wd_ref, fgain_ref = refs[:7]
    n_extra = n_cast + (3 if later_mod else 0)
    cast_in = refs[7:7 + n_cast]
    o_ref = refs[7 + n_extra]
    cast_out = refs[8 + n_extra:8 + n_extra + n_cast]
    act_ref = refs[-1]
    for src, dst in zip(cast_in, cast_out):
        dst[...] = src[...].astype(BF16)
    if later_mod:
        cond_ref, wmod_ref, bmod_ref = refs[7 + n_cast:7 + n_extra]
        refs[8 + n_extra + n_cast][0] = _mod_columns(cond_ref, wmod_ref, bmod_ref)
    batch = pl.ds(pl.program_id(0), 1)
    shift = mod_ref[mod_base, batch, :]
    scale = mod_ref[mod_base + 1, batch, :]
    gate = mod_ref[mod_base + 2, batch, :]
    for r in range(FFN_ROWS // FFN_SUB):
        rows = slice(r * FFN_SUB, (r + 1) * FFN_SUB)
        x = x_ref[rows, :]
        h = (_rms_norm(x, gain_ref[...]) * (1.0 + scale) + shift).astype(BF16)
        for f in range(D_FF // FFN_COLS):
            cols = slice(f * FFN_COLS, (f + 1) * FFN_COLS)
            g = _dot(h, wg_ref[:, cols])
            u = _dot(h, wu_ref[:, cols])
            act_ref[rows, cols] = (g * _sigmoid(g) * u).astype(BF16)
        y = _dot(act_ref[rows, :], wd_ref[...])
        out = x + (0.5 * gate) * y
        if final_norm:
            out = _rms_norm(out, fgain_ref[...])
        o_ref[rows, :] = out


def _const_spec(shape):
    zeros = (0,) * len(shape)
    return pl.BlockSpec(shape, lambda *_: zeros, pipeline_mode=pl.Buffered(1))


def _ffn(x, mod, gain, w_gate, w_up, w_down, final_gain, *, mod_base, final_norm, cast=(),
         later_mod=None):
    bsz, seq, _ = x.shape
    tm = FFN_ROWS
    n_t = seq // tm
    steps = bsz * n_t
    kernel = functools.partial(_ffn_kernel, mod_base=mod_base, final_norm=final_norm,
                               n_cast=len(cast), later_mod=later_mod is not None)
    band_specs = [pl.BlockSpec((w.shape[0] // steps, w.shape[1]), lambda b, t: (b * n_t + t, 0))
                  for w in cast]
    extra_in, extra_in_specs, extra_out_specs, extra_out_shapes = [], [], [], []
    if later_mod is not None:
        cond, w_mod, b_mod, first_plane = later_mod
        n_planes = N_MOD - first_plane
        halves = D_MODEL // MOD_HALF
        assert n_planes * halves <= steps

        def half(b, t):
            return jnp.minimum(b * n_t + t, n_planes * halves - 1)

        extra_in = [cond, w_mod, b_mod.reshape(1, -1)]
        extra_in_specs = [
            _const_spec(cond.shape),
            pl.BlockSpec((D_MODEL, MOD_HALF), lambda b, t: (0, first_plane * halves + half(b, t))),
            pl.BlockSpec((1, MOD_HALF), lambda b, t: (0, first_plane * halves + half(b, t))),
        ]
        extra_out_specs = [pl.BlockSpec((1, MOD_ROWS, MOD_HALF),
                                        lambda b, t: (half(b, t) // halves, 0, half(b, t) % halves))]
        extra_out_shapes = [jax.ShapeDtypeStruct((n_planes, MOD_ROWS, D_MODEL), F32)]
    outs = pl.pallas_call(
        kernel,
        grid=(bsz, n_t),
        in_specs=[
            pl.BlockSpec((None, tm, D_MODEL), lambda b, t: (b, t, 0)),
            _const_spec(mod.shape),
            _const_spec((1, D_MODEL)),
            _const_spec((D_MODEL, D_FF)),
            _const_spec((D_MODEL, D_FF)),
            _const_spec((D_FF, D_MODEL)),
            _const_spec((1, D_MODEL)),
        ] + band_specs + extra_in_specs,
        out_specs=[pl.BlockSpec((None, tm, D_MODEL), lambda b, t: (b, t, 0))] + band_specs
                  + extra_out_specs,
        out_shape=[jax.ShapeDtypeStruct(x.shape, F32)]
                  + [jax.ShapeDtypeStruct(w.shape, BF16) for w in cast] + extra_out_shapes,
        scratch_shapes=[pltpu.VMEM((tm, D_FF), BF16)],
        compiler_params=pltpu.CompilerParams(
            dimension_semantics=("arbitrary", "arbitrary"),
            vmem_limit_bytes=VMEM_LIMIT_BYTES),
        name="ffn_final" if final_norm else "ffn",
    )(x, mod, gain.reshape(1, D_MODEL), w_gate, w_up, w_down,
      final_gain.reshape(1, D_MODEL), *cast, *extra_in)
    return outs[0], tuple(outs[1:])


def _mixer_kernel(x_ref, mod_ref, gain_ref, win_ref, convw_ref, rd_ref, wb_ref,
                  tab_ref, dskip_ref, gluw_ref, glub_ref, gain_c_ref, gain_s_ref, wout_ref,
                  o_ref, zbuf, ubuf, ybuf, hbuf, carry, ymix):
    tm = MIX_ROWS
    chunk = SSM_CHUNK
    m_rows = tm // chunk
    halo = SUBLANES

    @pl.when(pl.program_id(1) == 0)
    def _():
        zbuf[0:halo, :] = jnp.zeros((halo, D_CONV), F32)
        ubuf[:, 0:halo, :] = jnp.zeros((N_SLABS, halo, LANES), F32)
        carry[...] = jnp.zeros(carry.shape, F32)

    x = x_ref[...]
    batch = pl.ds(pl.program_id(0), 1)
    shift = mod_ref[0, batch, :]
    scale = mod_ref[1, batch, :]
    gate = mod_ref[2, batch, :]
    h = (_rms_norm(x, gain_ref[...]) * (1.0 + scale) + shift).astype(BF16)


    u = _dot(h, win_ref[:, 3 * D_CONV:])
    for q in range(N_SLABS):
        ubuf[q, halo:halo + tm, :] = u[:, q * LANES:(q + 1) * LANES]
    u_curs, u_prvs = [], []
    for q in range(N_SLABS):
        cur = [ubuf[q, pl.ds(halo + s, m_rows, stride=chunk), :] for s in range(chunk)]
        prv = [ubuf[q, pl.ds(halo - chunk + s, m_rows, stride=chunk), :] for s in range(chunk)]
        u_curs.append(jnp.concatenate(cur, axis=1).astype(BF16))
        u_prvs.append(jnp.concatenate(prv, axis=1).astype(BF16))
        ubuf[q, 0:halo, :] = ubuf[q, tm:tm + halo, :]

    def chunk_outputs(q):
        lhs = jnp.concatenate([u_curs[q], hbuf[q].astype(BF16)], axis=1)
        y2 = _dot(lhs, rd_ref[q])
        for s in range(chunk):
            ybuf[q, pl.ds(s, m_rows, stride=chunk), :] = y2[:, s * LANES:(s + 1) * LANES]

    conv_proj = []
    for q in range(N_SLABS):
        inj = _dot(u_prvs[q], wb_ref[q])
        if q < N_CONV_PROJ:
            conv_proj.append(_dot(h, win_ref[:, q * D_CONV:(q + 1) * D_CONV]))
        for i in range(SLAB_STATE // LANES):
            lanes_re = slice(i * LANES, (i + 1) * LANES)
            lanes_im = slice(SLAB_STATE + i * LANES, SLAB_STATE + (i + 1) * LANES)
            re = inj[:, lanes_re].reshape(m_rows // SUBLANES, SUBLANES, LANES)
            im = inj[:, lanes_im].reshape(m_rows // SUBLANES, SUBLANES, LANES)
            for k, step in enumerate(SCAN_STEPS):
                a_re = tab_ref[q, 2 * k, :, lanes_re]
                a_im = tab_ref[q, 2 * k + 1, :, lanes_re]
                s_re = pltpu.roll(re, step, axis=1)
                s_im = pltpu.roll(im, step, axis=1)
                re, im = re + a_re * s_re - a_im * s_im, im + a_re * s_im + a_im * s_re
            p_re = tab_ref[q, 2 * len(SCAN_STEPS), :, lanes_re]
            p_im = tab_ref[q, 2 * len(SCAN_STEPS) + 1, :, lanes_re]
            c_re = carry[q:q + 1, lanes_re]
            c_im = carry[q:q + 1, lanes_im]
            for r in range(m_rows // SUBLANES):
                rows = slice(r * SUBLANES, (r + 1) * SUBLANES)
                b_re = jnp.broadcast_to(c_re, (SUBLANES, LANES))
                b_im = jnp.broadcast_to(c_im, (SUBLANES, LANES))
                f_re = re[r] + p_re * b_re - p_im * b_im
                f_im = im[r] + p_re * b_im + p_im * b_re
                hbuf[q, rows, lanes_re] = f_re
                hbuf[q, rows, lanes_im] = f_im
                c_re = f_re[SUBLANES - 1:SUBLANES, :]
                c_im = f_im[SUBLANES - 1:SUBLANES, :]
            carry[q:q + 1, lanes_re] = c_re
            carry[q:q + 1, lanes_im] = c_im

    for q in range(N_SLABS):
        chunk_outputs(q)
    b_gate, c_gate, v = conv_proj
    z = c_gate * v
    zbuf[halo:halo + tm, :] = z
    z1 = zbuf[halo - 1:halo - 1 + tm, :]
    z2 = zbuf[halo - 2:halo - 2 + tm, :]
    y_conv = b_gate * (convw_ref[0:1, :] * z2 + convw_ref[1:2, :] * z1 + convw_ref[2:3, :] * z)
    zbuf[0:halo, :] = zbuf[tm:tm + halo, :]
    ymix[:, 0:D_CONV] = _rms_norm(y_conv, gain_c_ref[...]).astype(BF16)

    for r in range(MIX_ROWS // MIX_SUB):
        rows = slice(r * MIX_SUB, (r + 1) * MIX_SUB)
        u_rows = slice(halo + r * MIX_SUB, halo + (r + 1) * MIX_SUB)
        y = jnp.concatenate([ybuf[q, rows, :] for q in range(N_SLABS)], axis=1)
        y = y + dskip_ref[...] * jnp.concatenate([ubuf[q, u_rows, :] for q in range(N_SLABS)], axis=1)
        y = _gelu_tanh(y)
        y = y * _sigmoid(_dot(y.astype(BF16), gluw_ref[...]) + glub_ref[...])
        ymix[rows, D_CONV:] = _rms_norm(y, gain_s_ref[...]).astype(BF16)
        o_ref[rows, :] = x_ref[rows, :] + gate * _dot(ymix[rows, :], wout_ref[...])


def _mixer(x, mod, gain, w_in, conv_w, rd, wb, tab, d_skip, glu_w, glu_b,
           gain_c, gain_s, w_out):
    bsz, seq, _ = x.shape
    tm = MIX_ROWS
    m_rows = tm // SSM_CHUNK
    return pl.pallas_call(
        _mixer_kernel,
        grid=(bsz, seq // tm),
        in_specs=[
            pl.BlockSpec((None, tm, D_MODEL), lambda b, t: (b, t, 0)),
            _const_spec(mod.shape),
            _const_spec((1, D_MODEL)),
            _const_spec(w_in.shape),
            _const_spec(conv_w.shape),
            _const_spec(rd.shape),
            _const_spec(wb.shape),
            _const_spec(tab.shape),
            _const_spec((1, D_SSM)),
            _const_spec(glu_w.shape),
            _const_spec((1, D_SSM)),
            _const_spec((1, D_CONV)),
            _const_spec((1, D_SSM)),
            _const_spec(w_out.shape),
        ],
        out_specs=pl.BlockSpec((None, tm, D_MODEL), lambda b, t: (b, t, 0)),
        out_shape=jax.ShapeDtypeStruct(x.shape, F32),
        scratch_shapes=[
            pltpu.VMEM((SUBLANES + tm, D_CONV), F32),
            pltpu.VMEM((N_SLABS, SUBLANES + tm, LANES), F32),
            pltpu.VMEM((N_SLABS, tm, LANES), F32),
            pltpu.VMEM((N_SLABS, m_rows, 2 * SLAB_STATE), F32),
            pltpu.VMEM((N_SLABS, 2 * SLAB_STATE), F32),
            pltpu.VMEM((tm, D_MODEL), BF16),
        ],
        compiler_params=pltpu.CompilerParams(
            dimension_semantics=("arbitrary", "arbitrary"),
            vmem_limit_bytes=VMEM_LIMIT_BYTES),
        name="mixer",
    )(x, mod, gain.reshape(1, D_MODEL), w_in, conv_w, rd, wb, tab,
      d_skip.reshape(1, D_SSM), glu_w, glu_b.reshape(1, D_SSM),
      gain_c.reshape(1, D_CONV), gain_s.reshape(1, D_SSM), w_out)


def _cmul(ar, ai, br, bi):
    return ar * br - ai * bi, ar * bi + ai * br


def _ssm_ops_kernel(*refs, n_cast):
    bt_ref, ct_ref, base_ref = refs[:3]
    cast_in = refs[3:3 + n_cast]
    out_ref, wb_ref, tab_ref = refs[3 + n_cast:6 + n_cast]
    cast_out = refs[6 + n_cast:]
    toep_ref = out_ref.at[0:SSM_CHUNK * LANES, :]
    wc_ref = out_ref.at[SSM_CHUNK * LANES:, :]
    for src, dst in zip(cast_in, cast_out):
        dst[...] = src[...].astype(BF16)
    L = SSM_CHUNK
    bar = base_ref[0, 0:1, :], base_ref[1, 0:1, :]
    coef = base_ref[0, 1:2, :], base_ref[1, 1:2, :]
    pows = [(jnp.ones((1, SLAB_STATE), F32), jnp.zeros((1, SLAB_STATE), F32))]
    for _ in range(SUBLANES * L):
        pows.append(_cmul(*pows[-1], *bar))

    row = lax.broadcasted_iota(jnp.int32, (SUBLANES, SLAB_STATE), 0)
    for k, step in enumerate(SCAN_STEPS):
        for part in range(2):
            tab_ref[2 * k + part] = jnp.where(row >= step, pows[L * step][part], 0.0)
    for part in range(2):
        plane = jnp.zeros((SUBLANES, SLAB_STATE), F32)
        for t in range(SUBLANES):
            plane = jnp.where(row == t, pows[L * (t + 1)][part], plane)
        tab_ref[2 * len(SCAN_STEPS) + part] = plane

    shape = (LANES, SLAB_STATE)
    row_group = lax.broadcasted_iota(jnp.int32, shape, 0) >> (SSM_GROUP_CH.bit_length() - 1)
    col_group = lax.broadcasted_iota(jnp.int32, shape, 1) >> (SSM_STATE.bit_length() - 1)
    same_group = row_group == col_group
    tile_p = (lax.broadcasted_iota(jnp.int32, (SSM_STATE, SLAB_STATE), 0)
              == (lax.broadcasted_iota(jnp.int32, (SSM_STATE, SLAB_STATE), 1) & (SSM_STATE - 1))
              ).astype(BF16)

    def split(a):
        head = a.astype(BF16)
        return head, (a - head.astype(F32)).astype(BF16)

    def block_diag(a):
        head, rest = split(a)
        return jnp.where(same_group, _dot(head, tile_p) + _dot(rest, tile_p), 0.0)

    bt_re, bt_im = block_diag(bt_ref[0]), block_diag(bt_ref[1])
    ct_re, ct_im = block_diag(ct_ref[0]), block_diag(ct_ref[1])
    ct_re_parts, ct_im_parts = split(ct_re), split(ct_im)

    def dot_nt(a, b_parts):
        nt = (((1,), (1,)), ((), ()))
        a_head, a_rest = split(a)
        b_head, b_rest = b_parts
        return (lax.dot_general(a_head, b_head, nt, preferred_element_type=F32)
                + lax.dot_general(a_rest, b_head, nt, preferred_element_type=F32)
                + lax.dot_general(a_head, b_rest, nt, preferred_element_type=F32))

    taps = []
    for m in range(L):
        s = L - 1 - m
        w_re, w_im = _cmul(bt_re, bt_im, *_cmul(*coef, *pows[m]))
        wb_ref[s * LANES:(s + 1) * LANES, 0:SLAB_STATE] = w_re.astype(BF16)
        wb_ref[s * LANES:(s + 1) * LANES, SLAB_STATE:] = w_im.astype(BF16)
        taps.append(dot_nt(w_re, ct_re_parts) - dot_nt(w_im, ct_im_parts))
    for s in range(L):
        for t in range(L):
            blk = taps[t - s] if t >= s else jnp.zeros((LANES, LANES), F32)
            toep_ref[s * LANES:(s + 1) * LANES, t * LANES:(t + 1) * LANES] = blk.astype(BF16)
    for t in range(L):
        o_re, o_im = _cmul(ct_re, ct_im, *pows[t + 1])
        wc_ref[0:SLAB_STATE, t * LANES:(t + 1) * LANES] = o_re.T.astype(BF16)
        wc_ref[SLAB_STATE:, t * LANES:(t + 1) * LANES] = (-o_im).T.astype(BF16)


def _ssm_operators(lambda_re, lambda_im, log_dt, b_re, b_im, c_re, c_im, cast=()):
    L = SSM_CHUNK
    lre, lim = lambda_re.astype(F32), lambda_im.astype(F32)
    dt = jnp.exp(log_dt.astype(F32))[:, None]
    mag = jnp.exp(lre * dt)
    bar_re, bar_im = mag * jnp.cos(lim * dt), mag * jnp.sin(lim * dt)
    den = lre * lre + lim * lim
    coef_re = ((bar_re - 1.0) * lre + bar_im * lim) / den
    coef_im = (bar_im * lre - (bar_re - 1.0) * lim) / den

    def slab(a):
        return a.reshape(N_SLABS, SLAB_STATE)

    zero = jnp.zeros((N_SLABS, SUBLANES - 2, SLAB_STATE), F32)
    base = jnp.stack([
        jnp.concatenate([slab(bar_re)[:, None], slab(coef_re)[:, None], zero], axis=1),
        jnp.concatenate([slab(bar_im)[:, None], slab(coef_im)[:, None], zero], axis=1)], axis=1)

    def compact(re, im):
        return jnp.stack([re, im], axis=1).astype(F32).reshape(
            N_SLABS, GROUPS_PER_SLAB, 2, SSM_GROUP_CH, SSM_STATE).transpose(0, 2, 1, 3, 4).reshape(
            N_SLABS, 2, LANES, SSM_STATE)

    bt = compact(b_re.transpose(0, 2, 1), b_im.transpose(0, 2, 1))
    ct = compact(c_re, c_im)

    n_planes = 2 * (len(SCAN_STEPS) + 1)
    band_specs = [pl.BlockSpec((w.shape[0] // N_SLABS, w.shape[1]), lambda q: (q, 0)) for w in cast]
    outs = pl.pallas_call(
        functools.partial(_ssm_ops_kernel, n_cast=len(cast)),
        grid=(N_SLABS,),
        in_specs=[
            pl.BlockSpec((None, 2, LANES, SSM_STATE), lambda q: (q, 0, 0, 0)),
            pl.BlockSpec((None, 2, LANES, SSM_STATE), lambda q: (q, 0, 0, 0)),
            pl.BlockSpec((None, 2, SUBLANES, SLAB_STATE), lambda q: (q, 0, 0, 0)),
        ] + band_specs,
        out_specs=[
            pl.BlockSpec((None, L * LANES + 2 * SLAB_STATE, L * LANES), lambda q: (q, 0, 0)),
            pl.BlockSpec((None, L * LANES, 2 * SLAB_STATE), lambda q: (q, 0, 0)),
            pl.BlockSpec((None, n_planes, SUBLANES, SLAB_STATE), lambda q: (q, 0, 0, 0)),
        ] + band_specs,
        out_shape=[
            jax.ShapeDtypeStruct((N_SLABS, L * LANES + 2 * SLAB_STATE, L * LANES), BF16),
            jax.ShapeDtypeStruct((N_SLABS, L * LANES, 2 * SLAB_STATE), BF16),
            jax.ShapeDtypeStruct((N_SLABS, n_planes, SUBLANES, SLAB_STATE), F32),
        ] + [jax.ShapeDtypeStruct(w.shape, BF16) for w in cast],
        compiler_params=pltpu.CompilerParams(
            dimension_semantics=("arbitrary",), vmem_limit_bytes=VMEM_LIMIT_BYTES),
        name="ssm_ops",
    )(bt, ct, base, *cast)
    return outs[:3], tuple(outs[3:])


def kernel(x, cond, w_mod, b_mod, ffn1_norm, ffn1_w_gate, ffn1_w_up, ffn1_w_down, mix_norm, w_in, conv_w, lambda_re, lambda_im, log_dt, ssm_b_re, ssm_b_im, ssm_c_re, ssm_c_im, ssm_d, glu_w, glu_b, out_norm_conv, out_norm_ssm, w_out, ffn2_norm, ffn2_w_gate, ffn2_w_up, ffn2_w_down, final_norm):
    depth = w_mod.shape[0]
    _, seq, d_model = x.shape
    assert d_model == D_MODEL and seq % FFN_ROWS == 0 and seq % MIX_ROWS == 0
    assert w_mod.shape[1:] == (D_MODEL, N_MOD * D_MODEL) and ffn1_w_gate.shape[1:] == (D_MODEL, D_FF)
    assert w_in.shape[1:] == (D_MODEL, N_CONV_PROJ * D_CONV + D_SSM)
    assert conv_w.shape[1:] == (CONV_WIDTH, D_CONV)
    assert lambda_re.shape[1:] == (SSM_GROUPS, SSM_STATE) and SSM_GROUPS * SSM_GROUP_CH == D_SSM
    for l in range(depth):
        mod1 = _modulation(cond, w_mod[l], b_mod[l], MOD_FIRST)
        (rd, wb, tab), (wg1, wu1, wd1) = _ssm_operators(
            lambda_re[l], lambda_im[l], log_dt[l], ssm_b_re[l], ssm_b_im[l], ssm_c_re[l],
            ssm_c_im[l], cast=(ffn1_w_gate[l], ffn1_w_up[l], ffn1_w_down[l]))
        later = (w_in[l], glu_w[l], w_out[l], ffn2_w_gate[l], ffn2_w_up[l], ffn2_w_down[l])
        x, (w_in_b, glu_w_b, w_out_b, wg2, wu2, wd2, mod2) = _ffn(
            x, mod1, ffn1_norm[l], wg1, wu1, wd1, final_norm, mod_base=0, final_norm=False,
            cast=later, later_mod=(cond, w_mod[l], b_mod[l], MOD_FIRST))
        x = _mixer(x, mod2, mix_norm[l], w_in_b, conv_w[l], rd, wb, tab,
                   ssm_d[l], glu_w_b, glu_b[l], out_norm_conv[l], out_norm_ssm[l], w_out_b)
        x, _ = _ffn(x, mod2, ffn2_norm[l], wg2, wu2, wd2, final_norm, mod_base=3,
                    final_norm=(l == depth - 1))
    return x
```

```python
import functools
import math

import jax
import jax.numpy as jnp
from jax import lax
from jax.experimental import pallas as pl
from jax.experimental.pallas import tpu as pltpu

D_MODEL = 1024
D_CONV = 512
D_SSM = 512
N_CONV_PROJ = 3
CONV_WIDTH = 3
SSM_GROUP_CH = 16
SSM_GROUPS = 32
SSM_STATE = 64
D_FF = 2816
N_MOD = 9
EPS = 1e-6

SUBLANES = 8
LANES = 128
VMEM_LIMIT_BYTES = 56 * 1024 * 1024

MOD_FIRST = 3
MOD_HALF = 512
MOD_ROWS = SUBLANES
FFN_ROWS = 1024
FFN_SUB = 256
FFN_COLS = 256
MIX_ROWS = 1024
MIX_SUB = 256
SSM_CHUNK = 4
GROUPS_PER_SLAB = LANES // SSM_GROUP_CH
N_SLABS = D_SSM // LANES
SLAB_STATE = GROUPS_PER_SLAB * SSM_STATE
SCAN_STEPS = (1, 2, 4)

BF16 = jnp.bfloat16
F32 = jnp.float32


def _dot(a, b):
    return jnp.dot(a, b, preferred_element_type=F32)


def _sigmoid(x):
    return 1.0 / (1.0 + jnp.exp(-x))


def _rms_norm(x, gain):
    ms = jnp.mean(x * x, axis=-1, keepdims=True)
    return x * lax.rsqrt(ms + EPS) * gain


def _gelu_tanh(x):
    c = math.sqrt(2.0 / math.pi)
    return x * (0.5 * (1.0 + jnp.tanh(c * (x + 0.044715 * (x * x * x)))))


def _mod_columns(c_ref, w_ref, b_ref):
    c = c_ref[...]
    c = c * _sigmoid(c)
    pad = jnp.zeros((MOD_ROWS - c.shape[0], D_MODEL), F32)
    c = jnp.concatenate([c, pad], axis=0)
    return _dot(c.astype(BF16), w_ref[...].astype(BF16)) + b_ref[...]


def _mod_kernel(c_ref, w_ref, b_ref, o_ref):
    o_ref[0] = _mod_columns(c_ref, w_ref, b_ref)


def _modulation(cond, w_mod, b_mod, n_planes):
    bsz = cond.shape[0]
    assert bsz <= MOD_ROWS
    return pl.pallas_call(
        _mod_kernel,
        grid=(n_planes,),
        in_specs=[
            pl.BlockSpec((bsz, D_MODEL), lambda j: (0, 0)),
            pl.BlockSpec((D_MODEL, D_MODEL), lambda j: (0, j)),
            pl.BlockSpec((1, D_MODEL), lambda j: (0, j)),
        ],
        out_specs=pl.BlockSpec((1, MOD_ROWS, D_MODEL), lambda j: (j, 0, 0)),
        out_shape=jax.ShapeDtypeStruct((n_planes, MOD_ROWS, D_MODEL), F32),
        compiler_params=pltpu.CompilerParams(
            dimension_semantics=("arbitrary",), vmem_limit_bytes=VMEM_LIMIT_BYTES),
        name="adaln_mod",
    )(cond, w_mod, b_mod.reshape(1, -1))


def _ffn_kernel(*refs, mod_base, final_norm, n_cast, later_mod):
    x_ref, mod_ref, gain_ref, wg_ref, wu_ref, wd_ref, fgain_ref = refs[:7]
    n_extra = n_cast + (3 if later_mod else 0)
    cast_in = refs[7:7 + n_cast]
    o_ref = refs[7 + n_extra]
    cast_out = refs[8 + n_extra:8 + n_extra + n_cast]
    act_ref = refs[-1]
    for src, dst in zip(cast_in, cast_out):
        dst[...] = src[...].astype(BF16)
    if later_mod:
        cond_ref, wmod_ref, bmod_ref = refs[7 + n_cast:7 + n_extra]
        refs[8 + n_extra + n_cast][0] = _mod_columns(cond_ref, wmod_ref, bmod_ref)
    batch = pl.ds(pl.program_id(0), 1)
    shift = mod_ref[mod_base, batch, :]
    scale = mod_ref[mod_base + 1, batch, :]
    gate = mod_ref[mod_base + 2, batch, :]
    for r in range(FFN_ROWS // FFN_SUB):
        rows = slice(r * FFN_SUB, (r + 1) * FFN_SUB)
        x = x_ref[rows, :]
        h = (_rms_norm(x, gain_ref[...]) * (1.0 + scale) + shift).astype(BF16)
        for f in range(D_FF // FFN_COLS):
            cols = slice(f * FFN_COLS, (f + 1) * FFN_COLS)
            g = _dot(h, wg_ref[:, cols])
            u = _dot(h, wu_ref[:, cols])
            act_ref[rows, cols] = (g * _sigmoid(g) * u).astype(BF16)
        y = _dot(act_ref[rows, :], wd_ref[...])
        out = x + (0.5 * gate) * y
        if final_norm:
            out = _rms_norm(out, fgain_ref[...])
        o_ref[rows, :] = out


def _const_spec(shape):
    zeros = (0,) * len(shape)
    return pl.BlockSpec(shape, lambda *_: zeros, pipeline_mode=pl.Buffered(1))


def _ffn(x, mod, gain, w_gate, w_up, w_down, final_gain, *, mod_base, final_norm, cast=(),
         later_mod=None):
    bsz, seq, _ = x.shape
    tm = FFN_ROWS
    n_t = seq // tm
    steps = bsz * n_t
    kernel = functools.partial(_ffn_kernel, mod_base=mod_base, final_norm=final_norm,
                               n_cast=len(cast), later_mod=later_mod is not None)
    band_specs = [pl.BlockSpec((w.shape[0] // steps, w.shape[1]), lambda b, t: (b * n_t + t, 0))
                  for w in cast]
    extra_in, extra_in_specs, extra_out_specs, extra_out_shapes = [], [], [], []
    if later_mod is not None:
        cond, w_mod, b_mod, first_plane = later_mod
        n_planes = N_MOD - first_plane
        halves = D_MODEL // MOD_HALF
        assert n_planes * halves <= steps

        def half(b, t):
            return jnp.minimum(b * n_t + t, n_planes * halves - 1)

        extra_in = [cond, w_mod, b_mod.reshape(1, -1)]
        extra_in_specs = [
            _const_spec(cond.shape),
            pl.BlockSpec((D_MODEL, MOD_HALF), lambda b, t: (0, first_plane * halves + half(b, t))),
            pl.BlockSpec((1, MOD_HALF), lambda b, t: (0, first_plane * halves + half(b, t))),
        ]
        extra_out_specs = [pl.BlockSpec((1, MOD_ROWS, MOD_HALF),
                                        lambda b, t: (half(b, t) // halves, 0, half(b, t) % halves))]
        extra_out_shapes = [jax.ShapeDtypeStruct((n_planes, MOD_ROWS, D_MODEL), F32)]
    outs = pl.pallas_call(
        kernel,
        grid=(bsz, n_t),
        in_specs=[
            pl.BlockSpec((None, tm, D_MODEL), lambda b, t: (b, t, 0)),
            _const_spec(mod.shape),
            _const_spec((1, D_MODEL)),
            _const_spec((D_MODEL, D_FF)),
            _const_spec((D_MODEL, D_FF)),
            _const_spec((D_FF, D_MODEL)),
            _const_spec((1, D_MODEL)),
        ] + band_specs + extra_in_specs,
        out_specs=[pl.BlockSpec((None, tm, D_MODEL), lambda b, t: (b, t, 0))] + band_specs
                  + extra_out_specs,
        out_shape=[jax.ShapeDtypeStruct(x.shape, F32)]
                  + [jax.ShapeDtypeStruct(w.shape, BF16) for w in cast] + extra_out_shapes,
        scratch_shapes=[pltpu.VMEM((tm, D_FF), BF16)],
        compiler_params=pltpu.CompilerParams(
            dimension_semantics=("arbitrary", "arbitrary"),
            vmem_limit_bytes=VMEM_LIMIT_BYTES),
        name="ffn_final" if final_norm else "ffn",
    )(x, mod, gain.reshape(1, D_MODEL), w_gate, w_up, w_down,
      final_gain.reshape(1, D_MODEL), *cast, *extra_in)
    return outs[0], tuple(outs[1:])


def _mixer_kernel(x_ref, mod_ref, gain_ref, win_ref, convw_ref, toep_ref, wb_ref, wc_ref,
                  tab_ref, dskip_ref, gluw_ref, glub_ref, gain_c_ref, gain_s_ref, wout_ref,
                  o_ref, zbuf, ubuf, ybuf, hbuf, carry, ymix):
    tm = MIX_ROWS
    chunk = SSM_CHUNK
    m_rows = tm // chunk
    halo = SUBLANES

    @pl.when(pl.program_id(1) == 0)
    def _():
        zbuf[0:halo, :] = jnp.zeros((halo, D_CONV), F32)
        ubuf[:, 0:halo, :] = jnp.zeros((N_SLABS, halo, LANES), F32)
        carry[...] = jnp.zeros(carry.shape, F32)

    x = x_ref[...]
    batch = pl.ds(pl.program_id(0), 1)
    shift = mod_ref[0, batch, :]
    scale = mod_ref[1, batch, :]
    gate = mod_ref[2, batch, :]
    h = (_rms_norm(x, gain_ref[...]) * (1.0 + scale) + shift).astype(BF16)


    u = _dot(h, win_ref[:, 3 * D_CONV:])
    for q in range(N_SLABS):
        ubuf[q, halo:halo + tm, :] = u[:, q * LANES:(q + 1) * LANES]
    u_curs, u_prvs = [], []
    for q in range(N_SLABS):
        cur = [ubuf[q, pl.ds(halo + s, m_rows, stride=chunk), :] for s in range(chunk)]
        prv = [ubuf[q, pl.ds(halo - chunk + s, m_rows, stride=chunk), :] for s in range(chunk)]
        u_curs.append(jnp.concatenate(cur, axis=1).astype(BF16))
        u_prvs.append(jnp.concatenate(prv, axis=1).astype(BF16))
        ubuf[q, 0:halo, :] = ubuf[q, tm:tm + halo, :]

    def chunk_outputs(q):
        y2 = _dot(u_curs[q], toep_ref[q]) + _dot(hbuf[q].astype(BF16), wc_ref[q])
        for s in range(chunk):
            ybuf[q, pl.ds(s, m_rows, stride=chunk), :] = y2[:, s * LANES:(s + 1) * LANES]

    conv_proj = []
    for q in range(N_SLABS):
        inj = _dot(u_prvs[q], wb_ref[q])
        if q < N_CONV_PROJ:
            conv_proj.append(_dot(h, win_ref[:, q * D_CONV:(q + 1) * D_CONV]))
        for i in range(SLAB_STATE // LANES):
            lanes_re = slice(i * LANES, (i + 1) * LANES)
            lanes_im = slice(SLAB_STATE + i * LANES, SLAB_STATE + (i + 1) * LANES)
            re = inj[:, lanes_re].reshape(m_rows // SUBLANES, SUBLANES, LANES)
            im = inj[:, lanes_im].reshape(m_rows // SUBLANES, SUBLANES, LANES)
            for k, step in enumerate(SCAN_STEPS):
                a_re = tab_ref[q, 2 * k, :, lanes_re]
                a_im = tab_ref[q, 2 * k + 1, :, lanes_re]
                s_re = pltpu.roll(re, step, axis=1)
                s_im = pltpu.roll(im, step, axis=1)
                re, im = re + a_re * s_re - a_im * s_im, im + a_re * s_im + a_im * s_re
            p_re = tab_ref[q, 2 * len(SCAN_STEPS), :, lanes_re]
            p_im = tab_ref[q, 2 * len(SCAN_STEPS) + 1, :, lanes_re]
            c_re = carry[q:q + 1, lanes_re]
            c_im = carry[q:q + 1, lanes_im]
            for r in range(m_rows // SUBLANES):
                rows = slice(r * SUBLANES, (r + 1) * SUBLANES)
                b_re = jnp.broadcast_to(c_re, (SUBLANES, LANES))
                b_im = jnp.broadcast_to(c_im, (SUBLANES, LANES))
                f_re = re[r] + p_re * b_re - p_im * b_im
                f_im = im[r] + p_re * b_im + p_im * b_re
                hbuf[q, rows, lanes_re] = f_re
                hbuf[q, rows, lanes_im] = f_im
                c_re = f_re[SUBLANES - 1:SUBLANES, :]
                c_im = f_im[SUBLANES - 1:SUBLANES, :]
            carry[q:q + 1, lanes_re] = c_re
            carry[q:q + 1, lanes_im] = c_im

    for q in range(N_SLABS):
        chunk_outputs(q)
    b_gate, c_gate, v = conv_proj
    z = c_gate * v
    prev = zbuf[0:halo, :]
    row = lax.broadcasted_iota(jnp.int32, (tm, 1), 0)
    z1 = jnp.where(row == 0, prev[halo - 1:halo, :], pltpu.roll(z, 1, axis=0))
    z2 = jnp.where(row == 0, prev[halo - 2:halo - 1, :],
                   jnp.where(row == 1, prev[halo - 1:halo, :], pltpu.roll(z, 2, axis=0)))
    y_conv = b_gate * (convw_ref[0:1, :] * z2 + convw_ref[1:2, :] * z1 + convw_ref[2:3, :] * z)
    zbuf[0:halo, :] = z[tm - halo:tm, :]
    ymix[:, 0:D_CONV] = _rms_norm(y_conv, gain_c_ref[...]).astype(BF16)

    for r in range(MIX_ROWS // MIX_SUB):
        rows = slice(r * MIX_SUB, (r + 1) * MIX_SUB)
        u_rows = slice(halo + r * MIX_SUB, halo + (r + 1) * MIX_SUB)
        y = jnp.concatenate([ybuf[q, rows, :] for q in range(N_SLABS)], axis=1)
        y = y + dskip_ref[...] * jnp.concatenate([ubuf[q, u_rows, :] for q in range(N_SLABS)], axis=1)
        y = _gelu_tanh(y)
        y = y * _sigmoid(_dot(y.astype(BF16), gluw_ref[...]) + glub_ref[...])
        ymix[rows, D_CONV:] = _rms_norm(y, gain_s_ref[...]).astype(BF16)
        o_ref[rows, :] = x_ref[rows, :] + gate * _dot(ymix[rows, :], wout_ref[...])


def _mixer(x, mod, gain, w_in, conv_w, toep, wb, wc, tab, d_skip, glu_w, glu_b,
           gain_c, gain_s, w_out):
    bsz, seq, _ = x.shape
    tm = MIX_ROWS
    m_rows = tm // SSM_CHUNK
    return pl.pallas_call(
        _mixer_kernel,
        grid=(bsz, seq // tm),
        in_specs=[
            pl.BlockSpec((None, tm, D_MODEL), lambda b, t: (b, t, 0)),
            _const_spec(mod.shape),
            _const_spec((1, D_MODEL)),
            _const_spec(w_in.shape),
            _const_spec(conv_w.shape),
            _const_spec(toep.shape),
            _const_spec(wb.shape),
            _const_spec(wc.shape),
            _const_spec(tab.shape),
            _const_spec((1, D_SSM)),
            _const_spec(glu_w.shape),
            _const_spec((1, D_SSM)),
            _const_spec((1, D_CONV)),
            _const_spec((1, D_SSM)),
            _const_spec(w_out.shape),
        ],
        out_specs=pl.BlockSpec((None, tm, D_MODEL), lambda b, t: (b, t, 0)),
        out_shape=jax.ShapeDtypeStruct(x.shape, F32),
        scratch_shapes=[
            pltpu.VMEM((SUBLANES + tm, D_CONV), F32),
            pltpu.VMEM((N_SLABS, SUBLANES + tm, LANES), F32),
            pltpu.VMEM((N_SLABS, tm, LANES), F32),
            pltpu.VMEM((N_SLABS, m_rows, 2 * SLAB_STATE), F32),
            pltpu.VMEM((N_SLABS, 2 * SLAB_STATE), F32),
            pltpu.VMEM((tm, D_MODEL), BF16),
        ],
        compiler_params=pltpu.CompilerParams(
            dimension_semantics=("arbitrary", "arbitrary"),
            vmem_limit_bytes=VMEM_LIMIT_BYTES),
        name="mixer",
    )(x, mod, gain.reshape(1, D_MODEL), w_in, conv_w, toep, wb, wc, tab,
      d_skip.reshape(1, D_SSM), glu_w, glu_b.reshape(1, D_SSM),
      gain_c.reshape(1, D_CONV), gain_s.reshape(1, D_SSM), w_out)


def _cmul(ar, ai, br, bi):
    return ar * br - ai * bi, ar * bi + ai * br


def _ssm_ops_kernel(*refs, n_cast):
    bt_ref, ct_ref, base_ref = refs[:3]
    cast_in = refs[3:3 + n_cast]
    toep_ref, wb_ref, wc_ref, tab_ref = refs[3 + n_cast:7 + n_cast]
    cast_out = refs[7 + n_cast:]
    for src, dst in zip(cast_in, cast_out):
        dst[...] = src[...].astype(BF16)
    L = SSM_CHUNK
    bar = base_ref[0, 0:1, :], base_ref[1, 0:1, :]
    coef = base_ref[0, 1:2, :], base_ref[1, 1:2, :]
    pows = [(jnp.ones((1, SLAB_STATE), F32), jnp.zeros((1, SLAB_STATE), F32))]
    for _ in range(SUBLANES * L):
        pows.append(_cmul(*pows[-1], *bar))

    row = lax.broadcasted_iota(jnp.int32, (SUBLANES, SLAB_STATE), 0)
    for k, step in enumerate(SCAN_STEPS):
        for part in range(2):
            tab_ref[2 * k + part] = jnp.where(row >= step, pows[L * step][part], 0.0)
    for part in range(2):
        plane = jnp.zeros((SUBLANES, SLAB_STATE), F32)
        for t in range(SUBLANES):
            plane = jnp.where(row == t, pows[L * (t + 1)][part], plane)
        tab_ref[2 * len(SCAN_STEPS) + part] = plane

    shape = (LANES, SLAB_STATE)
    row_group = lax.broadcasted_iota(jnp.int32, shape, 0) >> (SSM_GROUP_CH.bit_length() - 1)
    col_group = lax.broadcasted_iota(jnp.int32, shape, 1) >> (SSM_STATE.bit_length() - 1)
    same_group = row_group == col_group
    tile_p = (lax.broadcasted_iota(jnp.int32, (SSM_STATE, SLAB_STATE), 0)
              == (lax.broadcasted_iota(jnp.int32, (SSM_STATE, SLAB_STATE), 1) & (SSM_STATE - 1))
              ).astype(BF16)

    def split(a):
        head = a.astype(BF16)
        return head, (a - head.astype(F32)).astype(BF16)

    def block_diag(a):
        head, rest = split(a)
        return jnp.where(same_group, _dot(head, tile_p) + _dot(rest, tile_p), 0.0)

    bt_re, bt_im = block_diag(bt_ref[0]), block_diag(bt_ref[1])
    ct_re, ct_im = block_diag(ct_ref[0]), block_diag(ct_ref[1])
    ct_re_parts, ct_im_parts = split(ct_re), split(ct_im)

    def dot_nt(a, b_parts):
        nt = (((1,), (1,)), ((), ()))
        a_head, a_rest = split(a)
        b_head, b_rest = b_parts
        return (lax.dot_general(a_head, b_head, nt, preferred_element_type=F32)
                + lax.dot_general(a_rest, b_head, nt, preferred_element_type=F32)
                + lax.dot_general(a_head, b_rest, nt, preferred_element_type=F32))

    taps = []
    for m in range(L):
        s = L - 1 - m
        w_re, w_im = _cmul(bt_re, bt_im, *_cmul(*coef, *pows[m]))
        wb_ref[s * LANES:(s + 1) * LANES, 0:SLAB_STATE] = w_re.astype(BF16)
        wb_ref[s * LANES:(s + 1) * LANES, SLAB_STATE:] = w_im.astype(BF16)
        taps.append(dot_nt(w_re, ct_re_parts) - dot_nt(w_im, ct_im_parts))
    for s in range(L):
        for t in range(L):
            blk = taps[t - s] if t >= s else jnp.zeros((LANES, LANES), F32)
            toep_ref[s * LANES:(s + 1) * LANES, t * LANES:(t + 1) * LANES] = blk.astype(BF16)
    for t in range(L):
        o_re, o_im = _cmul(ct_re, ct_im, *pows[t + 1])
        wc_ref[0:SLAB_STATE, t * LANES:(t + 1) * LANES] = o_re.T.astype(BF16)
        wc_ref[SLAB_STATE:, t * LANES:(t + 1) * LANES] = (-o_im).T.astype(BF16)


def _ssm_operators(lambda_re, lambda_im, log_dt, b_re, b_im, c_re, c_im, cast=()):
    L = SSM_CHUNK
    lre, lim = lambda_re.astype(F32), lambda_im.astype(F32)
    dt = jnp.exp(log_dt.astype(F32))[:, None]
    mag = jnp.exp(lre * dt)
    bar_re, bar_im = mag * jnp.cos(lim * dt), mag * jnp.sin(lim * dt)
    den = lre * lre + lim * lim
    coef_re = ((bar_re - 1.0) * lre + bar_im * lim) / den
    coef_im = (bar_im * lre - (bar_re - 1.0) * lim) / den

    def slab(a):
        return a.reshape(N_SLABS, SLAB_STATE)

    zero = jnp.zeros((N_SLABS, SUBLANES - 2, SLAB_STATE), F32)
    base = jnp.stack([
        jnp.concatenate([slab(bar_re)[:, None], slab(coef_re)[:, None], zero], axis=1),
        jnp.concatenate([slab(bar_im)[:, None], slab(coef_im)[:, None], zero], axis=1)], axis=1)

    def compact(re, im):
        return jnp.stack([re, im], axis=1).astype(F32).reshape(
            N_SLABS, GROUPS_PER_SLAB, 2, SSM_GROUP_CH, SSM_STATE).transpose(0, 2, 1, 3, 4).reshape(
            N_SLABS, 2, LANES, SSM_STATE)

    bt = compact(b_re.transpose(0, 2, 1), b_im.transpose(0, 2, 1))
    ct = compact(c_re, c_im)

    n_planes = 2 * (len(SCAN_STEPS) + 1)
    band_specs = [pl.BlockSpec((w.shape[0] // N_SLABS, w.shape[1]), lambda q: (q, 0)) for w in cast]
    outs = pl.pallas_call(
        functools.partial(_ssm_ops_kernel, n_cast=len(cast)),
        grid=(N_SLABS,),
        in_specs=[
            pl.BlockSpec((None, 2, LANES, SSM_STATE), lambda q: (q, 0, 0, 0)),
            pl.BlockSpec((None, 2, LANES, SSM_STATE), lambda q: (q, 0, 0, 0)),
            pl.BlockSpec((None, 2, SUBLANES, SLAB_STATE), lambda q: (q, 0, 0, 0)),
        ] + band_specs,
        out_specs=[
            pl.BlockSpec((None, L * LANES, L * LANES), lambda q: (q, 0, 0)),
            pl.BlockSpec((None, L * LANES, 2 * SLAB_STATE), lambda q: (q, 0, 0)),
            pl.BlockSpec((None, 2 * SLAB_STATE, L * LANES), lambda q: (q, 0, 0)),
            pl.BlockSpec((None, n_planes, SUBLANES, SLAB_STATE), lambda q: (q, 0, 0, 0)),
        ] + band_specs,
        out_shape=[
            jax.ShapeDtypeStruct((N_SLABS, L * LANES, L * LANES), BF16),
            jax.ShapeDtypeStruct((N_SLABS, L * LANES, 2 * SLAB_STATE), BF16),
            jax.ShapeDtypeStruct((N_SLABS, 2 * SLAB_STATE, L * LANES), BF16),
            jax.ShapeDtypeStruct((N_SLABS, n_planes, SUBLANES, SLAB_STATE), F32),
        ] + [jax.ShapeDtypeStruct(w.shape, BF16) for w in cast],
        compiler_params=pltpu.CompilerParams(
            dimension_semantics=("arbitrary",), vmem_limit_bytes=VMEM_LIMIT_BYTES),
        name="ssm_ops",
    )(bt, ct, base, *cast)
    return outs[:4], tuple(outs[4:])


def kernel(x, cond, w_mod, b_mod, ffn1_norm, ffn1_w_gate, ffn1_w_up, ffn1_w_down, mix_norm, w_in, conv_w, lambda_re, lambda_im, log_dt, ssm_b_re, ssm_b_im, ssm_c_re, ssm_c_im, ssm_d, glu_w, glu_b, out_norm_conv, out_norm_ssm, w_out, ffn2_norm, ffn2_w_gate, ffn2_w_up, ffn2_w_down, final_norm):
    depth = w_mod.shape[0]
    _, seq, d_model = x.shape
    assert d_model == D_MODEL and seq % FFN_ROWS == 0 and seq % MIX_ROWS == 0
    assert w_mod.shape[1:] == (D_MODEL, N_MOD * D_MODEL) and ffn1_w_gate.shape[1:] == (D_MODEL, D_FF)
    assert w_in.shape[1:] == (D_MODEL, N_CONV_PROJ * D_CONV + D_SSM)
    assert conv_w.shape[1:] == (CONV_WIDTH, D_CONV)
    assert lambda_re.shape[1:] == (SSM_GROUPS, SSM_STATE) and SSM_GROUPS * SSM_GROUP_CH == D_SSM
    for l in range(depth):
        mod1 = _modulation(cond, w_mod[l], b_mod[l], MOD_FIRST)
        (toep, wb, wc, tab), (wg1, wu1, wd1) = _ssm_operators(
            lambda_re[l], lambda_im[l], log_dt[l], ssm_b_re[l], ssm_b_im[l], ssm_c_re[l],
            ssm_c_im[l], cast=(ffn1_w_gate[l], ffn1_w_up[l], ffn1_w_down[l]))
        later = (w_in[l], glu_w[l], w_out[l], ffn2_w_gate[l], ffn2_w_up[l], ffn2_w_down[l])
        x, (w_in_b, glu_w_b, w_out_b, wg2, wu2, wd2, mod2) = _ffn(
            x, mod1, ffn1_norm[l], wg1, wu1, wd1, final_norm, mod_base=0, final_norm=False,
            cast=later, later_mod=(cond, w_mod[l], b_mod[l], MOD_FIRST))
        x = _mixer(x, mod2, mix_norm[l], w_in_b, conv_w[l], toep, wb, wc, tab,
                   ssm_d[l], glu_w_b, glu_b[l], out_norm_conv[l], out_norm_ssm[l], w_out_b)
        x, _ = _ffn(x, mod2, ffn2_norm[l], wg2, wu2, wd2, final_norm, mod_base=3,
                    final_norm=(l == depth - 1))
    return x
```
